```python
import math
import jax
import jax.numpy as jnp
from jax import lax
import numpy as np

D_MODEL = 1024
BATCH = 8
SEQ = 8192
DEPTH = 4

N_MIXERS = 4
D_FF = 2816
NORM_EPS = 1e-6
ROPE_THETA = 500000.0
ROPE_FRACTION = 4
BLOCK = 128
NEG_INF = -1e30

SWA_HEADS = 16
SWA_KV_HEADS = 2
SWA_HEAD_DIM = 64
SWA_WINDOW = 128

GDN_HEADS = 8
GDN_HEAD_DIM = 128
GDN_CONV = 4
GDN_CHUNK = 64

SSD_D_INNER = 2 * D_MODEL
SSD_HEAD_DIM = 64
SSD_HEADS = SSD_D_INNER // SSD_HEAD_DIM
SSD_GROUPS = 4
SSD_STATE = 128
SSD_CONV = 4
SSD_CHUNK = 128

DIL_PATTERN = ((128, 1), (512, 4), (2048, 16))
DIL_HEADS = 16
DIL_KV_HEADS = 4
DIL_HEAD_DIM = 64

kernel_name = "hybrid_interleaved_swa_gdn_ssd_dilated"


def rmsnorm(x, gain):
    xf = x.astype(jnp.float32)
    y = xf * lax.rsqrt(jnp.mean(xf * xf, axis=-1, keepdims=True) + NORM_EPS)
    return (y * gain.astype(jnp.float32)).astype(x.dtype)


def l2norm(x):
    xf = x.astype(jnp.float32)
    return xf * lax.rsqrt(jnp.sum(xf * xf, axis=-1, keepdims=True) + NORM_EPS)


def swiglu(x, w_in, w_out):
    gate, up = jnp.split(x @ w_in, 2, axis=-1)
    return (jax.nn.silu(gate) * up) @ w_out


def rope_tables(positions, head_dim):
    rot = head_dim // ROPE_FRACTION
    inv_freq = ROPE_THETA ** (-jnp.arange(0, rot, 2, dtype=jnp.float32) / rot)
    ang = positions.astype(jnp.float32)[..., None] * inv_freq
    return jnp.cos(ang), jnp.sin(ang)


def apply_partial_rope(x, cos, sin):
    half = cos.shape[-1]
    c = cos[:, :, None, :].astype(x.dtype)
    s = sin[:, :, None, :].astype(x.dtype)
    x1, x2, rest = x[..., :half], x[..., half:2 * half], x[..., 2 * half:]
    return jnp.concatenate([x1 * c - x2 * s, x2 * c + x1 * s, rest], axis=-1)


def causal_depthwise_conv(x, w):
    width, ch = w.shape
    return lax.conv_general_dilated(x, w[:, None, :], window_strides=(1,), padding=[(width - 1, 0)],
                                    dimension_numbers=("NWC", "WIO", "NWC"), feature_group_count=ch)


def to_chunks(t, size):
    b, s = t.shape[:2]
    return jnp.moveaxis(t.reshape(b, s // size, size, *t.shape[2:]), 1, 0)


def banded_attention(q, k, v, n_back):
    b, L, kvh, gq, dh = q.shape
    nb = L // BLOCK
    qb = q.reshape(b, nb, BLOCK, kvh, gq, dh)

    def with_prev(t):
        tb = t.reshape(b, nb, BLOCK, kvh, dh)
        prev = jnp.concatenate([jnp.zeros_like(tb[:, :1]), tb[:, :-1]], axis=1)
        return jnp.concatenate([prev, tb], axis=2)

    kb, vb = with_prev(k), with_prev(v)
    scores = jnp.einsum("bnqkgd,bnskd->bnqkgs", qb, kb).astype(jnp.float32) * (dh ** -0.5)
    q_idx = jnp.arange(BLOCK)[:, None] + BLOCK
    s_idx = jnp.arange(2 * BLOCK)[None, :]
    rel = q_idx - s_idx
    key_pos = (jnp.arange(nb) * BLOCK - BLOCK)[:, None, None] + s_idx[None]
    visible = (rel >= 0) & (rel <= n_back) & (key_pos >= 0)
    scores = jnp.where(visible[None, :, :, None, None, :], scores, NEG_INF)
    m = jnp.max(scores, axis=-1)
    p = jnp.exp(scores - m[..., None])
    l = jnp.sum(p, axis=-1)
    acc = jnp.einsum("bnqkgs,bnskd->bnqkgd", p.astype(v.dtype), vb).astype(jnp.float32)
    return acc.reshape(b, L, kvh, gq, dh), m.reshape(b, L, kvh, gq), l.reshape(b, L, kvh, gq)


def swa_sink_mixer(x, cos, sin, w_in, b_in, sinks, w_out):
    b, s, _ = x.shape
    nh, kvh, dh = SWA_HEADS, SWA_KV_HEADS, SWA_HEAD_DIM
    gq = nh // kvh
    q, k, v = jnp.split(x @ w_in + b_in, [nh * dh, (nh + kvh) * dh], axis=-1)
    q = apply_partial_rope(q.reshape(b, s, nh, dh), cos, sin).reshape(b, s, kvh, gq, dh)
    k = apply_partial_rope(k.reshape(b, s, kvh, dh), cos, sin)
    acc, m, l = banded_attention(q, k, v.reshape(b, s, kvh, dh), SWA_WINDOW - 1)
    sink = sinks.astype(jnp.float32).reshape(kvh, gq)
    m_all = jnp.maximum(m, sink)
    rescale = jnp.exp(m - m_all)
    out = acc * (rescale / (l * rescale + jnp.exp(sink - m_all)))[..., None]
    return out.reshape(b, s, nh * dh).astype(x.dtype) @ w_out


def gated_delta_rule_chunked(q, k, v, g, beta):
    b, s, nh, dk = q.shape
    dv = v.shape[-1]
    c = GDN_CHUNK
    causal = jnp.tril(jnp.ones((c, c), dtype=bool))
    strict = jnp.tril(jnp.ones((c, c), dtype=bool), -1)
    eye = jnp.eye(c, dtype=jnp.float32)

    def heads_first(t):
        return jnp.moveaxis(to_chunks(t.astype(jnp.float32), c), 3, 2)

    def step(state, inp):
        qi, ki, vi, gi, bi = inp
        gcum = jnp.cumsum(gi, axis=-1)
        decay = jnp.exp(jnp.where(causal, gcum[..., :, None] - gcum[..., None, :], -jnp.inf))
        a_mat = eye + jnp.where(strict, jnp.einsum("bhid,bhjd->bhij", ki, ki) * decay * bi[..., :, None], 0.0)
        rhs = jnp.concatenate([vi * bi[..., None], ki * (bi * jnp.exp(gcum))[..., None]], axis=-1)
        sol = lax.linalg.triangular_solve(a_mat, rhs, left_side=True, lower=True, unit_diagonal=True)
        u, w = sol[..., :dv], sol[..., dv:]
        v_new = u - jnp.einsum("bhik,bhkv->bhiv", w, state)
        intra = jnp.where(causal, jnp.einsum("bhid,bhjd->bhij", qi, ki) * decay, 0.0)
        out = (jnp.einsum("bhid,bhdv->bhiv", qi * jnp.exp(gcum)[..., None], state)
               + jnp.einsum("bhij,bhjv->bhiv", intra, v_new))
        g_last = gcum[..., -1]
        state = (state * jnp.exp(g_last)[..., None, None]
                 + jnp.einsum("bhid,bhiv->bhdv", ki * jnp.exp(g_last[..., None] - gcum)[..., None], v_new))
        return state, out

    state0 = jnp.zeros((b, nh, dk, dv), jnp.float32)
    _, o = lax.scan(step, state0, (heads_first(q), heads_first(k), heads_first(v), heads_first(g), heads_first(beta)))
    return o.transpose(1, 0, 3, 2, 4).reshape(b, s, nh, dv)


def gated_deltanet_mixer(x, w_in, conv_w, A_log, dt_bias, norm_w, w_out):
    b, s, _ = x.shape
    nh, dk = GDN_HEADS, GDN_HEAD_DIM
    width = nh * dk
    qkv, z, beta_raw, a_raw = jnp.split(x @ w_in, [3 * width, 4 * width, 4 * width + nh], axis=-1)
    qkv = jax.nn.silu(causal_depthwise_conv(qkv, conv_w))
    q, k, v = jnp.split(qkv, 3, axis=-1)
    q = l2norm(q.reshape(b, s, nh, dk)) * (dk ** -0.5)
    k = l2norm(k.reshape(b, s, nh, dk))
    beta = jax.nn.sigmoid(beta_raw.astype(jnp.float32))
    g = -jnp.exp(A_log.astype(jnp.float32)) * jax.nn.softplus(a_raw.astype(jnp.float32) + dt_bias.astype(jnp.float32))
    o = gated_delta_rule_chunked(q, k, v.reshape(b, s, nh, dk), g, beta)
    o = rmsnorm(o, norm_w) * jax.nn.silu(z.reshape(b, s, nh, dk).astype(jnp.float32))
    return o.reshape(b, s, width).astype(x.dtype) @ w_out


def ssd_chunked(xs, dt, a, bm, cm):
    b, s, g, r, p = xs.shape
    n = bm.shape[-1]
    c = SSD_CHUNK
    causal = jnp.tril(jnp.ones((c, c), dtype=bool))
    x_dt = to_chunks(xs.astype(jnp.float32) * dt[..., None], c)
    a_dt = jnp.moveaxis(to_chunks(dt * a, c), 2, -1)
    bc = to_chunks(bm.astype(jnp.float32), c)
    cc = to_chunks(cm.astype(jnp.float32), c)

    def step(state, inp):
        xi, ai, bi, ci = inp
        acum = jnp.cumsum(ai, axis=-1)
        seg = jnp.exp(jnp.where(causal, acum[..., :, None] - acum[..., None, :], -jnp.inf))
        y_diag = jnp.einsum("blgn,bsgn,bgrls,bsgrp->blgrp", ci, bi, seg, xi)
        y_off = jnp.einsum("blgn,bgrpn,bgrl->blgrp", ci, state, jnp.exp(acum))
        a_last = acum[..., -1]
        state = (state * jnp.exp(a_last)[..., None, None]
                 + jnp.einsum("bsgn,bgrs,bsgrp->bgrpn", bi, jnp.exp(a_last[..., None] - acum), xi))
        return state, y_diag + y_off

    state0 = jnp.zeros((b, g, r, p, n), jnp.float32)
    _, y = lax.scan(step, state0, (x_dt, a_dt, bc, cc))
    return jnp.moveaxis(y, 0, 1).reshape(b, s, g, r, p)


def mamba2_ssd_mixer(x, w_in, conv_w, conv_b, dt_bias, A_log, D_skip, norm_w, w_out):
    b, s, _ = x.shape
    g, r, p, n = SSD_GROUPS, SSD_HEADS // SSD_GROUPS, SSD_HEAD_DIM, SSD_STATE
    di = SSD_D_INNER
    z, xbc, dt_raw = jnp.split(x @ w_in, [di, 2 * di + 2 * g * n], axis=-1)
    xbc = jax.nn.silu(causal_depthwise_conv(xbc, conv_w) + conv_b)
    xs, bm, cm = jnp.split(xbc, [di, di + g * n], axis=-1)
    xs = xs.reshape(b, s, g, r, p)
    dt = jax.nn.softplus(dt_raw.astype(jnp.float32) + dt_bias.astype(jnp.float32)).reshape(b, s, g, r)
    a = -jnp.exp(A_log.astype(jnp.float32)).reshape(g, r)
    y = ssd_chunked(xs, dt, a, bm.reshape(b, s, g, n), cm.reshape(b, s, g, n))
    y = y + xs.astype(jnp.float32) * D_skip.astype(jnp.float32).reshape(g, r, 1)
    y = y.reshape(b, s, di) * jax.nn.silu(z.astype(jnp.float32))
    y = rmsnorm(y.reshape(b, s, g, di // g), norm_w.reshape(g, di // g))
    return y.reshape(b, s, di).astype(x.dtype) @ w_out


def to_residues(t, d):
    b, s = t.shape[:2]
    lsub = s // d
    lpad = -(-lsub // BLOCK) * BLOCK
    t = jnp.moveaxis(t.reshape(b, lsub, d, *t.shape[2:]), 2, 1).reshape(b * d, lsub, *t.shape[2:])
    return jnp.pad(t, [(0, 0), (0, lpad - lsub)] + [(0, 0)] * (t.ndim - 2))


def from_residues(t, d, b, s):
    lsub = s // d
    t = t[:, :lsub].reshape(b, d, lsub, *t.shape[2:])
    return jnp.moveaxis(t, 1, 2).reshape(b, s, *t.shape[3:])


def dilated_mixer(x, cos, sin, w_in, w_out):
    b, s, _ = x.shape
    nh, kvh, dh = DIL_HEADS, DIL_KV_HEADS, DIL_HEAD_DIM
    gq = nh // kvh
    per_group = (nh + 2 * kvh) * dh
    proj = x @ w_in
    outs, lses = [], []
    for gi, (window, dilation) in enumerate(DIL_PATTERN):
        q, k, v = jnp.split(proj[..., gi * per_group:(gi + 1) * per_group], [nh * dh, (nh + kvh) * dh], axis=-1)
        q = apply_partial_rope(q.reshape(b, s, nh, dh), cos, sin).reshape(b, s, kvh, gq, dh)
        k = apply_partial_rope(k.reshape(b, s, kvh, dh), cos, sin)
        v = v.reshape(b, s, kvh, dh)
        acc, m, l = banded_attention(to_residues(q, dilation), to_residues(k, dilation),
                                     to_residues(v, dilation), window // dilation)
        outs.append(from_residues(acc / l[..., None], dilation, b, s))
        lses.append(from_residues(m + jnp.log(l), dilation, b, s))
    weights = jax.nn.softmax(jnp.stack(lses, axis=0), axis=0)
    out = jnp.einsum("gbskh,gbskhd->bskhd", weights, jnp.stack(outs, axis=0))
    return out.reshape(b, s, nh * dh).astype(x.dtype) @ w_out


def setup_inputs(seed: int = 0) -> dict:
    key = jax.random.key(seed)
    keys = jax.random.split(key, 48)
    counter = [0]

    def nk():
        counter[0] += 1
        return keys[counter[0] - 1]

    def dense(shape, fan_in):
        return jax.random.normal(nk(), shape, jnp.float32) * fan_in ** -0.5

    def gains(shape):
        return 1.0 + 0.02 * jax.random.normal(nk(), shape, jnp.float32)

    def small(shape, scale):
        return scale * jax.random.normal(nk(), shape, jnp.float32)

    def dt_bias_init(nheads):
        dt = jnp.exp(jax.random.uniform(nk(), (nheads,), jnp.float32, math.log(1e-3), math.log(1e-1)))
        return dt + jnp.log(-jnp.expm1(-dt))

    def a_log_init(nheads):
        return jnp.log(jax.random.uniform(nk(), (nheads,), jnp.float32, 1.0, 16.0))

    inputs = {}
    inputs["x"] = jax.random.normal(nk(), (BATCH, SEQ, D_MODEL), jnp.float32)
    inputs["positions"] = (jax.random.randint(nk(), (BATCH, 1), 0, 4096, jnp.int32)
                           + jnp.arange(SEQ, dtype=jnp.int32)[None, :])

    def channel_mixer(i):
        inputs[f"l{i}_norms"] = gains((6, D_MODEL))
        inputs[f"l{i}_ffn_w_in"] = dense((2, D_MODEL, 2 * D_FF), D_MODEL)
        inputs[f"l{i}_ffn_w_out"] = dense((2, D_FF, D_MODEL), D_FF)

    channel_mixer(0)
    swa_cols = (SWA_HEADS + 2 * SWA_KV_HEADS) * SWA_HEAD_DIM
    inputs["a_w_in"] = dense((D_MODEL, swa_cols), D_MODEL)
    inputs["a_b_in"] = small((swa_cols,), 0.02)
    inputs["a_sinks"] = small((SWA_HEADS,), 1.0)
    inputs["a_w_out"] = dense((SWA_HEADS * SWA_HEAD_DIM, D_MODEL), SWA_HEADS * SWA_HEAD_DIM)

    channel_mixer(1)
    gdn_width = GDN_HEADS * GDN_HEAD_DIM
    inputs["b_w_in"] = dense((D_MODEL, 4 * gdn_width + 2 * GDN_HEADS), D_MODEL)
    inputs["b_conv_w"] = dense((GDN_CONV, 3 * gdn_width), GDN_CONV)
    inputs["b_A_log"] = a_log_init(GDN_HEADS)
    inputs["b_dt_bias"] = dt_bias_init(GDN_HEADS)
    inputs["b_norm"] = gains((GDN_HEAD_DIM,))
    inputs["b_w_out"] = dense((gdn_width, D_MODEL), gdn_width)

    channel_mixer(2)
    ssd_conv_ch = SSD_D_INNER + 2 * SSD_GROUPS * SSD_STATE
    inputs["c_w_in"] = dense((D_MODEL, SSD_D_INNER + ssd_conv_ch + SSD_HEADS), D_MODEL)
    inputs["c_conv_w"] = dense((SSD_CONV, ssd_conv_ch), SSD_CONV)
    inputs["c_conv_b"] = small((ssd_conv_ch,), 0.02)
    inputs["c_dt_bias"] = dt_bias_init(SSD_HEADS)
    inputs["c_A_log"] = a_log_init(SSD_HEADS)
    inputs["c_D"] = gains((SSD_HEADS,))
    inputs["c_norm"] = gains((SSD_D_INNER,))
    inputs["c_w_out"] = dense((SSD_D_INNER, D_MODEL), SSD_D_INNER)

    channel_mixer(3)
    dil_cols = len(DIL_PATTERN) * (DIL_HEADS + 2 * DIL_KV_HEADS) * DIL_HEAD_DIM
    inputs["d_w_in"] = dense((D_MODEL, dil_cols), D_MODEL)
    inputs["d_w_out"] = dense((DIL_HEADS * DIL_HEAD_DIM, D_MODEL), DIL_HEADS * DIL_HEAD_DIM)
    return inputs


def reference(x, positions,
              l0_norms, l0_ffn_w_in, l0_ffn_w_out, a_w_in, a_b_in, a_sinks, a_w_out,
              l1_norms, l1_ffn_w_in, l1_ffn_w_out, b_w_in, b_conv_w, b_A_log, b_dt_bias, b_norm, b_w_out,
              l2_norms, l2_ffn_w_in, l2_ffn_w_out, c_w_in, c_conv_w, c_conv_b, c_dt_bias, c_A_log, c_D, c_norm, c_w_out,
              l3_norms, l3_ffn_w_in, l3_ffn_w_out, d_w_in, d_w_out):
    cos, sin = rope_tables(positions, SWA_HEAD_DIM)
    mixers = (
        lambda h: swa_sink_mixer(h, cos, sin, a_w_in, a_b_in, a_sinks, a_w_out),
        lambda h: gated_deltanet_mixer(h, b_w_in, b_conv_w, b_A_log, b_dt_bias, b_norm, b_w_out),
        lambda h: mamba2_ssd_mixer(h, c_w_in, c_conv_w, c_conv_b, c_dt_bias, c_A_log, c_D, c_norm, c_w_out),
        lambda h: dilated_mixer(h, cos, sin, d_w_in, d_w_out),
    )
    layer_params = (
        (l0_norms, l0_ffn_w_in, l0_ffn_w_out),
        (l1_norms, l1_ffn_w_in, l1_ffn_w_out),
        (l2_norms, l2_ffn_w_in, l2_ffn_w_out),
        (l3_norms, l3_ffn_w_in, l3_ffn_w_out),
    )
    for i in range(DEPTH):
        norms, ffn_w_in, ffn_w_out = layer_params[i]
        mixer = mixers[i % N_MIXERS]
        x = x + 0.5 * rmsnorm(swiglu(rmsnorm(x, norms[0]), ffn_w_in[0], ffn_w_out[0]), norms[1])
        x = x + rmsnorm(mixer(rmsnorm(x, norms[2])), norms[3])
        x = x + 0.5 * rmsnorm(swiglu(rmsnorm(x, norms[4]), ffn_w_in[1], ffn_w_out[1]), norms[5])
    return x
```

```python
import functools
import math

import jax
import jax.numpy as jnp
from jax import lax
from jax.experimental import pallas as pl
from jax.experimental.pallas import tpu as pltpu

F32 = jnp.float32
BF16 = jnp.bfloat16

D_MODEL = 1024
D_FF = 2816
NORM_EPS = 1e-6
ROPE_THETA = 500000.0
ROPE_DIMS = 16
HEAD_DIM = 64
BLOCK = 128
NEG_INF = -1e30

SWA_HEADS, SWA_KV_HEADS, SWA_WINDOW = 16, 2, 128
GDN_HEADS, GDN_HEAD_DIM, GDN_CONV, GDN_CHUNK = 8, 128, 4, 64
SSD_D_INNER, SSD_HEAD_DIM, SSD_GROUPS, SSD_STATE, SSD_CONV, SSD_CHUNK = 2048, 64, 4, 128, 4, 128
SSD_HEADS = SSD_D_INNER // SSD_HEAD_DIM
DIL_PATTERN = ((128, 1), (512, 4), (2048, 16))
DIL_HEADS, DIL_KV_HEADS = 16, 4

LANES = 128
CARRY_ROWS = 8
VMEM_LIMIT = 56 * 1024 * 1024
HI = lax.Precision.HIGHEST

TM_FFN = 512
TM_PROJ = 512
FF_CHUNK = 1408
ATTN_BLOCKS = 2


def _params(sem):
    return pltpu.CompilerParams(dimension_semantics=sem, vmem_limit_bytes=VMEM_LIMIT)


def _resident(shape):
    nd = len(shape)
    return pl.BlockSpec(shape, lambda *_: (0,) * nd, pipeline_mode=pl.Buffered(1))


def _rmsnorm(x, gain):
    return x * lax.rsqrt(jnp.mean(x * x, axis=-1, keepdims=True) + NORM_EPS) * gain


def _silu(x):
    return x * jax.nn.sigmoid(x)


def _softplus(x):
    return jnp.maximum(x, 0.0) + jnp.log(1.0 + jnp.exp(-jnp.abs(x)))


def _dot(a, b):
    return jnp.dot(a, b, preferred_element_type=F32)


def _dot_nt(a, b, precision=None):
    return lax.dot_general(a, b, (((1,), (1,)), ((), ())), preferred_element_type=F32, precision=precision)


def _dot_tn(a, b):
    return lax.dot_general(a, b, (((0,), (0,)), ((), ())), preferred_element_type=F32)


def _eye(n, dtype=F32):
    return (lax.broadcasted_iota(jnp.int32, (n, n), 0) == lax.broadcasted_iota(jnp.int32, (n, n), 1)).astype(dtype)


def _tril(n, dtype=F32):
    return (lax.broadcasted_iota(jnp.int32, (n, n), 0) >= lax.broadcasted_iota(jnp.int32, (n, n), 1)).astype(dtype)


def _ffn_body(x_ref, g_ref, win_ref, wout_ref, o_ref, act_ref):
    x = x_ref[...]
    h = _rmsnorm(x, g_ref[0:1, :]).astype(BF16)
    for lo in range(0, D_FF, FF_CHUNK):
        gate = _dot(h, win_ref[:, lo:lo + FF_CHUNK])
        up = _dot(h, win_ref[:, D_FF + lo:D_FF + lo + FF_CHUNK])
        act_ref[:, lo:lo + FF_CHUNK] = (_silu(gate) * up).astype(BF16)
    y = _dot(act_ref[...], wout_ref[...])
    o_ref[...] = x + 0.5 * _rmsnorm(y, g_ref[1:2, :])


def _ffn(x, gains, w_in, w_out):
    t = x.shape[0]
    tm = min(TM_FFN, t)
    return pl.pallas_call(
        _ffn_body,
        grid=(t // tm,),
        in_specs=[pl.BlockSpec((tm, D_MODEL), lambda i: (i, 0)),
                  _resident((2, D_MODEL)),
                  _resident((D_MODEL, 2 * D_FF)),
                  _resident((D_FF, D_MODEL))],
        out_specs=pl.BlockSpec((tm, D_MODEL), lambda i: (i, 0)),
        out_shape=jax.ShapeDtypeStruct((t, D_MODEL), F32),
        scratch_shapes=[pltpu.VMEM((tm, D_FF), BF16)],
        compiler_params=_params(("parallel",)),
        name="ffn",
    )(x, gains, w_in.astype(BF16), w_out.astype(BF16))


def _outproj_body(y_ref, x_ref, g_ref, w_ref, o_ref):
    o_ref[...] = x_ref[...] + _rmsnorm(_dot(y_ref[...], w_ref[...]), g_ref[...])


def _outproj(y, x, gain, w_out):
    t, k = y.shape
    tm = min(TM_PROJ, t)
    return pl.pallas_call(
        _outproj_body,
        grid=(t // tm,),
        in_specs=[pl.BlockSpec((tm, k), lambda i: (i, 0)),
                  pl.BlockSpec((tm, D_MODEL), lambda i: (i, 0)),
                  _resident((1, D_MODEL)),
                  _resident((k, D_MODEL))],
        out_specs=pl.BlockSpec((tm, D_MODEL), lambda i: (i, 0)),
        out_shape=jax.ShapeDtypeStruct((t, D_MODEL), F32),
        compiler_params=_params(("parallel",)),
        name="outproj",
    )(y, x, gain.reshape(1, D_MODEL), w_out.astype(BF16))


def _rope_tables(pos_ref, freq_ref):
    ang = pos_ref[...] * freq_ref[...]
    d = lax.broadcasted_iota(jnp.int32, ang.shape, 1) % HEAD_DIM
    half = ROPE_DIMS // 2
    cos, sin = jnp.cos(ang), jnp.sin(ang)
    c = jnp.where(d < ROPE_DIMS, cos, 1.0)
    s_lo = jnp.where(d < half, -sin, 0.0)
    s_hi = jnp.where((d >= half) & (d < ROPE_DIMS), sin, 0.0)
    return c, s_lo, s_hi


def _rope(y, tables):
    c, s_lo, s_hi = tables
    half = ROPE_DIMS // 2
    return y * c + pltpu.roll(y, LANES - half, 1) * s_lo + pltpu.roll(y, half, 1) * s_hi


def _proj_attn_body(x_ref, g_ref, pos_ref, freq_ref, w_ref, b_ref, *out_refs, layout, q_scale):
    h = _rmsnorm(x_ref[...], g_ref[...]).astype(BF16)
    tables = _rope_tables(pos_ref, freq_ref)
    for (kind, start, width), o_ref in zip(layout, out_refs):
        for lo in range(0, width, LANES):
            y = _dot(h, w_ref[:, start + lo:start + lo + LANES]) + b_ref[:, start + lo:start + lo + LANES]
            if kind != "v":
                y = _rope(y, tables)
            if kind == "q":
                y = y * q_scale
            o_ref[:, lo:lo + LANES] = y.astype(o_ref.dtype)


def _proj_attn(x, gain, pos_b, freq, w_in, b_in, layout):
    t = x.shape[0]
    n = w_in.shape[1]
    tm = min(TM_PROJ, t)
    body = functools.partial(_proj_attn_body, layout=layout, q_scale=HEAD_DIM ** -0.5)
    return pl.pallas_call(
        body,
        grid=(t // tm,),
        in_specs=[pl.BlockSpec((tm, D_MODEL), lambda i: (i, 0)),
                  _resident((1, D_MODEL)),
                  pl.BlockSpec((tm, LANES), lambda i: (i, 0)),
                  _resident((1, LANES)),
                  _resident((D_MODEL, n)),
                  _resident((1, n))],
        out_specs=[pl.BlockSpec((tm, w), lambda i: (i, 0)) for _, _, w in layout],
        out_shape=[jax.ShapeDtypeStruct((t, w), BF16) for _, _, w in layout],
        compiler_params=_params(("parallel",)),
        name="proj_attn",
    )(x, gain.reshape(1, D_MODEL), pos_b, freq, w_in.astype(BF16), b_in.reshape(1, n).astype(F32))


def _proj_plain_body(x_ref, g_ref, w_ref, *out_refs, layout):
    h = _rmsnorm(x_ref[...], g_ref[...]).astype(BF16)
    for (start, width), o_ref in zip(layout, out_refs):
        chunk = 512 if width % 512 == 0 else LANES
        for lo in range(0, width, chunk):
            o_ref[:, lo:lo + chunk] = _dot(h, w_ref[:, start + lo:start + lo + chunk])


def _proj_plain(x, gain, w_in, layout):
    t = x.shape[0]
    n = w_in.shape[1]
    tm = min(TM_PROJ, t)
    return pl.pallas_call(
        functools.partial(_proj_plain_body, layout=layout),
        grid=(t // tm,),
        in_specs=[pl.BlockSpec((tm, D_MODEL), lambda i: (i, 0)),
                  _resident((1, D_MODEL)),
                  _resident((D_MODEL, n))],
        out_specs=[pl.BlockSpec((tm, w), lambda i: (i, 0)) for _, w in layout],
        out_shape=[jax.ShapeDtypeStruct((t, w), F32) for _, w in layout],
        compiler_params=_params(("parallel",)),
        name="proj_plain",
    )(x, gain.reshape(1, D_MODEL), w_in.astype(BF16))


def _attn_body(*refs, n_heads, kv_heads, n_back, n_blocks, mode):
    if mode == "sink":
        sink_ref, q_ref, kc_ref, kp_ref, vc_ref, vp_ref, o_ref = refs
    elif mode == "first":
        q_ref, kc_ref, kp_ref, vc_ref, vp_ref, acc_out, st_out = refs
    elif mode == "mid":
        q_ref, kc_ref, kp_ref, vc_ref, vp_ref, acc_in, st_in, acc_out, st_out = refs
    else:
        q_ref, kc_ref, kp_ref, vc_ref, vp_ref, acc_in, st_in, o_ref = refs
    gq = n_heads // kv_heads
    step = pl.program_id(2)
    k_all = jnp.concatenate([kp_ref[0], kc_ref[0]], axis=0)
    v_all = jnp.concatenate([vp_ref[0], vc_ref[0]], axis=0)
    rows = lax.broadcasted_iota(jnp.int32, (BLOCK, 2 * BLOCK), 0) + BLOCK
    cols = lax.broadcasted_iota(jnp.int32, (BLOCK, 2 * BLOCK), 1)
    band = (rows - cols >= 0) & (rows - cols <= n_back)
    lane = lax.broadcasted_iota(jnp.int32, (BLOCK, LANES), 1)
    for j in range(n_blocks):
        r0 = j * BLOCK
        has_prev = (step * n_blocks + j) > 0
        visible = band & ((cols >= BLOCK) | has_prev)
        stats = jnp.zeros((BLOCK, LANES), F32)
        if mode in ("mid", "last"):
            stats_prev = st_in[0, r0:r0 + BLOCK, :]
        for kv in range(kv_heads):
            kw = k_all[r0:r0 + 2 * BLOCK, kv * HEAD_DIM:(kv + 1) * HEAD_DIM]
            vw = v_all[r0:r0 + 2 * BLOCK, kv * HEAD_DIM:(kv + 1) * HEAD_DIM]
            for g in range(gq):
                h = kv * gq + g
                c0 = h * HEAD_DIM
                s = _dot_nt(q_ref[0, r0:r0 + BLOCK, c0:c0 + HEAD_DIM], kw)
                s = jnp.where(visible, s, NEG_INF)
                m = jnp.max(s, axis=-1, keepdims=True)
                p = jnp.exp(s - m)
                l = jnp.sum(p, axis=-1, keepdims=True)
                acc = _dot(p.astype(BF16), vw)
                if mode == "sink":
                    sink = sink_ref[h]
                    m_all = jnp.maximum(m, sink)
                    rescale = jnp.exp(m - m_all)
                    out = acc * (rescale / (l * rescale + jnp.exp(sink - m_all)))
                    o_ref[0, r0:r0 + BLOCK, c0:c0 + HEAD_DIM] = out.astype(o_ref.dtype)
                    continue
                if mode in ("mid", "last"):
                    m_prev = stats_prev[:, h:h + 1]
                    l_prev = stats_prev[:, n_heads + h:n_heads + h + 1]
                    m_new = jnp.maximum(m_prev, m)
                    a_prev, a_cur = jnp.exp(m_prev - m_new), jnp.exp(m - m_new)
                    acc = acc_in[0, r0:r0 + BLOCK, c0:c0 + HEAD_DIM] * a_prev + acc * a_cur
                    l = l_prev * a_prev + l * a_cur
                    m = m_new
                if mode == "last":
                    o_ref[0, r0:r0 + BLOCK, c0:c0 + HEAD_DIM] = (acc / l).astype(o_ref.dtype)
                else:
                    acc_out[0, r0:r0 + BLOCK, c0:c0 + HEAD_DIM] = acc
                    stats = jnp.where(lane == h, m, stats)
                    stats = jnp.where(lane == n_heads + h, l, stats)
        if mode in ("first", "mid"):
            st_out[0, r0:r0 + BLOCK, :] = stats


def _attention(q, k, v, *, dilation, n_heads, kv_heads, n_back, mode, sinks=None, carry=None):
    b, s, qc = q.shape
    kc = k.shape[-1]
    d = dilation
    length = s // d
    assert length * d == s and length % BLOCK == 0
    n_blocks = ATTN_BLOCKS if length % (ATTN_BLOCKS * BLOCK) == 0 else 1
    tq = n_blocks * BLOCK

    def view(a):
        return a.reshape(b, length, d * a.shape[-1])

    def cur(width):
        return pl.BlockSpec((1, tq, width), lambda bi, r, i: (bi, i, r))

    def prev(width):
        return pl.BlockSpec((1, BLOCK, width), lambda bi, r, i: (bi, jnp.maximum(i * n_blocks - 1, 0), r))

    in_specs = [cur(qc), cur(kc), prev(kc), cur(kc), prev(kc)]
    args = [view(q), view(k), view(k), view(v), view(v)]
    if mode == "sink":
        in_specs = [pl.BlockSpec(memory_space=pltpu.SMEM)] + in_specs
        args = [sinks.astype(F32)] + args
    if mode in ("mid", "last"):
        in_specs += [cur(qc), cur(LANES)]
        args += [view(carry[0]), view(carry[1])]
    if mode in ("sink", "last"):
        out_specs = cur(qc)
        out_shape = jax.ShapeDtypeStruct((b, length, d * qc), BF16)
    else:
        out_specs = [cur(qc), cur(LANES)]
        out_shape = [jax.ShapeDtypeStruct((b, length, d * qc), F32),
                     jax.ShapeDtypeStruct((b, length, d * LANES), F32)]
    body = functools.partial(_attn_body, n_heads=n_heads, kv_heads=kv_heads, n_back=n_back,
                             n_blocks=n_blocks, mode=mode)
    out = pl.pallas_call(
        body,
        grid=(b, d, length // tq),
        in_specs=in_specs,
        out_specs=out_specs,
        out_shape=out_shape,
        compiler_params=_params(("parallel", "parallel", "parallel")),
        name="attn_" + mode,
    )(*args)
    if mode in ("sink", "last"):
        return out.reshape(b, s, qc)
    return out[0].reshape(b, s, qc), out[1].reshape(b, s, LANES)


def _conv_taps(xbuf_ref, w_ref, c0, rows, width):
    acc = None
    for i in range(width):
        start = CARRY_ROWS - (width - 1) + i
        term = xbuf_ref[start:start + rows, c0:c0 + LANES] * w_ref[i:i + 1, c0:c0 + LANES]
        acc = term if acc is None else acc + term
    return acc


def _l2norm(x):
    return x * lax.rsqrt(jnp.sum(x * x, axis=-1, keepdims=True) + NORM_EPS)


def _unit_lower_inverse(strict_lower):
    n = strict_lower.shape[0]
    neg = -strict_lower
    inv = _eye(n) + neg
    power = neg
    for _ in range(int(math.log2(n)) - 1):
        power = jnp.dot(power, power, preferred_element_type=F32, precision=HI)
        inv = inv + jnp.dot(inv, power, preferred_element_type=F32, precision=HI)
    return inv


def _gdn_body(qkv_ref, z_ref, small_ref, convw_ref, alog_ref, dtb_ref, normw_ref, o_ref, xbuf_ref, state_ref):
    c = GDN_CHUNK
    nh, dk = GDN_HEADS, GDN_HEAD_DIM
    width = nh * dk

    @pl.when(pl.program_id(1) == 0)
    def _():
        xbuf_ref[0:CARRY_ROWS, :] = jnp.zeros((CARRY_ROWS, 3 * width), F32)
        state_ref[...] = jnp.zeros_like(state_ref)

    xbuf_ref[CARRY_ROWS:CARRY_ROWS + c, :] = qkv_ref[0]

    small = small_ref[0]
    beta_all = jax.nn.sigmoid(small)
    g_all = -jnp.exp(alog_ref[...]) * _softplus(small + dtb_ref[...])
    gcum_all = jnp.dot(_tril(c), g_all, preferred_element_type=F32, precision=HI)
    lane = lax.broadcasted_iota(jnp.int32, (c, LANES), 1)
    ri = lax.broadcasted_iota(jnp.int32, (c, c), 0)
    ci = lax.broadcasted_iota(jnp.int32, (c, c), 1)
    causal, strict = ri >= ci, ri > ci

    for h in range(nh):
        q = _l2norm(_silu(_conv_taps(xbuf_ref, convw_ref, h * dk, c, GDN_CONV))) * (dk ** -0.5)
        k = _l2norm(_silu(_conv_taps(xbuf_ref, convw_ref, width + h * dk, c, GDN_CONV)))
        v = _silu(_conv_taps(xbuf_ref, convw_ref, 2 * width + h * dk, c, GDN_CONV))
        beta = beta_all[:, h:h + 1]
        gcum = gcum_all[:, nh + h:nh + h + 1]
        gcum_row = _dot_nt((lane == nh + h).astype(F32), gcum_all, precision=HI)
        decay = jnp.exp(jnp.where(causal, gcum - gcum_row, -jnp.inf))
        k_b = k.astype(BF16)
        lower = jnp.where(strict, _dot_nt(k_b, k_b) * decay * beta, 0.0)
        inv = _unit_lower_inverse(lower)
        rhs = jnp.concatenate([v * beta, k * (beta * jnp.exp(gcum))], axis=-1)
        sol = jnp.dot(inv, rhs, preferred_element_type=F32, precision=HI)
        u, w = sol[:, :dk], sol[:, dk:]
        state = state_ref[h]
        state_b = state.astype(BF16)
        v_new = u - _dot(w.astype(BF16), state_b)
        v_new_b = v_new.astype(BF16)
        q_b = q.astype(BF16)
        intra = jnp.where(causal, _dot_nt(q_b, k_b) * decay, 0.0)
        out = _dot((q * jnp.exp(gcum)).astype(BF16), state_b) + _dot(intra.astype(BF16), v_new_b)
        g_last = gcum[c - 1:c, :]
        state_ref[h] = state * jnp.exp(g_last) + _dot_tn((k * jnp.exp(g_last - gcum)).astype(BF16), v_new_b)
        gate = _silu(z_ref[0, :, h * dk:(h + 1) * dk])
        o_ref[0, :, h * dk:(h + 1) * dk] = (_rmsnorm(out, normw_ref[...]) * gate).astype(o_ref.dtype)

    xbuf_ref[0:CARRY_ROWS, :] = xbuf_ref[c:c + CARRY_ROWS, :]


def _gdn_core(qkv, z, small, conv_w, a_log, dt_bias, norm_w):
    b, s, _ = qkv.shape
    c = GDN_CHUNK
    nh, dk = GDN_HEADS, GDN_HEAD_DIM
    width = nh * dk
    pad = lambda a: jnp.zeros((1, LANES), F32).at[0, nh:2 * nh].set(a.astype(F32))
    return pl.pallas_call(
        _gdn_body,
        grid=(b, s // c),
        in_specs=[pl.BlockSpec((1, c, 3 * width), lambda bi, i: (bi, i, 0)),
                  pl.BlockSpec((1, c, width), lambda bi, i: (bi, i, 0)),
                  pl.BlockSpec((1, c, LANES), lambda bi, i: (bi, i, 0)),
                  _resident((GDN_CONV, 3 * width)),
                  _resident((1, LANES)),
                  _resident((1, LANES)),
                  _resident((1, dk))],
        out_specs=pl.BlockSpec((1, c, width), lambda bi, i: (bi, i, 0)),
        out_shape=jax.ShapeDtypeStruct((b, s, width), BF16),
        scratch_shapes=[pltpu.VMEM((CARRY_ROWS + c, 3 * width), F32),
                        pltpu.VMEM((nh, dk, dk), F32)],
        compiler_params=_params(("parallel", "arbitrary")),
        name="gdn",
    )(qkv, z, small, conv_w.astype(F32), pad(a_log), pad(dt_bias), norm_w.reshape(1, dk).astype(F32))


def _ssd_body(z_ref, xbc_ref, dt_ref, convw_ref, convb_ref, dtb_ref, alog_ref, dskip_ref, normw_ref,
              o_ref, xbuf_ref, state_ref):
    c = SSD_CHUNK
    di, ng, p = SSD_D_INNER, SSD_GROUPS, SSD_HEAD_DIM
    heads_per_group = SSD_HEADS // ng
    pairs_per_group = heads_per_group // 2
    group_width = di // ng

    @pl.when(pl.program_id(1) == 0)
    def _():
        xbuf_ref[0:CARRY_ROWS, :] = jnp.zeros((CARRY_ROWS, xbuf_ref.shape[1]), F32)
        state_ref[...] = jnp.zeros_like(state_ref)

    xbuf_ref[CARRY_ROWS:CARRY_ROWS + c, :] = xbc_ref[0]

    def conv(c0):
        return _silu(_conv_taps(xbuf_ref, convw_ref, c0, c, SSD_CONV) + convb_ref[:, c0:c0 + LANES])

    dt_all = _softplus(dt_ref[0] + dtb_ref[...])
    adt = dt_all * -jnp.exp(alog_ref[...])
    acum_all = jnp.dot(_tril(c), adt, preferred_element_type=F32, precision=HI)
    acum_rows = _dot_nt(_eye(LANES), acum_all, precision=HI)
    ri = lax.broadcasted_iota(jnp.int32, (c, c), 0)
    ci = lax.broadcasted_iota(jnp.int32, (c, c), 1)
    causal = ri >= ci
    low_half = lax.broadcasted_iota(jnp.int32, (c, LANES), 1) < p
    eye_b = _eye(LANES, BF16)

    for g in range(ng):
        bm = conv(di + g * SSD_STATE).astype(BF16)
        cm = conv(di + ng * SSD_STATE + g * SSD_STATE).astype(BF16)
        cb = _dot_nt(cm, bm)
        bm_t = _dot_nt(eye_b, bm).astype(BF16)
        ys = []
        sumsq = jnp.zeros((c, 1), F32)
        for pp in range(pairs_per_group):
            pair = g * pairs_per_group + pp
            ha, hb = 2 * pair, 2 * pair + 1
            c0 = pair * LANES
            x = conv(c0)

            def per_head(fn):
                return jnp.where(low_half, fn(ha), fn(hb))

            def seg(hd):
                diff = acum_all[:, hd:hd + 1] - acum_rows[hd:hd + 1, :]
                return (cb * jnp.exp(jnp.where(causal, diff, -jnp.inf))).astype(BF16)

            x_dt = x * per_head(lambda hd: dt_all[:, hd:hd + 1])
            x_dt_b = x_dt.astype(BF16)
            y_diag = jnp.where(low_half, _dot(seg(ha), x_dt_b), _dot(seg(hb), x_dt_b))
            state = state_ref[pair]
            y_off = _dot(cm, state.astype(BF16)) * per_head(lambda hd: jnp.exp(acum_all[:, hd:hd + 1]))
            a_last = per_head(lambda hd: acum_all[c - 1:c, hd:hd + 1])
            to_end = jnp.exp(a_last - per_head(lambda hd: acum_all[:, hd:hd + 1]))
            state_ref[pair] = state * jnp.exp(a_last[0:1, :]) + _dot(bm_t, (x_dt * to_end).astype(BF16))
            y = y_diag + y_off + x * dskip_ref[:, c0:c0 + LANES]
            y = y * _silu(z_ref[0, :, c0:c0 + LANES])
            ys.append(y)
            sumsq = sumsq + jnp.sum(y * y, axis=-1, keepdims=True)
        scale = lax.rsqrt(sumsq * (1.0 / group_width) + NORM_EPS)
        for pp, y in enumerate(ys):
            c0 = (g * pairs_per_group + pp) * LANES
            o_ref[0, :, c0:c0 + LANES] = (y * scale * normw_ref[:, c0:c0 + LANES]).astype(o_ref.dtype)

    xbuf_ref[0:CARRY_ROWS, :] = xbuf_ref[c:c + CARRY_ROWS, :]


def _ssd_core(z, xbc, dt_raw, conv_w, conv_b, dt_bias, a_log, d_skip, norm_w):
    b, s, conv_ch = xbc.shape
    c = SSD_CHUNK
    di = SSD_D_INNER
    pad = lambda a: jnp.zeros((1, LANES), F32).at[0, :SSD_HEADS].set(a.astype(F32))
    return pl.pallas_call(
        _ssd_body,
        grid=(b, s // c),
        in_specs=[pl.BlockSpec((1, c, di), lambda bi, i: (bi, i, 0)),
                  pl.BlockSpec((1, c, conv_ch), lambda bi, i: (bi, i, 0)),
                  pl.BlockSpec((1, c, LANES), lambda bi, i: (bi, i, 0)),
                  _resident((SSD_CONV, conv_ch)),
                  _resident((1, conv_ch)),
                  _resident((1, LANES)),
                  _resident((1, LANES)),
                  _resident((1, di)),
                  _resident((1, di))],
        out_specs=pl.BlockSpec((1, c, di), lambda bi, i: (bi, i, 0)),
        out_shape=jax.ShapeDtypeStruct((b, s, di), BF16),
        scratch_shapes=[pltpu.VMEM((CARRY_ROWS + c, conv_ch), F32),
                        pltpu.VMEM((SSD_HEADS // 2, SSD_STATE, LANES), F32)],
        compiler_params=_params(("parallel", "arbitrary")),
        name="ssd",
    )(z, xbc, dt_raw, conv_w.astype(F32), conv_b.reshape(1, conv_ch).astype(F32), pad(dt_bias), pad(a_log),
      jnp.repeat(d_skip.astype(F32), SSD_HEAD_DIM).reshape(1, di), norm_w.reshape(1, di).astype(F32))


def _pad_cols(w, n):
    return jnp.pad(w, ((0, 0), (0, n - w.shape[1])))


def _swa_mixer(x, b, s, gain, pos_b, freq, w_in, b_in, sinks, w_out, gain_post):
    nq, nkv = SWA_HEADS * HEAD_DIM, SWA_KV_HEADS * HEAD_DIM
    layout = (("q", 0, nq), ("k", nq, nkv), ("v", nq + nkv, nkv))
    q, k, v = _proj_attn(x, gain, pos_b, freq, w_in, b_in, layout)
    o = _attention(q.reshape(b, s, nq), k.reshape(b, s, nkv), v.reshape(b, s, nkv), dilation=1,
                   n_heads=SWA_HEADS, kv_heads=SWA_KV_HEADS, n_back=SWA_WINDOW - 1, mode="sink", sinks=sinks)
    return _outproj(o.reshape(b * s, nq), x, gain_post, w_out)


def _dilated_mixer(x, b, s, gain, pos_b, freq, w_in, w_out, gain_post):
    nq, nkv = DIL_HEADS * HEAD_DIM, DIL_KV_HEADS * HEAD_DIM
    per_group = nq + 2 * nkv
    layout = []
    for gi in range(len(DIL_PATTERN)):
        base = gi * per_group
        layout += [("q", base, nq), ("k", base + nq, nkv), ("v", base + nq + nkv, nkv)]
    outs = _proj_attn(x, gain, pos_b, freq, w_in, jnp.zeros((w_in.shape[1],), F32), tuple(layout))
    carry = None
    for gi, (window, dilation) in enumerate(DIL_PATTERN):
        q, k, v = outs[3 * gi:3 * gi + 3]
        mode = "first" if gi == 0 else ("last" if gi == len(DIL_PATTERN) - 1 else "mid")
        res = _attention(q.reshape(b, s, nq), k.reshape(b, s, nkv), v.reshape(b, s, nkv), dilation=dilation,
                         n_heads=DIL_HEADS, kv_heads=DIL_KV_HEADS, n_back=window // dilation, mode=mode, carry=carry)
        carry = res
    return _outproj(carry.reshape(b * s, nq), x, gain_post, w_out)


def _gdn_mixer(x, b, s, gain, w_in, conv_w, a_log, dt_bias, norm_w, w_out, gain_post):
    width = GDN_HEADS * GDN_HEAD_DIM
    w = jnp.concatenate([w_in[:, :4 * width], _pad_cols(w_in[:, 4 * width:], LANES)], axis=1)
    layout = ((0, 3 * width), (3 * width, width), (4 * width, LANES))
    qkv, z, small = _proj_plain(x, gain, w, layout)
    o = _gdn_core(qkv.reshape(b, s, 3 * width), z.reshape(b, s, width), small.reshape(b, s, LANES),
                  conv_w, a_log, dt_bias, norm_w)
    return _outproj(o.reshape(b * s, width), x, gain_post, w_out)


def _ssd_mixer(x, b, s, gain, w_in, conv_w, conv_b, dt_bias, a_log, d_skip, norm_w, w_out, gain_post):
    di = SSD_D_INNER
    conv_ch = di + 2 * SSD_GROUPS * SSD_STATE
    w = jnp.concatenate([w_in[:, :di + conv_ch], _pad_cols(w_in[:, di + conv_ch:], LANES)], axis=1)
    layout = ((0, di), (di, conv_ch), (di + conv_ch, LANES))
    z, xbc, dt_raw = _proj_plain(x, gain, w, layout)
    o = _ssd_core(z.reshape(b, s, di), xbc.reshape(b, s, conv_ch), dt_raw.reshape(b, s, LANES),
                  conv_w, conv_b, dt_bias, a_log, d_skip, norm_w)
    return _outproj(o.reshape(b * s, di), x, gain_post, w_out)


def kernel(x, positions, l0_norms, l0_ffn_w_in, l0_ffn_w_out, a_w_in, a_b_in, a_sinks, a_w_out, l1_norms, l1_ffn_w_in, l1_ffn_w_out, b_w_in, b_conv_w, b_A_log, b_dt_bias, b_norm, b_w_out, l2_norms, l2_ffn_w_in, l2_ffn_w_out, c_w_in, c_conv_w, c_conv_b, c_dt_bias, c_A_log, c_D, c_norm, c_w_out, l3_norms, l3_ffn_w_in, l3_ffn_w_out, d_w_in, d_w_out):
    b, s, d = x.shape
    t = b * s
    pos_b = jnp.broadcast_to(positions.astype(F32).reshape(t, 1), (t, LANES))
    inv_freq = ROPE_THETA ** (-jnp.arange(0, ROPE_DIMS, 2, dtype=F32) / ROPE_DIMS)
    freq = jnp.tile(inv_freq, LANES // inv_freq.shape[0]).reshape(1, LANES)

    mixers = (
        lambda h, n: _swa_mixer(h, b, s, n[2], pos_b, freq, a_w_in, a_b_in, a_sinks, a_w_out, n[3]),
        lambda h, n: _gdn_mixer(h, b, s, n[2], b_w_in, b_conv_w, b_A_log, b_dt_bias, b_norm, b_w_out, n[3]),
        lambda h, n: _ssd_mixer(h, b, s, n[2], c_w_in, c_conv_w, c_conv_b, c_dt_bias, c_A_log, c_D, c_norm,
                                c_w_out, n[3]),
        lambda h, n: _dilated_mixer(h, b, s, n[2], pos_b, freq, d_w_in, d_w_out, n[3]),
    )
    layers = ((l0_norms, l0_ffn_w_in, l0_ffn_w_out), (l1_norms, l1_ffn_w_in, l1_ffn_w_out),
              (l2_norms, l2_ffn_w_in, l2_ffn_w_out), (l3_norms, l3_ffn_w_in, l3_ffn_w_out))
    h = x.reshape(t, d)
    for i, (norms, ffn_w_in, ffn_w_out) in enumerate(layers):
        norms = norms.astype(F32)
        h = _ffn(h, norms[0:2], ffn_w_in[0], ffn_w_out[0])
        h = mixers[i % len(mixers)](h, norms)
        h = _ffn(h, norms[4:6], ffn_w_in[1], ffn_w_out[1])
    return h.reshape(b, s, d)
```

```python
import functools
import math

import jax
import jax.numpy as jnp
from jax import lax
from jax.experimental import pallas as pl
from jax.experimental.pallas import tpu as pltpu

F32 = jnp.float32
BF16 = jnp.bfloat16

D_MODEL = 1024
D_FF = 2816
NORM_EPS = 1e-6
ROPE_THETA = 500000.0
ROPE_DIMS = 16
HEAD_DIM = 64
BLOCK = 128
NEG_INF = -1e30

SWA_HEADS, SWA_KV_HEADS, SWA_WINDOW = 16, 2, 128
GDN_HEADS, GDN_HEAD_DIM, GDN_CONV, GDN_CHUNK = 8, 128, 4, 64
SSD_D_INNER, SSD_HEAD_DIM, SSD_GROUPS, SSD_STATE, SSD_CONV, SSD_CHUNK = 2048, 64, 4, 128, 4, 128
SSD_HEADS = SSD_D_INNER // SSD_HEAD_DIM
DIL_PATTERN = ((128, 1), (512, 4), (2048, 16))
DIL_HEADS, DIL_KV_HEADS = 16, 4

LANES = 128
CARRY_ROWS = 8
VMEM_LIMIT = 56 * 1024 * 1024
HI = lax.Precision.HIGHEST

TM_FFN = 512
TM_PROJ = 512
FF_CHUNK = 1408
ATTN_BLOCKS = 2


def _params(sem):
    return pltpu.CompilerParams(dimension_semantics=sem, vmem_limit_bytes=VMEM_LIMIT)


def _resident(shape):
    nd = len(shape)
    return pl.BlockSpec(shape, lambda *_: (0,) * nd, pipeline_mode=pl.Buffered(1))


def _rmsnorm(x, gain):
    return x * lax.rsqrt(jnp.mean(x * x, axis=-1, keepdims=True) + NORM_EPS) * gain


def _silu(x):
    return x * jax.nn.sigmoid(x)


def _softplus(x):
    return jnp.maximum(x, 0.0) + jnp.log(1.0 + jnp.exp(-jnp.abs(x)))


def _dot(a, b):
    return jnp.dot(a, b, preferred_element_type=F32)


def _dot_nt(a, b, precision=None):
    return lax.dot_general(a, b, (((1,), (1,)), ((), ())), preferred_element_type=F32, precision=precision)


def _dot_tn(a, b):
    return lax.dot_general(a, b, (((0,), (0,)), ((), ())), preferred_element_type=F32)


def _eye(n, dtype=F32):
    return (lax.broadcasted_iota(jnp.int32, (n, n), 0) == lax.broadcasted_iota(jnp.int32, (n, n), 1)).astype(dtype)


def _tril(n, dtype=F32):
    return (lax.broadcasted_iota(jnp.int32, (n, n), 0) >= lax.broadcasted_iota(jnp.int32, (n, n), 1)).astype(dtype)


def _ffn_body(x_ref, g_ref, win_ref, wout_ref, o_ref, act_ref):
    x = x_ref[...]
    h = _rmsnorm(x, g_ref[0:1, :]).astype(BF16)
    for lo in range(0, D_FF, FF_CHUNK):
        gate = _dot(h, win_ref[:, lo:lo + FF_CHUNK])
        up = _dot(h, win_ref[:, D_FF + lo:D_FF + lo + FF_CHUNK])
        act_ref[:, lo:lo + FF_CHUNK] = (_silu(gate) * up).astype(BF16)
    y = _dot(act_ref[...], wout_ref[...])
    o_ref[...] = x + 0.5 * _rmsnorm(y, g_ref[1:2, :])


def _ffn(x, gains, w_in, w_out):
    t = x.shape[0]
    tm = min(TM_FFN, t)
    return pl.pallas_call(
        _ffn_body,
        grid=(t // tm,),
        in_specs=[pl.BlockSpec((tm, D_MODEL), lambda i: (i, 0)),
                  _resident((2, D_MODEL)),
                  _resident((D_MODEL, 2 * D_FF)),
                  _resident((D_FF, D_MODEL))],
        out_specs=pl.BlockSpec((tm, D_MODEL), lambda i: (i, 0)),
        out_shape=jax.ShapeDtypeStruct((t, D_MODEL), F32),
        scratch_shapes=[pltpu.VMEM((tm, D_FF), BF16)],
        compiler_params=_params(("parallel",)),
        name="ffn",
    )(x, gains, w_in.astype(BF16), w_out.astype(BF16))


def _outproj_body(y_ref, x_ref, g_ref, w_ref, o_ref):
    o_ref[...] = x_ref[...] + _rmsnorm(_dot(y_ref[...], w_ref[...]), g_ref[...])


def _outproj(y, x, gain, w_out):
    t, k = y.shape
    tm = min(TM_PROJ, t)
    return pl.pallas_call(
        _outproj_body,
        grid=(t // tm,),
        in_specs=[pl.BlockSpec((tm, k), lambda i: (i, 0)),
                  pl.BlockSpec((tm, D_MODEL), lambda i: (i, 0)),
                  _resident((1, D_MODEL)),
                  _resident((k, D_MODEL))],
        out_specs=pl.BlockSpec((tm, D_MODEL), lambda i: (i, 0)),
        out_shape=jax.ShapeDtypeStruct((t, D_MODEL), F32),
        compiler_params=_params(("parallel",)),
        name="outproj",
    )(y, x, gain.reshape(1, D_MODEL), w_out.astype(BF16))


def _rope_tables(pos_ref, freq_ref):
    ang = pos_ref[...] * freq_ref[...]
    d = lax.broadcasted_iota(jnp.int32, ang.shape, 1) % HEAD_DIM
    half = ROPE_DIMS // 2
    cos, sin = jnp.cos(ang), jnp.sin(ang)
    c = jnp.where(d < ROPE_DIMS, cos, 1.0)
    s_lo = jnp.where(d < half, -sin, 0.0)
    s_hi = jnp.where((d >= half) & (d < ROPE_DIMS), sin, 0.0)
    return c, s_lo, s_hi


def _rope(y, tables):
    c, s_lo, s_hi = tables
    half = ROPE_DIMS // 2
    return y * c + pltpu.roll(y, LANES - half, 1) * s_lo + pltpu.roll(y, half, 1) * s_hi


def _proj_attn_body(x_ref, g_ref, pos_ref, freq_ref, w_ref, b_ref, *out_refs, layout, q_scale):
    h = _rmsnorm(x_ref[...], g_ref[...]).astype(BF16)
    tables = _rope_tables(pos_ref, freq_ref)
    for (kind, start, width), o_ref in zip(layout, out_refs):
        for lo in range(0, width, LANES):
            y = _dot(h, w_ref[:, start + lo:start + lo + LANES]) + b_ref[:, start + lo:start + lo + LANES]
            if kind != "v":
                y = _rope(y, tables)
            if kind == "q":
                y = y * q_scale
            o_ref[:, lo:lo + LANES] = y.astype(o_ref.dtype)


def _proj_attn(x, gain, pos_b, freq, w_in, b_in, layout):
    t = x.shape[0]
    n = w_in.shape[1]
    tm = min(TM_PROJ, t)
    body = functools.partial(_proj_attn_body, layout=layout, q_scale=HEAD_DIM ** -0.5)
    return pl.pallas_call(
        body,
        grid=(t // tm,),
        in_specs=[pl.BlockSpec((tm, D_MODEL), lambda i: (i, 0)),
                  _resident((1, D_MODEL)),
                  pl.BlockSpec((tm, LANES), lambda i: (i, 0)),
                  _resident((1, LANES)),
                  _resident((D_MODEL, n)),
                  _resident((1, n))],
        out_specs=[pl.BlockSpec((tm, w), lambda i: (i, 0)) for _, _, w in layout],
        out_shape=[jax.ShapeDtypeStruct((t, w), BF16) for _, _, w in layout],
        compiler_params=_params(("parallel",)),
        name="proj_attn",
    )(x, gain.reshape(1, D_MODEL), pos_b, freq, w_in.astype(BF16), b_in.reshape(1, n).astype(F32))


def _proj_plain_body(x_ref, g_ref, w_ref, *out_refs, layout):
    h = _rmsnorm(x_ref[...], g_ref[...]).astype(BF16)
    for (start, width), o_ref in zip(layout, out_refs):
        chunk = 512 if width % 512 == 0 else LANES
        for lo in range(0, width, chunk):
            o_ref[:, lo:lo + chunk] = _dot(h, w_ref[:, start + lo:start + lo + chunk])


def _proj_plain(x, gain, w_in, layout):
    t = x.shape[0]
    n = w_in.shape[1]
    tm = min(TM_PROJ, t)
    return pl.pallas_call(
        functools.partial(_proj_plain_body, layout=layout),
        grid=(t // tm,),
        in_specs=[pl.BlockSpec((tm, D_MODEL), lambda i: (i, 0)),
                  _resident((1, D_MODEL)),
                  _resident((D_MODEL, n))],
        out_specs=[pl.BlockSpec((tm, w), lambda i: (i, 0)) for _, w in layout],
        out_shape=[jax.ShapeDtypeStruct((t, w), F32) for _, w in layout],
        compiler_params=_params(("parallel",)),
        name="proj_plain",
    )(x, gain.reshape(1, D_MODEL), w_in.astype(BF16))


def _attn_body(*refs, n_heads, kv_heads, n_back, n_blocks, with_sink):
    if with_sink:
        sink_ref, q_ref, kc_ref, kp_ref, vc_ref, vp_ref, o_ref = refs
    else:
        q_ref, kc_ref, kp_ref, vc_ref, vp_ref, o_ref, lse_ref = refs
    gq = n_heads // kv_heads
    step = pl.program_id(2)
    k_all = jnp.concatenate([kp_ref[0], kc_ref[0]], axis=0)
    v_all = jnp.concatenate([vp_ref[0], vc_ref[0]], axis=0)
    rows = lax.broadcasted_iota(jnp.int32, (BLOCK, 2 * BLOCK), 0) + BLOCK
    cols = lax.broadcasted_iota(jnp.int32, (BLOCK, 2 * BLOCK), 1)
    band = (rows - cols >= 0) & (rows - cols <= n_back)
    lane = lax.broadcasted_iota(jnp.int32, (BLOCK, LANES), 1)
    for j in range(n_blocks):
        r0 = j * BLOCK
        has_prev = (step * n_blocks + j) > 0
        visible = band & ((cols >= BLOCK) | has_prev)
        lse_tile = jnp.zeros((BLOCK, LANES), F32)
        for kv in range(kv_heads):
            kw = k_all[r0:r0 + 2 * BLOCK, kv * HEAD_DIM:(kv + 1) * HEAD_DIM]
            vw = v_all[r0:r0 + 2 * BLOCK, kv * HEAD_DIM:(kv + 1) * HEAD_DIM]
            heads = [kv * gq + g for g in range(gq)]
            scores = [jnp.where(visible, _dot_nt(q_ref[0, r0:r0 + BLOCK, h * HEAD_DIM:(h + 1) * HEAD_DIM], kw), NEG_INF)
                      for h in heads]
            maxes = [jnp.max(s, axis=-1, keepdims=True) for s in scores]
            probs = [jnp.exp(s - m) for s, m in zip(scores, maxes)]
            sums = [jnp.sum(p, axis=-1, keepdims=True) for p in probs]
            accs = [_dot(p.astype(BF16), vw) for p in probs]
            for h, m, l, acc in zip(heads, maxes, sums, accs):
                if with_sink:
                    sink = sink_ref[h]
                    m_all = jnp.maximum(m, sink)
                    rescale = jnp.exp(m - m_all)
                    out = acc * (rescale / (l * rescale + jnp.exp(sink - m_all)))
                else:
                    out = acc / l
                    lse_tile = jnp.where(lane == h, m + jnp.log(l), lse_tile)
                o_ref[0, r0:r0 + BLOCK, h * HEAD_DIM:(h + 1) * HEAD_DIM] = out.astype(o_ref.dtype)
        if not with_sink:
            lse_ref[0, r0:r0 + BLOCK, :] = lse_tile


def _attention(q, k, v, *, dilation, n_heads, kv_heads, n_back, sinks=None):
    b, s, qc = q.shape
    kc = k.shape[-1]
    d = dilation
    length = s // d
    assert length * d == s and length % BLOCK == 0
    n_blocks = ATTN_BLOCKS if length % (ATTN_BLOCKS * BLOCK) == 0 else 1
    tq = n_blocks * BLOCK
    with_sink = sinks is not None

    def view(a):
        return a.reshape(b, length, d * a.shape[-1])

    def cur(width):
        return pl.BlockSpec((1, tq, width), lambda bi, r, i: (bi, i, r))

    def prev(width):
        return pl.BlockSpec((1, BLOCK, width), lambda bi, r, i: (bi, jnp.maximum(i * n_blocks - 1, 0), r))

    in_specs = [cur(qc), cur(kc), prev(kc), cur(kc), prev(kc)]
    args = [view(q), view(k), view(k), view(v), view(v)]
    if with_sink:
        in_specs = [pl.BlockSpec(memory_space=pltpu.SMEM)] + in_specs
        args = [sinks.astype(F32)] + args
        out_specs = cur(qc)
        out_shape = jax.ShapeDtypeStruct((b, length, d * qc), BF16)
    else:
        out_specs = [cur(qc), cur(LANES)]
        out_shape = [jax.ShapeDtypeStruct((b, length, d * qc), BF16),
                     jax.ShapeDtypeStruct((b, length, d * LANES), F32)]
    body = functools.partial(_attn_body, n_heads=n_heads, kv_heads=kv_heads, n_back=n_back,
                             n_blocks=n_blocks, with_sink=with_sink)
    out = pl.pallas_call(
        body,
        grid=(b, d, length // tq),
        in_specs=in_specs,
        out_specs=out_specs,
        out_shape=out_shape,
        compiler_params=_params(("parallel", "parallel", "parallel")),
        name="attn_sink" if with_sink else "attn_lse",
    )(*args)
    if with_sink:
        return out.reshape(b, s, qc)
    return out[0].reshape(b, s, qc), out[1].reshape(b, s, LANES)


def _outproj_merge_body(*refs, n_groups):
    o_refs, lse_refs = refs[:n_groups], refs[n_groups:2 * n_groups]
    x_ref, g_ref, w_ref, out_ref = refs[2 * n_groups:]
    lses = [r[...] for r in lse_refs]
    top = functools.reduce(jnp.maximum, lses)
    exps = [jnp.exp(s - top) for s in lses]
    den = functools.reduce(jnp.add, exps)
    n = o_refs[0].shape[1]
    head_of_col = lax.broadcasted_iota(jnp.int32, (LANES, n), 1) // HEAD_DIM
    expand = (head_of_col == lax.broadcasted_iota(jnp.int32, (LANES, n), 0)).astype(BF16)
    merged = None
    for e, o_ref in zip(exps, o_refs):
        w_hi, w_lo = _split(e / den)
        term = (_dot(w_hi, expand) + _dot(w_lo, expand)) * o_ref[...].astype(F32)
        merged = term if merged is None else merged + term
    out_ref[...] = x_ref[...] + _rmsnorm(_dot(merged.astype(BF16), w_ref[...]), g_ref[...])


def _outproj_merge(outs, lses, x, gain, w_out):
    t, k = outs[0].shape
    tm = min(TM_PROJ, t)
    n_groups = len(outs)
    row = lambda width: pl.BlockSpec((tm, width), lambda i: (i, 0))
    return pl.pallas_call(
        functools.partial(_outproj_merge_body, n_groups=n_groups),
        grid=(t // tm,),
        in_specs=[row(k)] * n_groups + [row(LANES)] * n_groups + [row(D_MODEL), _resident((1, D_MODEL)),
                                                                  _resident((k, D_MODEL))],
        out_specs=row(D_MODEL),
        out_shape=jax.ShapeDtypeStruct((t, D_MODEL), F32),
        compiler_params=_params(("parallel",)),
        name="outproj_merge",
    )(*outs, *lses, x, gain.reshape(1, D_MODEL), w_out.astype(BF16))


def _conv_taps(xbuf_ref, w_ref, c0, rows, width):
    acc = None
    for i in range(width):
        start = CARRY_ROWS - (width - 1) + i
        term = xbuf_ref[start:start + rows, c0:c0 + LANES] * w_ref[i:i + 1, c0:c0 + LANES]
        acc = term if acc is None else acc + term
    return acc


def _l2norm(x):
    return x * lax.rsqrt(jnp.sum(x * x, axis=-1, keepdims=True) + NORM_EPS)


def _split(a):
    hi = a.astype(BF16)
    return hi, (a - hi.astype(F32)).astype(BF16)


def _dot_split(a, b):
    (ah, al), (bh, bl) = a, b
    return _dot(ah, bh) + (_dot(ah, bl) + _dot(al, bh))


def _unit_lower_inverses(strict_lowers):
    n = strict_lowers[0].shape[0]
    eye = _eye(n)
    powers = [_split(-l) for l in strict_lowers]
    invs = [eye - l for l in strict_lowers]
    for _ in range(int(math.log2(n)) - 1):
        powers = [_split(_dot_split(p, p)) for p in powers]
        invs = [inv + _dot_split(_split(inv), p) for inv, p in zip(invs, powers)]
    return invs


def _gdn_body(qkv_ref, z_ref, small_ref, convw_ref, alog_ref, dtb_ref, normw_ref, o_ref, xbuf_ref, state_ref):
    c = GDN_CHUNK
    nh, dk = GDN_HEADS, GDN_HEAD_DIM
    width = nh * dk

    @pl.when(pl.program_id(1) == 0)
    def _():
        xbuf_ref[0:CARRY_ROWS, :] = jnp.zeros((CARRY_ROWS, 3 * width), F32)
        state_ref[...] = jnp.zeros_like(state_ref)

    xbuf_ref[CARRY_ROWS:CARRY_ROWS + c, :] = qkv_ref[0]

    small = small_ref[0]
    beta_all = jax.nn.sigmoid(small)
    g_all = -jnp.exp(alog_ref[...]) * _softplus(small + dtb_ref[...])
    gcum_all = jnp.dot(_tril(c), g_all, preferred_element_type=F32, precision=HI)
    gcum_rows = _dot_nt(_eye(LANES), gcum_all, precision=HI)
    ri = lax.broadcasted_iota(jnp.int32, (c, c), 0)
    ci = lax.broadcasted_iota(jnp.int32, (c, c), 1)
    causal, strict = ri >= ci, ri > ci
    heads = range(nh)

    def conv(c0):
        return _silu(_conv_taps(xbuf_ref, convw_ref, c0, c, GDN_CONV))

    qs = [_l2norm(conv(h * dk)) * (dk ** -0.5) for h in heads]
    ks = [_l2norm(conv(width + h * dk)) for h in heads]
    vs = [conv(2 * width + h * dk) for h in heads]
    betas = [beta_all[:, h:h + 1] for h in heads]
    gcums = [gcum_all[:, nh + h:nh + h + 1] for h in heads]
    decays = [jnp.exp(jnp.where(causal, gcums[h] - gcum_rows[nh + h:nh + h + 1, :], -jnp.inf)) for h in heads]
    k_bs = [k.astype(BF16) for k in ks]
    q_bs = [q.astype(BF16) for q in qs]
    lowers = [jnp.where(strict, _dot_nt(k_bs[h], k_bs[h]) * decays[h] * betas[h], 0.0) for h in heads]
    intras = [jnp.where(causal, _dot_nt(q_bs[h], k_bs[h]) * decays[h], 0.0).astype(BF16) for h in heads]
    invs = _unit_lower_inverses(lowers)
    rhss = [jnp.concatenate([vs[h] * betas[h], ks[h] * (betas[h] * jnp.exp(gcums[h]))], axis=-1) for h in heads]
    sols = [_dot_split(_split(invs[h]), _split(rhss[h])) for h in heads]
    states = [state_ref[h] for h in heads]
    state_bs = [s.astype(BF16) for s in states]
    v_news = [sols[h][:, :dk] - _dot(sols[h][:, dk:].astype(BF16), state_bs[h]) for h in heads]
    v_new_bs = [v.astype(BF16) for v in v_news]
    outs = [_dot((qs[h] * jnp.exp(gcums[h])).astype(BF16), state_bs[h]) + _dot(intras[h], v_new_bs[h])
            for h in heads]
    for h in heads:
        g_last = gcums[h][c - 1:c, :]
        k_end = (ks[h] * jnp.exp(g_last - gcums[h])).astype(BF16)
        state_ref[h] = states[h] * jnp.exp(g_last) + _dot_tn(k_end, v_new_bs[h])
    for h in heads:
        gate = _silu(z_ref[0, :, h * dk:(h + 1) * dk])
        o_ref[0, :, h * dk:(h + 1) * dk] = (_rmsnorm(outs[h], normw_ref[...]) * gate).astype(o_ref.dtype)

    xbuf_ref[0:CARRY_ROWS, :] = xbuf_ref[c:c + CARRY_ROWS, :]


def _gdn_core(qkv, z, small, conv_w, a_log, dt_bias, norm_w):
    b, s, _ = qkv.shape
    c = GDN_CHUNK
    nh, dk = GDN_HEADS, GDN_HEAD_DIM
    width = nh * dk
    pad = lambda a: jnp.zeros((1, LANES), F32).at[0, nh:2 * nh].set(a.astype(F32))
    return pl.pallas_call(
        _gdn_body,
        grid=(b, s // c),
        in_specs=[pl.BlockSpec((1, c, 3 * width), lambda bi, i: (bi, i, 0)),
                  pl.BlockSpec((1, c, width), lambda bi, i: (bi, i, 0)),
                  pl.BlockSpec((1, c, LANES), lambda bi, i: (bi, i, 0)),
                  _resident((GDN_CONV, 3 * width)),
                  _resident((1, LANES)),
                  _resident((1, LANES)),
                  _resident((1, dk))],
        out_specs=pl.BlockSpec((1, c, width), lambda bi, i: (bi, i, 0)),
        out_shape=jax.ShapeDtypeStruct((b, s, width), BF16),
        scratch_shapes=[pltpu.VMEM((CARRY_ROWS + c, 3 * width), F32),
                        pltpu.VMEM((nh, dk, dk), F32)],
        compiler_params=_params(("parallel", "arbitrary")),
        name="gdn",
    )(qkv, z, small, conv_w.astype(F32), pad(a_log), pad(dt_bias), norm_w.reshape(1, dk).astype(F32))


def _ssd_body(z_ref, xbc_ref, dt_ref, convw_ref, convb_ref, dtb_ref, alog_ref, dskip_ref, normw_ref,
              o_ref, xbuf_ref, state_ref):
    c = SSD_CHUNK
    di, ng, p = SSD_D_INNER, SSD_GROUPS, SSD_HEAD_DIM
    heads_per_group = SSD_HEADS // ng
    pairs_per_group = heads_per_group // 2
    group_width = di // ng

    @pl.when(pl.program_id(1) == 0)
    def _():
        xbuf_ref[0:CARRY_ROWS, :] = jnp.zeros((CARRY_ROWS, xbuf_ref.shape[1]), F32)
        state_ref[...] = jnp.zeros_like(state_ref)

    xbuf_ref[CARRY_ROWS:CARRY_ROWS + c, :] = xbc_ref[0]

    def conv(c0):
        return _silu(_conv_taps(xbuf_ref, convw_ref, c0, c, SSD_CONV) + convb_ref[:, c0:c0 + LANES])

    dt_all = _softplus(dt_ref[0] + dtb_ref[...])
    adt = dt_all * -jnp.exp(alog_ref[...])
    acum_all = jnp.dot(_tril(c), adt, preferred_element_type=F32, precision=HI)
    acum_rows = _dot_nt(_eye(LANES), acum_all, precision=HI)
    ri = lax.broadcasted_iota(jnp.int32, (c, c), 0)
    ci = lax.broadcasted_iota(jnp.int32, (c, c), 1)
    causal = ri >= ci
    low_half = lax.broadcasted_iota(jnp.int32, (c, LANES), 1) < p
    eye_b = _eye(LANES, BF16)

    for g in range(ng):
        bm = conv(di + g * SSD_STATE).astype(BF16)
        cm = conv(di + ng * SSD_STATE + g * SSD_STATE).astype(BF16)
        cb = _dot_nt(cm, bm)
        bm_t = _dot_nt(eye_b, bm).astype(BF16)
        ys = []
        sumsq = jnp.zeros((c, 1), F32)
        for pp in range(pairs_per_group):
            pair = g * pairs_per_group + pp
            ha, hb = 2 * pair, 2 * pair + 1
            c0 = pair * LANES
            x = conv(c0)

            def per_head(fn):
                return jnp.where(low_half, fn(ha), fn(hb))

            def seg(hd):
                diff = acum_all[:, hd:hd + 1] - acum_rows[hd:hd + 1, :]
                return (cb * jnp.exp(jnp.where(causal, diff, -jnp.inf))).astype(BF16)

            x_dt = x * per_head(lambda hd: dt_all[:, hd:hd + 1])
            x_dt_b = x_dt.astype(BF16)
            y_diag = jnp.where(low_half, _dot(seg(ha), x_dt_b), _dot(seg(hb), x_dt_b))
            state = state_ref[pair]
            y_off = _dot(cm, state.astype(BF16)) * per_head(lambda hd: jnp.exp(acum_all[:, hd:hd + 1]))
            a_last = per_head(lambda hd: acum_all[c - 1:c, hd:hd + 1])
            to_end = jnp.exp(a_last - per_head(lambda hd: acum_all[:, hd:hd + 1]))
            state_ref[pair] = state * jnp.exp(a_last[0:1, :]) + _dot(bm_t, (x_dt * to_end).astype(BF16))
            y = y_diag + y_off + x * dskip_ref[:, c0:c0 + LANES]
            y = y * _silu(z_ref[0, :, c0:c0 + LANES])
            ys.append(y)
            sumsq = sumsq + jnp.sum(y * y, axis=-1, keepdims=True)
        scale = lax.rsqrt(sumsq * (1.0 / group_width) + NORM_EPS)
        for pp, y in enumerate(ys):
            c0 = (g * pairs_per_group + pp) * LANES
            o_ref[0, :, c0:c0 + LANES] = (y * scale * normw_ref[:, c0:c0 + LANES]).astype(o_ref.dtype)

    xbuf_ref[0:CARRY_ROWS, :] = xbuf_ref[c:c + CARRY_ROWS, :]


def _ssd_core(z, xbc, dt_raw, conv_w, conv_b, dt_bias, a_log, d_skip, norm_w):
    b, s, conv_ch = xbc.shape
    c = SSD_CHUNK
    di = SSD_D_INNER
    pad = lambda a: jnp.zeros((1, LANES), F32).at[0, :SSD_HEADS].set(a.astype(F32))
    return pl.pallas_call(
        _ssd_body,
        grid=(b, s // c),
        in_specs=[pl.BlockSpec((1, c, di), lambda bi, i: (bi, i, 0)),
                  pl.BlockSpec((1, c, conv_ch), lambda bi, i: (bi, i, 0)),
                  pl.BlockSpec((1, c, LANES), lambda bi, i: (bi, i, 0)),
                  _resident((SSD_CONV, conv_ch)),
                  _resident((1, conv_ch)),
                  _resident((1, LANES)),
                  _resident((1, LANES)),
                  _resident((1, di)),
                  _resident((1, di))],
        out_specs=pl.BlockSpec((1, c, di), lambda bi, i: (bi, i, 0)),
        out_shape=jax.ShapeDtypeStruct((b, s, di), BF16),
        scratch_shapes=[pltpu.VMEM((CARRY_ROWS + c, conv_ch), F32),
                        pltpu.VMEM((SSD_HEADS // 2, SSD_STATE, LANES), F32)],
        compiler_params=_params(("parallel", "arbitrary")),
        name="ssd",
    )(z, xbc, dt_raw, conv_w.astype(F32), conv_b.reshape(1, conv_ch).astype(F32), pad(dt_bias), pad(a_log),
      jnp.repeat(d_skip.astype(F32), SSD_HEAD_DIM).reshape(1, di), norm_w.reshape(1, di).astype(F32))


def _pad_cols(w, n):
    return jnp.pad(w, ((0, 0), (0, n - w.shape[1])))


def _swa_mixer(x, b, s, gain, pos_b, freq, w_in, b_in, sinks, w_out, gain_post):
    nq, nkv = SWA_HEADS * HEAD_DIM, SWA_KV_HEADS * HEAD_DIM
    layout = (("q", 0, nq), ("k", nq, nkv), ("v", nq + nkv, nkv))
    q, k, v = _proj_attn(x, gain, pos_b, freq, w_in, b_in, layout)
    o = _attention(q.reshape(b, s, nq), k.reshape(b, s, nkv), v.reshape(b, s, nkv), dilation=1,
                   n_heads=SWA_HEADS, kv_heads=SWA_KV_HEADS, n_back=SWA_WINDOW - 1, sinks=sinks)
    return _outproj(o.reshape(b * s, nq), x, gain_post, w_out)


def _dilated_mixer(x, b, s, gain, pos_b, freq, w_in, w_out, gain_post):
    nq, nkv = DIL_HEADS * HEAD_DIM, DIL_KV_HEADS * HEAD_DIM
    per_group = nq + 2 * nkv
    layout = []
    for gi in range(len(DIL_PATTERN)):
        base = gi * per_group
        layout += [("q", base, nq), ("k", base + nq, nkv), ("v", base + nq + nkv, nkv)]
    outs = _proj_attn(x, gain, pos_b, freq, w_in, jnp.zeros((w_in.shape[1],), F32), tuple(layout))
    group_outs, group_lses = [], []
    for gi, (window, dilation) in enumerate(DIL_PATTERN):
        q, k, v = outs[3 * gi:3 * gi + 3]
        o, lse = _attention(q.reshape(b, s, nq), k.reshape(b, s, nkv), v.reshape(b, s, nkv), dilation=dilation,
                            n_heads=DIL_HEADS, kv_heads=DIL_KV_HEADS, n_back=window // dilation)
        group_outs.append(o.reshape(b * s, nq))
        group_lses.append(lse.reshape(b * s, LANES))
    return _outproj_merge(group_outs, group_lses, x, gain_post, w_out)


def _gdn_mixer(x, b, s, gain, w_in, conv_w, a_log, dt_bias, norm_w, w_out, gain_post):
    width = GDN_HEADS * GDN_HEAD_DIM
    w = jnp.concatenate([w_in[:, :4 * width], _pad_cols(w_in[:, 4 * width:], LANES)], axis=1)
    layout = ((0, 3 * width), (3 * width, width), (4 * width, LANES))
    qkv, z, small = _proj_plain(x, gain, w, layout)
    o = _gdn_core(qkv.reshape(b, s, 3 * width), z.reshape(b, s, width), small.reshape(b, s, LANES),
                  conv_w, a_log, dt_bias, norm_w)
    return _outproj(o.reshape(b * s, width), x, gain_post, w_out)


def _ssd_mixer(x, b, s, gain, w_in, conv_w, conv_b, dt_bias, a_log, d_skip, norm_w, w_out, gain_post):
    di = SSD_D_INNER
    conv_ch = di + 2 * SSD_GROUPS * SSD_STATE
    w = jnp.concatenate([w_in[:, :di + conv_ch], _pad_cols(w_in[:, di + conv_ch:], LANES)], axis=1)
    layout = ((0, di), (di, conv_ch), (di + conv_ch, LANES))
    z, xbc, dt_raw = _proj_plain(x, gain, w, layout)
    o = _ssd_core(z.reshape(b, s, di), xbc.reshape(b, s, conv_ch), dt_raw.reshape(b, s, LANES),
                  conv_w, conv_b, dt_bias, a_log, d_skip, norm_w)
    return _outproj(o.reshape(b * s, di), x, gain_post, w_out)


def kernel(x, positions, l0_norms, l0_ffn_w_in, l0_ffn_w_out, a_w_in, a_b_in, a_sinks, a_w_out, l1_norms, l1_ffn_w_in, l1_ffn_w_out, b_w_in, b_conv_w, b_A_log, b_dt_bias, b_norm, b_w_out, l2_norms, l2_ffn_w_in, l2_ffn_w_out, c_w_in, c_conv_w, c_conv_b, c_dt_bias, c_A_log, c_D, c_norm, c_w_out, l3_norms, l3_ffn_w_in, l3_ffn_w_out, d_w_in, d_w_out):
    b, s, d = x.shape
    t = b * s
    pos_b = jnp.broadcast_to(positions.astype(F32).reshape(t, 1), (t, LANES))
    inv_freq = ROPE_THETA ** (-jnp.arange(0, ROPE_DIMS, 2, dtype=F32) / ROPE_DIMS)
    freq = jnp.tile(inv_freq, LANES // inv_freq.shape[0]).reshape(1, LANES)

    mixers = (
        lambda h, n: _swa_mixer(h, b, s, n[2], pos_b, freq, a_w_in, a_b_in, a_sinks, a_w_out, n[3]),
        lambda h, n: _gdn_mixer(h, b, s, n[2], b_w_in, b_conv_w, b_A_log, b_dt_bias, b_norm, b_w_out, n[3]),
        lambda h, n: _ssd_mixer(h, b, s, n[2], c_w_in, c_conv_w, c_conv_b, c_dt_bias, c_A_log, c_D, c_norm,
                                c_w_out, n[3]),
        lambda h, n: _dilated_mixer(h, b, s, n[2], pos_b, freq, d_w_in, d_w_out, n[3]),
    )
    layers = ((l0_norms, l0_ffn_w_in, l0_ffn_w_out), (l1_norms, l1_ffn_w_in, l1_ffn_w_out),
              (l2_norms, l2_ffn_w_in, l2_ffn_w_out), (l3_norms, l3_ffn_w_in, l3_ffn_w_out))
    h = x.reshape(t, d)
    for i, (norms, ffn_w_in, ffn_w_out) in enumerate(layers):
        norms = norms.astype(F32)
        h = _ffn(h, norms[0:2], ffn_w_in[0], ffn_w_out[0])
        h = mixers[i % len(mixers)](h, norms)
        h = _ffn(h, norms[4:6], ffn_w_in[1], ffn_w_out[1])
    return h.reshape(b, s, d)
```

```python
import functools
import math

import jax
import jax.numpy as jnp
from jax import lax
from jax.experimental import pallas as pl
from jax.experimental.pallas import tpu as pltpu

F32 = jnp.float32
BF16 = jnp.bfloat16

D_MODEL = 1024
D_FF = 2816
NORM_EPS = 1e-6
ROPE_THETA = 500000.0
ROPE_DIMS = 16
HEAD_DIM = 64
BLOCK = 128
NEG_INF = -1e30

SWA_HEADS, SWA_KV_HEADS, SWA_WINDOW = 16, 2, 128
GDN_HEADS, GDN_HEAD_DIM, GDN_CONV, GDN_CHUNK = 8, 128, 4, 64
SSD_D_INNER, SSD_HEAD_DIM, SSD_GROUPS, SSD_STATE, SSD_CONV, SSD_CHUNK = 2048, 64, 4, 128, 4, 128
SSD_HEADS = SSD_D_INNER // SSD_HEAD_DIM
DIL_PATTERN = ((128, 1), (512, 4), (2048, 16))
DIL_HEADS, DIL_KV_HEADS = 16, 4

LANES = 128
CARRY_ROWS = 8
VMEM_LIMIT = 56 * 1024 * 1024
HI = lax.Precision.HIGHEST

TM_FFN = 512
TM_PROJ = 512
FF_CHUNK = 1408
ATTN_BLOCKS = 2


def _params(sem):
    return pltpu.CompilerParams(dimension_semantics=sem, vmem_limit_bytes=VMEM_LIMIT)


def _resident(shape):
    nd = len(shape)
    return pl.BlockSpec(shape, lambda *_: (0,) * nd, pipeline_mode=pl.Buffered(1))


def _rmsnorm(x, gain):
    return x * lax.rsqrt(jnp.mean(x * x, axis=-1, keepdims=True) + NORM_EPS) * gain


def _silu(x):
    half = 0.5 * x
    return half * (jnp.tanh(half) + 1.0)


def _softplus(x):
    return jnp.maximum(x, 0.0) + jnp.log(1.0 + jnp.exp(-jnp.abs(x)))


def _dot(a, b):
    return jnp.dot(a, b, preferred_element_type=F32)


def _dot_nt(a, b, precision=None):
    return lax.dot_general(a, b, (((1,), (1,)), ((), ())), preferred_element_type=F32, precision=precision)


def _dot_tn(a, b):
    return lax.dot_general(a, b, (((0,), (0,)), ((), ())), preferred_element_type=F32)


def _eye(n, dtype=F32):
    return (lax.broadcasted_iota(jnp.int32, (n, n), 0) == lax.broadcasted_iota(jnp.int32, (n, n), 1)).astype(dtype)


def _tril(n, dtype=F32):
    return (lax.broadcasted_iota(jnp.int32, (n, n), 0) >= lax.broadcasted_iota(jnp.int32, (n, n), 1)).astype(dtype)


def _ffn_body(x_ref, g_ref, win_ref, wout_ref, o_ref, act_ref):
    x = x_ref[...]
    h = _rmsnorm(x, g_ref[0:1, :]).astype(BF16)
    for lo in range(0, D_FF, FF_CHUNK):
        gate = _dot(h, win_ref[:, lo:lo + FF_CHUNK])
        up = _dot(h, win_ref[:, D_FF + lo:D_FF + lo + FF_CHUNK])
        act_ref[:, lo:lo + FF_CHUNK] = (_silu(gate) * up).astype(BF16)
    y = _dot(act_ref[...], wout_ref[...])
    o_ref[...] = x + 0.5 * _rmsnorm(y, g_ref[1:2, :])


def _ffn(x, gains, w_in, w_out):
    t = x.shape[0]
    tm = min(TM_FFN, t)
    return pl.pallas_call(
        _ffn_body,
        grid=(t // tm,),
        in_specs=[pl.BlockSpec((tm, D_MODEL), lambda i: (i, 0)),
                  _resident((2, D_MODEL)),
                  _resident((D_MODEL, 2 * D_FF)),
                  _resident((D_FF, D_MODEL))],
        out_specs=pl.BlockSpec((tm, D_MODEL), lambda i: (i, 0)),
        out_shape=jax.ShapeDtypeStruct((t, D_MODEL), F32),
        scratch_shapes=[pltpu.VMEM((tm, D_FF), BF16)],
        compiler_params=_params(("parallel",)),
        name="ffn",
    )(x, gains, w_in.astype(BF16), w_out.astype(BF16))


def _outproj_body(y_ref, x_ref, g_ref, w_ref, o_ref):
    o_ref[...] = x_ref[...] + _rmsnorm(_dot(y_ref[...], w_ref[...]), g_ref[...])


def _outproj(y, x, gain, w_out):
    t, k = y.shape
    tm = min(TM_PROJ, t)
    return pl.pallas_call(
        _outproj_body,
        grid=(t // tm,),
        in_specs=[pl.BlockSpec((tm, k), lambda i: (i, 0)),
                  pl.BlockSpec((tm, D_MODEL), lambda i: (i, 0)),
                  _resident((1, D_MODEL)),
                  _resident((k, D_MODEL))],
        out_specs=pl.BlockSpec((tm, D_MODEL), lambda i: (i, 0)),
        out_shape=jax.ShapeDtypeStruct((t, D_MODEL), F32),
        compiler_params=_params(("parallel",)),
        name="outproj",
    )(y, x, gain.reshape(1, D_MODEL), w_out.astype(BF16))


def _rope_tables(pos_ref, freq_ref):
    ang = pos_ref[...] * freq_ref[...]
    d = lax.broadcasted_iota(jnp.int32, ang.shape, 1) % HEAD_DIM
    half = ROPE_DIMS // 2
    cos, sin = jnp.cos(ang), jnp.sin(ang)
    c = jnp.where(d < ROPE_DIMS, cos, 1.0)
    s_lo = jnp.where(d < half, -sin, 0.0)
    s_hi = jnp.where((d >= half) & (d < ROPE_DIMS), sin, 0.0)
    return c, s_lo, s_hi


def _rope(y, tables):
    c, s_lo, s_hi = tables
    half = ROPE_DIMS // 2
    return y * c + pltpu.roll(y, LANES - half, 1) * s_lo + pltpu.roll(y, half, 1) * s_hi


PROJ_CHUNK = 512


def _proj_attn_body(*refs, layout, q_scale, has_bias):
    x_ref, g_ref, pos_ref, freq_ref, w_ref = refs[:5]
    refs = refs[5:]
    if has_bias:
        b_ref, refs = refs[0], refs[1:]
    out_refs, tmp_ref = refs[:len(layout)], refs[len(layout)]
    rows = x_ref.shape[0]
    h = _rmsnorm(x_ref[...], g_ref[...]).astype(BF16)
    k_tables = _rope_tables(pos_ref, freq_ref)
    q_tables = tuple(t * q_scale for t in k_tables)
    for (kind, start, width, dilation), o_ref in zip(layout, out_refs):
        chunk = min(width, PROJ_CHUNK)
        for lo in range(0, width, chunk):
            y = _dot(h, w_ref[:, start + lo:start + lo + chunk])
            if has_bias:
                y = y + b_ref[:, start + lo:start + lo + chunk]
            for sub in range(0, chunk, LANES):
                ys = y[:, sub:sub + LANES]
                if kind == "q":
                    ys = _rope(ys, q_tables)
                elif kind == "k":
                    ys = _rope(ys, k_tables)
                if dilation == 1:
                    o_ref[0, :, lo + sub:lo + sub + LANES] = ys.astype(o_ref.dtype)
                else:
                    tmp_ref[sub // LANES] = ys
            if dilation > 1:
                for sub in range(0, chunk, LANES):
                    for r in range(dilation):
                        o_ref[r, :, lo + sub:lo + sub + LANES] = tmp_ref[
                            sub // LANES, pl.ds(r, rows // dilation, stride=dilation), :].astype(o_ref.dtype)


def _proj_attn(x, b, s, gain, pos_b, freq, w_in, b_in, layout):
    n = w_in.shape[1]
    tm = min(TM_PROJ, s)
    steps = s // tm
    has_bias = b_in is not None
    body = functools.partial(_proj_attn_body, layout=layout, q_scale=HEAD_DIM ** -0.5, has_bias=has_bias)
    row = lambda width: pl.BlockSpec((tm, width), lambda bi, i: (bi * steps + i, 0))
    in_specs = [row(D_MODEL), _resident((1, D_MODEL)), row(LANES), _resident((1, LANES)), _resident((D_MODEL, n))]
    args = [x, gain.reshape(1, D_MODEL), pos_b, freq, w_in.astype(BF16)]
    if has_bias:
        in_specs.append(_resident((1, n)))
        args.append(b_in.reshape(1, n).astype(F32))
    return pl.pallas_call(
        body,
        grid=(b, steps),
        in_specs=in_specs,
        out_specs=[pl.BlockSpec((None, d, tm // d, w), lambda bi, i: (bi, 0, i, 0)) for _, _, w, d in layout],
        out_shape=[jax.ShapeDtypeStruct((b, d, s // d, w), BF16) for _, _, w, d in layout],
        scratch_shapes=[pltpu.VMEM((PROJ_CHUNK // LANES, tm, LANES), F32)],
        compiler_params=_params(("parallel", "parallel")),
        name="proj_attn",
    )(*args)


def _proj_plain_body(x_ref, g_ref, w_ref, *out_refs, layout):
    h = _rmsnorm(x_ref[...], g_ref[...]).astype(BF16)
    for (start, width), o_ref in zip(layout, out_refs):
        chunk = 512 if width % 512 == 0 else LANES
        for lo in range(0, width, chunk):
            o_ref[:, lo:lo + chunk] = _dot(h, w_ref[:, start + lo:start + lo + chunk])


def _proj_plain(x, gain, w_in, layout):
    t = x.shape[0]
    n = w_in.shape[1]
    tm = min(TM_PROJ, t)
    return pl.pallas_call(
        functools.partial(_proj_plain_body, layout=layout),
        grid=(t // tm,),
        in_specs=[pl.BlockSpec((tm, D_MODEL), lambda i: (i, 0)),
                  _resident((1, D_MODEL)),
                  _resident((D_MODEL, n))],
        out_specs=[pl.BlockSpec((tm, w), lambda i: (i, 0)) for _, w in layout],
        out_shape=[jax.ShapeDtypeStruct((t, w), F32) for _, w in layout],
        compiler_params=_params(("parallel",)),
        name="proj_plain",
    )(x, gain.reshape(1, D_MODEL), w_in.astype(BF16))


def _attn_body(*refs, n_heads, kv_heads, n_back, n_blocks, with_sink):
    if with_sink:
        sink_ref, q_ref, kc_ref, kp_ref, vc_ref, vp_ref, o_ref = refs
    else:
        q_ref, kc_ref, kp_ref, vc_ref, vp_ref, o_ref, lse_ref = refs
    gq = n_heads // kv_heads
    assert gq % 2 == 0 and kv_heads % 2 == 0
    kw = 2 * BLOCK
    step = pl.program_id(2)
    k_all = jnp.concatenate([kp_ref[...], kc_ref[...]], axis=0)
    v_all = jnp.concatenate([vp_ref[...], vc_ref[...]], axis=0)
    rows = lax.broadcasted_iota(jnp.int32, (BLOCK, 2 * kw), 0) + BLOCK
    cols = lax.broadcasted_iota(jnp.int32, (BLOCK, 2 * kw), 1) % kw
    band = (rows - cols >= 0) & (rows - cols <= n_back)
    lane = lax.broadcasted_iota(jnp.int32, (BLOCK, LANES), 1)
    low = lane < HEAD_DIM
    key_low = lax.broadcasted_iota(jnp.int32, (kw, LANES), 1) < HEAD_DIM
    zeros_b = jnp.zeros((kw, LANES), BF16)
    ones_low, ones_high = key_low.astype(BF16), jnp.logical_not(key_low).astype(BF16)
    for j in range(n_blocks):
        r0 = j * BLOCK
        has_prev = (step * n_blocks + j) > 0
        visible = band & ((cols >= BLOCK) | has_prev)
        keys, values = [], []
        for t in range(kv_heads // 2):
            k_tile = k_all[r0:r0 + kw, t * LANES:(t + 1) * LANES]
            v_tile = v_all[r0:r0 + kw, t * LANES:(t + 1) * LANES]
            k_swap, v_swap = pltpu.roll(k_tile, HEAD_DIM, 1), pltpu.roll(v_tile, HEAD_DIM, 1)
            for first in (True, False):
                k_dup = jnp.where(key_low == first, k_tile, k_swap)
                v_dup = jnp.where(key_low == first, v_tile, v_swap)
                keys.append(jnp.concatenate([jnp.where(key_low, k_dup, zeros_b),
                                             jnp.where(key_low, zeros_b, k_dup)], axis=0))
                values.append(jnp.concatenate(
                    [jnp.concatenate([jnp.where(key_low, v_dup, zeros_b), ones_low], axis=1),
                     jnp.concatenate([jnp.where(key_low, zeros_b, v_dup), ones_high], axis=1)], axis=0))
        pairs = range(n_heads // 2)
        kv_of = [2 * p // gq for p in pairs]
        scores = [jnp.where(visible, _dot_nt(q_ref[r0:r0 + BLOCK, p * LANES:(p + 1) * LANES], keys[kv_of[p]]), NEG_INF)
                  for p in pairs]
        max_a = [jnp.max(s[:, :kw], axis=-1, keepdims=True) for s in scores]
        max_b = [jnp.max(s[:, kw:], axis=-1, keepdims=True) for s in scores]
        probs = [jnp.concatenate([jnp.exp(s[:, :kw] - ma), jnp.exp(s[:, kw:] - mb)], axis=1).astype(BF16)
                 for s, ma, mb in zip(scores, max_a, max_b)]
        accs = [_dot(pr, values[kv_of[p]]) for p, pr in zip(pairs, probs)]
        lse_tile = jnp.zeros((BLOCK, LANES), F32)
        for p in pairs:
            acc, l = accs[p][:, :LANES], accs[p][:, LANES:]
            m = jnp.where(low, max_a[p], max_b[p])
            if with_sink:
                sink = jnp.where(low, sink_ref[2 * p], sink_ref[2 * p + 1])
                m_all = jnp.maximum(m, sink)
                rescale = jnp.exp(m - m_all)
                out = acc * (rescale / (l * rescale + jnp.exp(sink - m_all)))
            else:
                out = acc / l
                lse_tile = jnp.where((lane == 2 * p) | (lane == HEAD_DIM + 2 * p + 1), m + jnp.log(l), lse_tile)
            o_ref[r0:r0 + BLOCK, p * LANES:(p + 1) * LANES] = out.astype(o_ref.dtype)
        if not with_sink:
            lse_ref[r0:r0 + BLOCK, :] = lse_tile


def _lse_lane_of_head(h):
    return h if h % 2 == 0 else HEAD_DIM + h


def _attention(q, k, v, *, n_heads, kv_heads, n_back, sinks=None):
    b, d, length, qc = q.shape
    kc = k.shape[-1]
    assert length % BLOCK == 0
    n_blocks = ATTN_BLOCKS if length % (ATTN_BLOCKS * BLOCK) == 0 else 1
    tq = n_blocks * BLOCK
    with_sink = sinks is not None

    def cur(width):
        return pl.BlockSpec((None, None, tq, width), lambda bi, r, i: (bi, r, i, 0))

    def prev(width):
        return pl.BlockSpec((None, None, BLOCK, width), lambda bi, r, i: (bi, r, jnp.maximum(i * n_blocks - 1, 0), 0))

    in_specs = [cur(qc), cur(kc), prev(kc), cur(kc), prev(kc)]
    args = [q, k, k, v, v]
    if with_sink:
        in_specs = [pl.BlockSpec(memory_space=pltpu.SMEM)] + in_specs
        args = [sinks.astype(F32)] + args
        out_specs = cur(qc)
        out_shape = jax.ShapeDtypeStruct((b, d, length, qc), BF16)
    else:
        out_specs = [cur(qc), cur(LANES)]
        out_shape = [jax.ShapeDtypeStruct((b, d, length, qc), BF16),
                     jax.ShapeDtypeStruct((b, d, length, LANES), F32)]
    body = functools.partial(_attn_body, n_heads=n_heads, kv_heads=kv_heads, n_back=n_back,
                             n_blocks=n_blocks, with_sink=with_sink)
    return pl.pallas_call(
        body,
        grid=(b, d, length // tq),
        in_specs=in_specs,
        out_specs=out_specs,
        out_shape=out_shape,
        compiler_params=_params(("parallel", "parallel", "parallel")),
        name="attn_sink" if with_sink else "attn_lse",
    )(*args)


def _outproj_merge_body(*refs, dilations, n_heads):
    n_groups = len(dilations)
    o_refs, lse_refs = refs[:n_groups], refs[n_groups:2 * n_groups]
    x_ref, g_ref, w_ref, out_ref, o_nat_ref, lse_nat_ref = refs[2 * n_groups:]
    rows = x_ref.shape[0]

    def natural(ref, scratch, d):
        if d == 1:
            return ref[0].astype(F32)
        slabs = scratch.shape[0]
        for r in range(d):
            for c in range(slabs):
                scratch[c, pl.ds(r, rows // d, stride=d), :] = ref[r, :, c * LANES:(c + 1) * LANES].astype(F32)
        return jnp.concatenate([scratch[c] for c in range(slabs)], axis=1) if slabs > 1 else scratch[0]

    lses = [natural(lse_refs[g], lse_nat_ref.at[g:g + 1], d) for g, d in enumerate(dilations)]
    top = functools.reduce(jnp.maximum, lses)
    exps = [jnp.exp(s - top) for s in lses]
    den = functools.reduce(jnp.add, exps)
    n = o_refs[0].shape[-1]
    lane = lax.broadcasted_iota(jnp.int32, (LANES, n), 0)
    head_of_col = lax.broadcasted_iota(jnp.int32, (LANES, n), 1) // HEAD_DIM
    expand = functools.reduce(jnp.logical_or, [(lane == _lse_lane_of_head(h)) & (head_of_col == h)
                                               for h in range(n_heads)]).astype(BF16)
    merged = None
    for e, o_ref, d in zip(exps, o_refs, dilations):
        w_hi, w_lo = _split(e / den)
        term = (_dot(w_hi, expand) + _dot(w_lo, expand)) * natural(o_ref, o_nat_ref, d)
        merged = term if merged is None else merged + term
    out_ref[...] = x_ref[...] + _rmsnorm(_dot(merged.astype(BF16), w_ref[...]), g_ref[...])


def _outproj_merge(outs, lses, x, b, s, gain, w_out, n_heads):
    k = outs[0].shape[-1]
    dilations = tuple(o.shape[1] for o in outs)
    tm = min(TM_PROJ, s)
    steps = s // tm
    grouped = lambda d, width: pl.BlockSpec((None, d, tm // d, width), lambda bi, i: (bi, 0, i, 0))
    row = pl.BlockSpec((tm, D_MODEL), lambda bi, i: (bi * steps + i, 0))
    return pl.pallas_call(
        functools.partial(_outproj_merge_body, dilations=dilations, n_heads=n_heads),
        grid=(b, steps),
        in_specs=[grouped(d, k) for d in dilations] + [grouped(d, LANES) for d in dilations]
                 + [row, _resident((1, D_MODEL)), _resident((k, D_MODEL))],
        out_specs=row,
        out_shape=jax.ShapeDtypeStruct((b * s, D_MODEL), F32),
        scratch_shapes=[pltpu.VMEM((k // LANES, tm, LANES), F32), pltpu.VMEM((len(dilations), tm, LANES), F32)],
        compiler_params=_params(("parallel", "parallel")),
        name="outproj_merge",
    )(*outs, *lses, x, gain.reshape(1, D_MODEL), w_out.astype(BF16))


def _conv_taps(xbuf_ref, w_ref, c0, rows, width):
    acc = None
    for i in range(width):
        start = CARRY_ROWS - (width - 1) + i
        term = xbuf_ref[start:start + rows, c0:c0 + LANES] * w_ref[i:i + 1, c0:c0 + LANES]
        acc = term if acc is None else acc + term
    return acc


def _l2norm(x):
    return x * lax.rsqrt(jnp.sum(x * x, axis=-1, keepdims=True) + NORM_EPS)


def _split(a):
    hi = a.astype(BF16)
    return hi, (a - hi.astype(F32)).astype(BF16)


def _dot_split(a, b):
    (ah, al), (bh, bl) = a, b
    return _dot(ah, bh) + (_dot(ah, bl) + _dot(al, bh))


def _unit_lower_inverses(strict_lowers):
    n = strict_lowers[0].shape[0]
    eye = _eye(n)
    powers = [_split(-l) for l in strict_lowers]
    invs = [eye - l for l in strict_lowers]
    for _ in range(int(math.log2(n)) - 1):
        powers = [_split(_dot_split(p, p)) for p in powers]
        invs = [inv + _dot_split(_split(inv), p) for inv, p in zip(invs, powers)]
    return invs


def _gdn_body(qkv_ref, z_ref, small_ref, convw_ref, alog_ref, dtb_ref, normw_ref, o_ref, xbuf_ref, state_ref):
    c = GDN_CHUNK
    nh, dk = GDN_HEADS, GDN_HEAD_DIM
    width = nh * dk

    @pl.when(pl.program_id(1) == 0)
    def _():
        xbuf_ref[0:CARRY_ROWS, :] = jnp.zeros((CARRY_ROWS, 3 * width), F32)
        state_ref[...] = jnp.zeros_like(state_ref)

    xbuf_ref[CARRY_ROWS:CARRY_ROWS + c, :] = qkv_ref[0]

    small = small_ref[0]
    beta_all = jax.nn.sigmoid(small)
    g_all = -jnp.exp(alog_ref[...]) * _softplus(small + dtb_ref[...])
    gcum_all = jnp.dot(_tril(c), g_all, preferred_element_type=F32, precision=HI)
    gcum_rows = _dot_nt(_eye(LANES), gcum_all, precision=HI)
    ri = lax.broadcasted_iota(jnp.int32, (c, c), 0)
    ci = lax.broadcasted_iota(jnp.int32, (c, c), 1)
    causal, strict = ri >= ci, ri > ci
    heads = range(nh)

    def conv(c0):
        return _silu(_conv_taps(xbuf_ref, convw_ref, c0, c, GDN_CONV))

    qs = [_l2norm(conv(h * dk)) * (dk ** -0.5) for h in heads]
    ks = [_l2norm(conv(width + h * dk)) for h in heads]
    vs = [conv(2 * width + h * dk) for h in heads]
    betas = [beta_all[:, h:h + 1] for h in heads]
    gcums = [gcum_all[:, nh + h:nh + h + 1] for h in heads]
    decays = [jnp.exp(jnp.where(causal, gcums[h] - gcum_rows[nh + h:nh + h + 1, :], -jnp.inf)) for h in heads]
    k_bs = [k.astype(BF16) for k in ks]
    q_bs = [q.astype(BF16) for q in qs]
    lowers = [jnp.where(strict, _dot_nt(k_bs[h], k_bs[h]) * decays[h] * betas[h], 0.0) for h in heads]
    intras = [jnp.where(causal, _dot_nt(q_bs[h], k_bs[h]) * decays[h], 0.0).astype(BF16) for h in heads]
    invs = _unit_lower_inverses(lowers)
    rhss = [jnp.concatenate([vs[h] * betas[h], ks[h] * (betas[h] * jnp.exp(gcums[h]))], axis=-1) for h in heads]
    sols = [_dot_split(_split(invs[h]), _split(rhss[h])) for h in heads]
    states = [state_ref[h] for h in heads]
    state_bs = [s.astype(BF16) for s in states]
    v_news = [sols[h][:, :dk] - _dot(sols[h][:, dk:].astype(BF16), state_bs[h]) for h in heads]
    v_new_bs = [v.astype(BF16) for v in v_news]
    outs = [_dot((qs[h] * jnp.exp(gcums[h])).astype(BF16), state_bs[h]) + _dot(intras[h], v_new_bs[h])
            for h in heads]
    for h in heads:
        g_last = gcums[h][c - 1:c, :]
        k_end = (ks[h] * jnp.exp(g_last - gcums[h])).astype(BF16)
        state_ref[h] = states[h] * jnp.exp(g_last) + _dot_tn(k_end, v_new_bs[h])
    for h in heads:
        gate = _silu(z_ref[0, :, h * dk:(h + 1) * dk])
        o_ref[0, :, h * dk:(h + 1) * dk] = (_rmsnorm(outs[h], normw_ref[...]) * gate).astype(o_ref.dtype)

    xbuf_ref[0:CARRY_ROWS, :] = xbuf_ref[c:c + CARRY_ROWS, :]


def _gdn_core(qkv, z, small, conv_w, a_log, dt_bias, norm_w):
    b, s, _ = qkv.shape
    c = GDN_CHUNK
    nh, dk = GDN_HEADS, GDN_HEAD_DIM
    width = nh * dk
    pad = lambda a: jnp.zeros((1, LANES), F32).at[0, nh:2 * nh].set(a.astype(F32))
    return pl.pallas_call(
        _gdn_body,
        grid=(b, s // c),
        in_specs=[pl.BlockSpec((1, c, 3 * width), lambda bi, i: (bi, i, 0)),
                  pl.BlockSpec((1, c, width), lambda bi, i: (bi, i, 0)),
                  pl.BlockSpec((1, c, LANES), lambda bi, i: (bi, i, 0)),
                  _resident((GDN_CONV, 3 * width)),
                  _resident((1, LANES)),
                  _resident((1, LANES)),
                  _resident((1, dk))],
        out_specs=pl.BlockSpec((1, c, width), lambda bi, i: (bi, i, 0)),
        out_shape=jax.ShapeDtypeStruct((b, s, width), BF16),
        scratch_shapes=[pltpu.VMEM((CARRY_ROWS + c, 3 * width), F32),
                        pltpu.VMEM((nh, dk, dk), F32)],
        compiler_params=_params(("parallel", "arbitrary")),
        name="gdn",
    )(qkv, z, small, conv_w.astype(F32), pad(a_log), pad(dt_bias), norm_w.reshape(1, dk).astype(F32))


def _ssd_body(z_ref, xbc_ref, dt_ref, convw_ref, convb_ref, dtb_ref, alog_ref, dskip_ref, normw_ref,
              o_ref, xbuf_ref, state_ref):
    c = SSD_CHUNK
    di, ng, p = SSD_D_INNER, SSD_GROUPS, SSD_HEAD_DIM
    heads_per_group = SSD_HEADS // ng
    pairs_per_group = heads_per_group // 2
    group_width = di // ng

    @pl.when(pl.program_id(1) == 0)
    def _():
        xbuf_ref[0:CARRY_ROWS, :] = jnp.zeros((CARRY_ROWS, xbuf_ref.shape[1]), F32)
        state_ref[...] = jnp.zeros_like(state_ref)

    xbuf_ref[CARRY_ROWS:CARRY_ROWS + c, :] = xbc_ref[0]

    def conv(c0):
        return _silu(_conv_taps(xbuf_ref, convw_ref, c0, c, SSD_CONV) + convb_ref[:, c0:c0 + LANES])

    dt_all = _softplus(dt_ref[0] + dtb_ref[...])
    adt = dt_all * -jnp.exp(alog_ref[...])
    acum_all = jnp.dot(_tril(c), adt, preferred_element_type=F32, precision=HI)
    acum_rows = _dot_nt(_eye(LANES), acum_all, precision=HI)
    from_start_all = jnp.exp(acum_all)
    to_end_all = jnp.exp(acum_all[c - 1:c, :] - acum_all)
    ri = lax.broadcasted_iota(jnp.int32, (c, c), 0)
    ci = lax.broadcasted_iota(jnp.int32, (c, c), 1)
    causal = ri >= ci
    low_half = lax.broadcasted_iota(jnp.int32, (c, LANES), 1) < p
    eye_b = _eye(LANES, BF16)
    groups, pairs = range(ng), range(SSD_HEADS // 2)
    group_of = [pr // pairs_per_group for pr in pairs]

    def per_head(all_heads, pr):
        return jnp.where(low_half, all_heads[:, 2 * pr:2 * pr + 1], all_heads[:, 2 * pr + 1:2 * pr + 2])

    def seg(hd, cb):
        diff = acum_all[:, hd:hd + 1] - acum_rows[hd:hd + 1, :]
        return (cb * jnp.exp(jnp.where(causal, diff, -jnp.inf))).astype(BF16)

    bms = [conv(di + g * SSD_STATE).astype(BF16) for g in groups]
    cms = [conv(di + ng * SSD_STATE + g * SSD_STATE).astype(BF16) for g in groups]
    cbs = [_dot_nt(cms[g], bms[g]) for g in groups]
    bm_ts = [_dot_nt(eye_b, bms[g]).astype(BF16) for g in groups]
    xs = [conv(pr * LANES) for pr in pairs]
    x_dts = [xs[pr] * per_head(dt_all, pr) for pr in pairs]
    x_dt_bs = [v.astype(BF16) for v in x_dts]
    y_diags = [jnp.where(low_half, _dot(seg(2 * pr, cbs[group_of[pr]]), x_dt_bs[pr]),
                         _dot(seg(2 * pr + 1, cbs[group_of[pr]]), x_dt_bs[pr])) for pr in pairs]
    states = [state_ref[pr] for pr in pairs]
    y_offs = [_dot(cms[group_of[pr]], states[pr].astype(BF16)) * per_head(from_start_all, pr) for pr in pairs]
    for pr in pairs:
        decay_all = jnp.where(low_half[0:1, :], from_start_all[c - 1:c, 2 * pr:2 * pr + 1],
                              from_start_all[c - 1:c, 2 * pr + 1:2 * pr + 2])
        state_ref[pr] = states[pr] * decay_all + _dot(bm_ts[group_of[pr]],
                                                      (x_dts[pr] * per_head(to_end_all, pr)).astype(BF16))
    ys = [(y_diags[pr] + y_offs[pr] + xs[pr] * dskip_ref[:, pr * LANES:(pr + 1) * LANES])
          * _silu(z_ref[0, :, pr * LANES:(pr + 1) * LANES]) for pr in pairs]
    sumsqs = [jnp.sum(y * y, axis=-1, keepdims=True) for y in ys]
    for g in groups:
        members = [pr for pr in pairs if group_of[pr] == g]
        scale = lax.rsqrt(functools.reduce(jnp.add, [sumsqs[pr] for pr in members]) * (1.0 / group_width) + NORM_EPS)
        for pr in members:
            c0 = pr * LANES
            o_ref[0, :, c0:c0 + LANES] = (ys[pr] * scale * normw_ref[:, c0:c0 + LANES]).astype(o_ref.dtype)

    xbuf_ref[0:CARRY_ROWS, :] = xbuf_ref[c:c + CARRY_ROWS, :]


def _ssd_core(z, xbc, dt_raw, conv_w, conv_b, dt_bias, a_log, d_skip, norm_w):
    b, s, conv_ch = xbc.shape
    c = SSD_CHUNK
    di = SSD_D_INNER
    pad = lambda a: jnp.zeros((1, LANES), F32).at[0, :SSD_HEADS].set(a.astype(F32))
    return pl.pallas_call(
        _ssd_body,
        grid=(b, s // c),
        in_specs=[pl.BlockSpec((1, c, di), lambda bi, i: (bi, i, 0)),
                  pl.BlockSpec((1, c, conv_ch), lambda bi, i: (bi, i, 0)),
                  pl.BlockSpec((1, c, LANES), lambda bi, i: (bi, i, 0)),
                  _resident((SSD_CONV, conv_ch)),
                  _resident((1, conv_ch)),
                  _resident((1, LANES)),
                  _resident((1, LANES)),
                  _resident((1, di)),
                  _resident((1, di))],
        out_specs=pl.BlockSpec((1, c, di), lambda bi, i: (bi, i, 0)),
        out_shape=jax.ShapeDtypeStruct((b, s, di), BF16),
        scratch_shapes=[pltpu.VMEM((CARRY_ROWS + c, conv_ch), F32),
                        pltpu.VMEM((SSD_HEADS // 2, SSD_STATE, LANES), F32)],
        compiler_params=_params(("parallel", "arbitrary")),
        name="ssd",
    )(z, xbc, dt_raw, conv_w.astype(F32), conv_b.reshape(1, conv_ch).astype(F32), pad(dt_bias), pad(a_log),
      jnp.repeat(d_skip.astype(F32), SSD_HEAD_DIM).reshape(1, di), norm_w.reshape(1, di).astype(F32))


def _pad_cols(w, n):
    return jnp.pad(w, ((0, 0), (0, n - w.shape[1])))


def _swa_mixer(x, b, s, gain, pos_b, freq, w_in, b_in, sinks, w_out, gain_post):
    nq, nkv = SWA_HEADS * HEAD_DIM, SWA_KV_HEADS * HEAD_DIM
    layout = (("q", 0, nq, 1), ("k", nq, nkv, 1), ("v", nq + nkv, nkv, 1))
    q, k, v = _proj_attn(x, b, s, gain, pos_b, freq, w_in, b_in, layout)
    o = _attention(q, k, v, n_heads=SWA_HEADS, kv_heads=SWA_KV_HEADS, n_back=SWA_WINDOW - 1, sinks=sinks)
    return _outproj(o.reshape(b * s, nq), x, gain_post, w_out)


def _dilated_mixer(x, b, s, gain, pos_b, freq, w_in, w_out, gain_post):
    nq, nkv = DIL_HEADS * HEAD_DIM, DIL_KV_HEADS * HEAD_DIM
    per_group = nq + 2 * nkv
    layout = []
    for gi, (_, dilation) in enumerate(DIL_PATTERN):
        base = gi * per_group
        layout += [("q", base, nq, dilation), ("k", base + nq, nkv, dilation), ("v", base + nq + nkv, nkv, dilation)]
    qkv = _proj_attn(x, b, s, gain, pos_b, freq, w_in, None, tuple(layout))
    group_outs, group_lses = [], []
    for gi, (window, dilation) in enumerate(DIL_PATTERN):
        o, lse = _attention(*qkv[3 * gi:3 * gi + 3], n_heads=DIL_HEADS, kv_heads=DIL_KV_HEADS,
                            n_back=window // dilation)
        group_outs.append(o)
        group_lses.append(lse)
    return _outproj_merge(group_outs, group_lses, x, b, s, gain_post, w_out, DIL_HEADS)


def _gdn_mixer(x, b, s, gain, w_in, conv_w, a_log, dt_bias, norm_w, w_out, gain_post):
    width = GDN_HEADS * GDN_HEAD_DIM
    w = jnp.concatenate([w_in[:, :4 * width], _pad_cols(w_in[:, 4 * width:], LANES)], axis=1)
    layout = ((0, 3 * width), (3 * width, width), (4 * width, LANES))
    qkv, z, small = _proj_plain(x, gain, w, layout)
    o = _gdn_core(qkv.reshape(b, s, 3 * width), z.reshape(b, s, width), small.reshape(b, s, LANES),
                  conv_w, a_log, dt_bias, norm_w)
    return _outproj(o.reshape(b * s, width), x, gain_post, w_out)


def _ssd_mixer(x, b, s, gain, w_in, conv_w, conv_b, dt_bias, a_log, d_skip, norm_w, w_out, gain_post):
    di = SSD_D_INNER
    conv_ch = di + 2 * SSD_GROUPS * SSD_STATE
    w = jnp.concatenate([w_in[:, :di + conv_ch], _pad_cols(w_in[:, di + conv_ch:], LANES)], axis=1)
    layout = ((0, di), (di, conv_ch), (di + conv_ch, LANES))
    z, xbc, dt_raw = _proj_plain(x, gain, w, layout)
    o = _ssd_core(z.reshape(b, s, di), xbc.reshape(b, s, conv_ch), dt_raw.reshape(b, s, LANES),
                  conv_w, conv_b, dt_bias, a_log, d_skip, norm_w)
    return _outproj(o.reshape(b * s, di), x, gain_post, w_out)


def kernel(x, positions, l0_norms, l0_ffn_w_in, l0_ffn_w_out, a_w_in, a_b_in, a_sinks, a_w_out, l1_norms, l1_ffn_w_in, l1_ffn_w_out, b_w_in, b_conv_w, b_A_log, b_dt_bias, b_norm, b_w_out, l2_norms, l2_ffn_w_in, l2_ffn_w_out, c_w_in, c_conv_w, c_conv_b, c_dt_bias, c_A_log, c_D, c_norm, c_w_out, l3_norms, l3_ffn_w_in, l3_ffn_w_out, d_w_in, d_w_out):
    b, s, d = x.shape
    t = b * s
    pos_b = jnp.broadcast_to(positions.astype(F32).reshape(t, 1), (t, LANES))
    inv_freq = ROPE_THETA ** (-jnp.arange(0, ROPE_DIMS, 2, dtype=F32) / ROPE_DIMS)
    freq = jnp.tile(inv_freq, LANES // inv_freq.shape[0]).reshape(1, LANES)

    mixers = (
        lambda h, n: _swa_mixer(h, b, s, n[2], pos_b, freq, a_w_in, a_b_in, a_sinks, a_w_out, n[3]),
        lambda h, n: _gdn_mixer(h, b, s, n[2], b_w_in, b_conv_w, b_A_log, b_dt_bias, b_norm, b_w_out, n[3]),
        lambda h, n: _ssd_mixer(h, b, s, n[2], c_w_in, c_conv_w, c_conv_b, c_dt_bias, c_A_log, c_D, c_norm,
                                c_w_out, n[3]),
        lambda h, n: _dilated_mixer(h, b, s, n[2], pos_b, freq, d_w_in, d_w_out, n[3]),
    )
    layers = ((l0_norms, l0_ffn_w_in, l0_ffn_w_out), (l1_norms, l1_ffn_w_in, l1_ffn_w_out),
              (l2_norms, l2_ffn_w_in, l2_ffn_w_out), (l3_norms, l3_ffn_w_in, l3_ffn_w_out))
    h = x.reshape(t, d)
    for i, (norms, ffn_w_in, ffn_w_out) in enumerate(layers):
        norms = norms.astype(F32)
        h = _ffn(h, norms[0:2], ffn_w_in[0], ffn_w_out[0])
        h = mixers[i % len(mixers)](h, norms)
        h = _ffn(h, norms[4:6], ffn_w_in[1], ffn_w_out[1])
    return h.reshape(b, s, d)
```

```python
import functools
import math

import jax
import jax.numpy as jnp
from jax import lax
from jax.experimental import pallas as pl
from jax.experimental.pallas import tpu as pltpu

F32 = jnp.float32
BF16 = jnp.bfloat16

D_MODEL = 1024
D_FF = 2816
NORM_EPS = 1e-6
ROPE_THETA = 500000.0
ROPE_DIMS = 16
HEAD_DIM = 64
BLOCK = 128
NEG_INF = -1e30

SWA_HEADS, SWA_KV_HEADS, SWA_WINDOW = 16, 2, 128
GDN_HEADS, GDN_HEAD_DIM, GDN_CONV, GDN_CHUNK = 8, 128, 4, 64
SSD_D_INNER, SSD_HEAD_DIM, SSD_GROUPS, SSD_STATE, SSD_CONV, SSD_CHUNK = 2048, 64, 4, 128, 4, 128
SSD_HEADS = SSD_D_INNER // SSD_HEAD_DIM
DIL_PATTERN = ((128, 1), (512, 4), (2048, 16))
DIL_HEADS, DIL_KV_HEADS = 16, 4

LANES = 128
CARRY_ROWS = 8
VMEM_LIMIT = 56 * 1024 * 1024
HI = lax.Precision.HIGHEST

TM_FFN = 512
TM_PROJ = 512
FF_CHUNK = 1536
ATTN_BLOCKS = 2


def _params(sem):
    return pltpu.CompilerParams(dimension_semantics=sem, vmem_limit_bytes=VMEM_LIMIT)


def _resident(shape):
    nd = len(shape)
    return pl.BlockSpec(shape, lambda *_: (0,) * nd, pipeline_mode=pl.Buffered(1))


def _rmsnorm(x, gain):
    return x * lax.rsqrt(jnp.mean(x * x, axis=-1, keepdims=True) + NORM_EPS) * gain


def _silu(x):
    half = 0.5 * x
    return half * (jnp.tanh(half) + 1.0)


def _softplus(x):
    return jnp.maximum(x, 0.0) + jnp.log(1.0 + jnp.exp(-jnp.abs(x)))


def _dot(a, b):
    return jnp.dot(a, b, preferred_element_type=F32)


def _dot_nt(a, b, precision=None):
    return lax.dot_general(a, b, (((1,), (1,)), ((), ())), preferred_element_type=F32, precision=precision)


def _dot_tn(a, b):
    return lax.dot_general(a, b, (((0,), (0,)), ((), ())), preferred_element_type=F32)


def _eye(n, dtype=F32):
    return (lax.broadcasted_iota(jnp.int32, (n, n), 0) == lax.broadcasted_iota(jnp.int32, (n, n), 1)).astype(dtype)


def _tril(n, dtype=F32):
    return (lax.broadcasted_iota(jnp.int32, (n, n), 0) >= lax.broadcasted_iota(jnp.int32, (n, n), 1)).astype(dtype)


def _ffn_value(x, g_ref, win_ref, wout_ref, act_ref):
    h = _rmsnorm(x, g_ref[0:1, :]).astype(BF16)
    for lo in range(0, D_FF, FF_CHUNK):
        width = min(FF_CHUNK, D_FF - lo)
        gate = _dot(h, win_ref[:, lo:lo + width])
        up = _dot(h, win_ref[:, D_FF + lo:D_FF + lo + width])
        act_ref[:, lo:lo + width] = (_silu(gate) * up).astype(BF16)
    y = _dot(act_ref[...], wout_ref[...])
    return x + 0.5 * _rmsnorm(y, g_ref[1:2, :])


def _ffn_specs():
    return [_resident((2, D_MODEL)), _resident((D_MODEL, 2 * D_FF)), _resident((D_FF, D_MODEL))]


def _ffn_body(x_ref, g_ref, win_ref, wout_ref, o_ref, act_ref):
    o_ref[...] = _ffn_value(x_ref[...], g_ref, win_ref, wout_ref, act_ref)


def _ffn(x, gains, w_in, w_out):
    t = x.shape[0]
    tm = min(TM_FFN, t)
    return pl.pallas_call(
        _ffn_body,
        grid=(t // tm,),
        in_specs=[pl.BlockSpec((tm, D_MODEL), lambda i: (i, 0))] + _ffn_specs(),
        out_specs=pl.BlockSpec((tm, D_MODEL), lambda i: (i, 0)),
        out_shape=jax.ShapeDtypeStruct((t, D_MODEL), F32),
        scratch_shapes=[pltpu.VMEM((tm, D_FF), BF16)],
        compiler_params=_params(("parallel",)),
        name="ffn",
    )(x, gains, w_in.astype(BF16), w_out.astype(BF16))


def _outproj_ffn_body(y_ref, x_ref, gp_ref, wo_ref, g_ref, win_ref, wout_ref, o_ref, act_ref):
    x = x_ref[...] + _rmsnorm(_dot(y_ref[...], wo_ref[...]), gp_ref[...])
    o_ref[...] = _ffn_value(x, g_ref, win_ref, wout_ref, act_ref)


def _outproj_ffn(y, x, gain, w_out, ffn_gains, ffn_w_in, ffn_w_out):
    t, k = y.shape
    tm = min(TM_FFN, t)
    return pl.pallas_call(
        _outproj_ffn_body,
        grid=(t // tm,),
        in_specs=[pl.BlockSpec((tm, k), lambda i: (i, 0)),
                  pl.BlockSpec((tm, D_MODEL), lambda i: (i, 0)),
                  _resident((1, D_MODEL)),
                  _resident((k, D_MODEL))] + _ffn_specs(),
        out_specs=pl.BlockSpec((tm, D_MODEL), lambda i: (i, 0)),
        out_shape=jax.ShapeDtypeStruct((t, D_MODEL), F32),
        scratch_shapes=[pltpu.VMEM((tm, D_FF), BF16)],
        compiler_params=_params(("parallel",)),
        name="outproj_ffn",
    )(y, x, gain.reshape(1, D_MODEL), w_out.astype(BF16), ffn_gains, ffn_w_in.astype(BF16), ffn_w_out.astype(BF16))


def _rope_tables(pos_ref, freq_ref):
    ang = pos_ref[...] * freq_ref[...]
    d = lax.broadcasted_iota(jnp.int32, ang.shape, 1) % HEAD_DIM
    half = ROPE_DIMS // 2
    cos, sin = jnp.cos(ang), jnp.sin(ang)
    c = jnp.where(d < ROPE_DIMS, cos, 1.0)
    s_lo = jnp.where(d < half, -sin, 0.0)
    s_hi = jnp.where((d >= half) & (d < ROPE_DIMS), sin, 0.0)
    return c, s_lo, s_hi


def _rope(y, tables):
    c, s_lo, s_hi = tables
    half = ROPE_DIMS // 2
    return y * c + pltpu.roll(y, LANES - half, 1) * s_lo + pltpu.roll(y, half, 1) * s_hi


PROJ_CHUNK = 512


def _proj_attn_body(*refs, layout, q_scale, has_bias):
    x_ref, g_ref, pos_ref, freq_ref, w_ref = refs[:5]
    refs = refs[5:]
    if has_bias:
        b_ref, refs = refs[0], refs[1:]
    out_refs, tmp_ref = refs[:len(layout)], refs[len(layout)]
    rows = x_ref.shape[0]
    h = _rmsnorm(x_ref[...], g_ref[...]).astype(BF16)
    k_tables = _rope_tables(pos_ref, freq_ref)
    q_tables = tuple(t * q_scale for t in k_tables)
    for (kind, start, width, dilation), o_ref in zip(layout, out_refs):
        chunk = min(width, PROJ_CHUNK)
        for lo in range(0, width, chunk):
            y = _dot(h, w_ref[:, start + lo:start + lo + chunk])
            if has_bias:
                y = y + b_ref[:, start + lo:start + lo + chunk]
            for sub in range(0, chunk, LANES):
                ys = y[:, sub:sub + LANES]
                if kind == "q":
                    ys = _rope(ys, q_tables)
                elif kind == "k":
                    ys = _rope(ys, k_tables)
                if dilation == 1:
                    o_ref[0, :, lo + sub:lo + sub + LANES] = ys.astype(o_ref.dtype)
                else:
                    tmp_ref[sub // LANES] = ys
            if dilation > 1:
                for sub in range(0, chunk, LANES):
                    for r in range(dilation):
                        o_ref[r, :, lo + sub:lo + sub + LANES] = tmp_ref[
                            sub // LANES, pl.ds(r, rows // dilation, stride=dilation), :].astype(o_ref.dtype)


def _proj_attn(x, b, s, gain, pos_b, freq, w_in, b_in, layout):
    n = w_in.shape[1]
    tm = min(TM_PROJ, s)
    steps = s // tm
    has_bias = b_in is not None
    body = functools.partial(_proj_attn_body, layout=layout, q_scale=HEAD_DIM ** -0.5, has_bias=has_bias)
    row = lambda width: pl.BlockSpec((tm, width), lambda bi, i: (bi * steps + i, 0))
    in_specs = [row(D_MODEL), _resident((1, D_MODEL)), row(LANES), _resident((1, LANES)), _resident((D_MODEL, n))]
    args = [x, gain.reshape(1, D_MODEL), pos_b, freq, w_in.astype(BF16)]
    if has_bias:
        in_specs.append(_resident((1, n)))
        args.append(b_in.reshape(1, n).astype(F32))
    return pl.pallas_call(
        body,
        grid=(b, steps),
        in_specs=in_specs,
        out_specs=[pl.BlockSpec((None, d, tm // d, w), lambda bi, i: (bi, 0, i, 0)) for _, _, w, d in layout],
        out_shape=[jax.ShapeDtypeStruct((b, d, s // d, w), BF16) for _, _, w, d in layout],
        scratch_shapes=[pltpu.VMEM((PROJ_CHUNK // LANES, tm, LANES), F32)],
        compiler_params=_params(("parallel", "parallel")),
        name="proj_attn",
    )(*args)


def _proj_plain_body(x_ref, g_ref, w_ref, *out_refs, layout):
    h = _rmsnorm(x_ref[...], g_ref[...]).astype(BF16)
    for (start, width), o_ref in zip(layout, out_refs):
        chunk = 512 if width % 512 == 0 else LANES
        for lo in range(0, width, chunk):
            o_ref[:, lo:lo + chunk] = _dot(h, w_ref[:, start + lo:start + lo + chunk])


def _proj_plain(x, gain, w_in, layout):
    t = x.shape[0]
    n = w_in.shape[1]
    tm = min(TM_PROJ, t)
    return pl.pallas_call(
        functools.partial(_proj_plain_body, layout=layout),
        grid=(t // tm,),
        in_specs=[pl.BlockSpec((tm, D_MODEL), lambda i: (i, 0)),
                  _resident((1, D_MODEL)),
                  _resident((D_MODEL, n))],
        out_specs=[pl.BlockSpec((tm, w), lambda i: (i, 0)) for _, w in layout],
        out_shape=[jax.ShapeDtypeStruct((t, w), F32) for _, w in layout],
        compiler_params=_params(("parallel",)),
        name="proj_plain",
    )(x, gain.reshape(1, D_MODEL), w_in.astype(BF16))


def _attn_body(*refs, n_heads, kv_heads, n_back, n_blocks, with_sink):
    if with_sink:
        sink_ref, q_ref, kc_ref, kp_ref, vc_ref, vp_ref, o_ref = refs
    else:
        q_ref, kc_ref, kp_ref, vc_ref, vp_ref, o_ref, lse_ref = refs
    gq = n_heads // kv_heads
    assert gq % 2 == 0 and kv_heads % 2 == 0
    kw = 2 * BLOCK
    step = pl.program_id(2)
    k_all = jnp.concatenate([kp_ref[...], kc_ref[...]], axis=0)
    v_all = jnp.concatenate([vp_ref[...], vc_ref[...]], axis=0)
    rows = lax.broadcasted_iota(jnp.int32, (BLOCK, 2 * kw), 0) + BLOCK
    cols = lax.broadcasted_iota(jnp.int32, (BLOCK, 2 * kw), 1) % kw
    band = (rows - cols >= 0) & (rows - cols <= n_back)
    lane = lax.broadcasted_iota(jnp.int32, (BLOCK, LANES), 1)
    low = lane < HEAD_DIM
    key_low = lax.broadcasted_iota(jnp.int32, (kw, LANES), 1) < HEAD_DIM
    zeros_b = jnp.zeros((kw, LANES), BF16)
    ones_low, ones_high = key_low.astype(BF16), jnp.logical_not(key_low).astype(BF16)
    for j in range(n_blocks):
        r0 = j * BLOCK
        has_prev = (step * n_blocks + j) > 0
        visible = band & ((cols >= BLOCK) | has_prev)
        keys, values = [], []
        for t in range(kv_heads // 2):
            k_tile = k_all[r0:r0 + kw, t * LANES:(t + 1) * LANES]
            v_tile = v_all[r0:r0 + kw, t * LANES:(t + 1) * LANES]
            k_swap, v_swap = pltpu.roll(k_tile, HEAD_DIM, 1), pltpu.roll(v_tile, HEAD_DIM, 1)
            for first in (True, False):
                k_dup = jnp.where(key_low == first, k_tile, k_swap)
                v_dup = jnp.where(key_low == first, v_tile, v_swap)
                keys.append(jnp.concatenate([jnp.where(key_low, k_dup, zeros_b),
                                             jnp.where(key_low, zeros_b, k_dup)], axis=0))
                values.append(jnp.concatenate(
                    [jnp.concatenate([jnp.where(key_low, v_dup, zeros_b), ones_low], axis=1),
                     jnp.concatenate([jnp.where(key_low, zeros_b, v_dup), ones_high], axis=1)], axis=0))
        pairs = range(n_heads // 2)
        kv_of = [2 * p // gq for p in pairs]
        scores = [jnp.where(visible, _dot_nt(q_ref[r0:r0 + BLOCK, p * LANES:(p + 1) * LANES], keys[kv_of[p]]), NEG_INF)
                  for p in pairs]
        max_a = [jnp.max(s[:, :kw], axis=-1, keepdims=True) for s in scores]
        max_b = [jnp.max(s[:, kw:], axis=-1, keepdims=True) for s in scores]
        probs = [jnp.concatenate([jnp.exp(s[:, :kw] - ma), jnp.exp(s[:, kw:] - mb)], axis=1).astype(BF16)
                 for s, ma, mb in zip(scores, max_a, max_b)]
        accs = [_dot(pr, values[kv_of[p]]) for p, pr in zip(pairs, probs)]
        lse_tile = jnp.zeros((BLOCK, LANES), F32)
        for p in pairs:
            acc, l = accs[p][:, :LANES], accs[p][:, LANES:]
            m = jnp.where(low, max_a[p], max_b[p])
            if with_sink:
                sink = jnp.where(low, sink_ref[2 * p], sink_ref[2 * p + 1])
                m_all = jnp.maximum(m, sink)
                rescale = jnp.exp(m - m_all)
                out = acc * (rescale / (l * rescale + jnp.exp(sink - m_all)))
            else:
                out = acc / l
                lse_tile = jnp.where((lane == 2 * p) | (lane == HEAD_DIM + 2 * p + 1), m + jnp.log(l), lse_tile)
            o_ref[r0:r0 + BLOCK, p * LANES:(p + 1) * LANES] = out.astype(o_ref.dtype)
        if not with_sink:
            lse_ref[r0:r0 + BLOCK, :] = lse_tile


def _lse_lane_of_head(h):
    return h if h % 2 == 0 else HEAD_DIM + h


def _attention(q, k, v, *, n_heads, kv_heads, n_back, sinks=None):
    b, d, length, qc = q.shape
    kc = k.shape[-1]
    assert length % BLOCK == 0
    n_blocks = ATTN_BLOCKS if length % (ATTN_BLOCKS * BLOCK) == 0 else 1
    tq = n_blocks * BLOCK
    with_sink = sinks is not None

    def cur(width):
        return pl.BlockSpec((None, None, tq, width), lambda bi, r, i: (bi, r, i, 0))

    def prev(width):
        return pl.BlockSpec((None, None, BLOCK, width), lambda bi, r, i: (bi, r, jnp.maximum(i * n_blocks - 1, 0), 0))

    in_specs = [cur(qc), cur(kc), prev(kc), cur(kc), prev(kc)]
    args = [q, k, k, v, v]
    if with_sink:
        in_specs = [pl.BlockSpec(memory_space=pltpu.SMEM)] + in_specs
        args = [sinks.astype(F32)] + args
        out_specs = cur(qc)
        out_shape = jax.ShapeDtypeStruct((b, d, length, qc), BF16)
    else:
        out_specs = [cur(qc), cur(LANES)]
        out_shape = [jax.ShapeDtypeStruct((b, d, length, qc), BF16),
                     jax.ShapeDtypeStruct((b, d, length, LANES), F32)]
    body = functools.partial(_attn_body, n_heads=n_heads, kv_heads=kv_heads, n_back=n_back,
                             n_blocks=n_blocks, with_sink=with_sink)
    return pl.pallas_call(
        body,
        grid=(b, d, length // tq),
        in_specs=in_specs,
        out_specs=out_specs,
        out_shape=out_shape,
        compiler_params=_params(("parallel", "parallel", "parallel")),
        name="attn_sink" if with_sink else "attn_lse",
    )(*args)


def _outproj_merge_body(*refs, dilations, n_heads):
    n_groups = len(dilations)
    o_refs, lse_refs = refs[:n_groups], refs[n_groups:2 * n_groups]
    x_ref, g_ref, w_ref, fg_ref, win_ref, wout_ref, out_ref, o_nat_ref, lse_nat_ref, act_ref = refs[2 * n_groups:]
    rows = x_ref.shape[0]

    def natural(ref, scratch, d):
        if d == 1:
            return ref[0].astype(F32)
        slabs = scratch.shape[0]
        for r in range(d):
            for c in range(slabs):
                scratch[c, pl.ds(r, rows // d, stride=d), :] = ref[r, :, c * LANES:(c + 1) * LANES].astype(F32)
        return jnp.concatenate([scratch[c] for c in range(slabs)], axis=1) if slabs > 1 else scratch[0]

    lses = [natural(lse_refs[g], lse_nat_ref.at[g:g + 1], d) for g, d in enumerate(dilations)]
    top = functools.reduce(jnp.maximum, lses)
    exps = [jnp.exp(s - top) for s in lses]
    den = functools.reduce(jnp.add, exps)
    n = o_refs[0].shape[-1]
    lane = lax.broadcasted_iota(jnp.int32, (LANES, n), 0)
    head_of_col = lax.broadcasted_iota(jnp.int32, (LANES, n), 1) // HEAD_DIM
    expand = functools.reduce(jnp.logical_or, [(lane == _lse_lane_of_head(h)) & (head_of_col == h)
                                               for h in range(n_heads)]).astype(BF16)
    merged = None
    for e, o_ref, d in zip(exps, o_refs, dilations):
        w_hi, w_lo = _split(e / den)
        term = (_dot(w_hi, expand) + _dot(w_lo, expand)) * natural(o_ref, o_nat_ref, d)
        merged = term if merged is None else merged + term
    x = x_ref[...] + _rmsnorm(_dot(merged.astype(BF16), w_ref[...]), g_ref[...])
    out_ref[...] = _ffn_value(x, fg_ref, win_ref, wout_ref, act_ref)


def _outproj_merge_ffn(outs, lses, x, b, s, gain, w_out, n_heads, ffn_gains, ffn_w_in, ffn_w_out):
    k = outs[0].shape[-1]
    dilations = tuple(o.shape[1] for o in outs)
    tm = min(TM_FFN, s)
    steps = s // tm
    grouped = lambda d, width: pl.BlockSpec((None, d, tm // d, width), lambda bi, i: (bi, 0, i, 0))
    row = pl.BlockSpec((tm, D_MODEL), lambda bi, i: (bi * steps + i, 0))
    return pl.pallas_call(
        functools.partial(_outproj_merge_body, dilations=dilations, n_heads=n_heads),
        grid=(b, steps),
        in_specs=[grouped(d, k) for d in dilations] + [grouped(d, LANES) for d in dilations]
                 + [row, _resident((1, D_MODEL)), _resident((k, D_MODEL))] + _ffn_specs(),
        out_specs=row,
        out_shape=jax.ShapeDtypeStruct((b * s, D_MODEL), F32),
        scratch_shapes=[pltpu.VMEM((k // LANES, tm, LANES), F32), pltpu.VMEM((len(dilations), tm, LANES), F32),
                        pltpu.VMEM((tm, D_FF), BF16)],
        compiler_params=_params(("parallel", "parallel")),
        name="outproj_merge_ffn",
    )(*outs, *lses, x, gain.reshape(1, D_MODEL), w_out.astype(BF16), ffn_gains, ffn_w_in.astype(BF16),
      ffn_w_out.astype(BF16))


def _conv_taps(xbuf_ref, w_ref, c0, rows, width):
    acc = None
    for i in range(width):
        start = CARRY_ROWS - (width - 1) + i
        term = xbuf_ref[start:start + rows, c0:c0 + LANES] * w_ref[i:i + 1, c0:c0 + LANES]
        acc = term if acc is None else acc + term
    return acc


def _l2norm(x):
    return x * lax.rsqrt(jnp.sum(x * x, axis=-1, keepdims=True) + NORM_EPS)


def _split(a):
    hi = a.astype(BF16)
    return hi, (a - hi.astype(F32)).astype(BF16)


def _dot_split(a, b):
    (ah, al), (bh, bl) = a, b
    return _dot(ah, bh) + (_dot(ah, bl) + _dot(al, bh))


def _unit_lower_inverses(strict_lowers):
    n = strict_lowers[0].shape[0]
    ri = lax.broadcasted_iota(jnp.int32, (n, n), 0)
    ci = lax.broadcasted_iota(jnp.int32, (n, n), 1)
    invs = None
    for level in range(int(math.log2(n))):
        joins = ((ri >> (level + 1)) == (ci >> (level + 1))) & ((ri >> level) != (ci >> level))
        couplings = [jnp.where(joins, l, 0.0) for l in strict_lowers]
        if invs is None:
            invs = [_eye(n) - c for c in couplings]
            continue
        inv_splits = [_split(inv) for inv in invs]
        right = [_dot_split(_split(c), d) for c, d in zip(couplings, inv_splits)]
        invs = [inv - _dot_split(d, _split(r)) for inv, d, r in zip(invs, inv_splits, right)]
    return invs


def _gdn_body(qkv_ref, z_ref, small_ref, convw_ref, alog_ref, dtb_ref, normw_ref, o_ref, xbuf_ref, state_ref):
    c = GDN_CHUNK
    nh, dk = GDN_HEADS, GDN_HEAD_DIM
    width = nh * dk

    @pl.when(pl.program_id(1) == 0)
    def _():
        xbuf_ref[0:CARRY_ROWS, :] = jnp.zeros((CARRY_ROWS, 3 * width), F32)
        state_ref[...] = jnp.zeros_like(state_ref)

    xbuf_ref[CARRY_ROWS:CARRY_ROWS + c, :] = qkv_ref[0]

    small = small_ref[0]
    beta_all = jax.nn.sigmoid(small)
    g_all = -jnp.exp(alog_ref[...]) * _softplus(small + dtb_ref[...])
    gcum_all = jnp.dot(_tril(c), g_all, preferred_element_type=F32, precision=HI)
    gcum_rows = _dot_nt(_eye(LANES), gcum_all, precision=HI)
    ri = lax.broadcasted_iota(jnp.int32, (c, c), 0)
    ci = lax.broadcasted_iota(jnp.int32, (c, c), 1)
    causal, strict = ri >= ci, ri > ci
    heads = range(nh)

    def conv(c0):
        return _silu(_conv_taps(xbuf_ref, convw_ref, c0, c, GDN_CONV))

    qs = [_l2norm(conv(h * dk)) * (dk ** -0.5) for h in heads]
    ks = [_l2norm(conv(width + h * dk)) for h in heads]
    vs = [conv(2 * width + h * dk) for h in heads]
    betas = [beta_all[:, h:h + 1] for h in heads]
    gcums = [gcum_all[:, nh + h:nh + h + 1] for h in heads]
    decays = [jnp.exp(jnp.where(causal, gcums[h] - gcum_rows[nh + h:nh + h + 1, :], -jnp.inf)) for h in heads]
    k_bs = [k.astype(BF16) for k in ks]
    q_bs = [q.astype(BF16) for q in qs]
    lowers = [jnp.where(strict, _dot_nt(k_bs[h], k_bs[h]) * decays[h] * betas[h], 0.0) for h in heads]
    intras = [jnp.where(causal, _dot_nt(q_bs[h], k_bs[h]) * decays[h], 0.0).astype(BF16) for h in heads]
    invs = _unit_lower_inverses(lowers)
    rhss = [jnp.concatenate([vs[h] * betas[h], ks[h] * (betas[h] * jnp.exp(gcums[h]))], axis=-1) for h in heads]
    sols = [_dot_split(_split(invs[h]), _split(rhss[h])) for h in heads]
    states = [state_ref[h] for h in heads]
    state_bs = [s.astype(BF16) for s in states]
    v_news = [sols[h][:, :dk] - _dot(sols[h][:, dk:].astype(BF16), state_bs[h]) for h in heads]
    v_new_bs = [v.astype(BF16) for v in v_news]
    outs = [_dot((qs[h] * jnp.exp(gcums[h])).astype(BF16), state_bs[h]) + _dot(intras[h], v_new_bs[h])
            for h in heads]
    for h in heads:
        g_last = gcums[h][c - 1:c, :]
        k_end = (ks[h] * jnp.exp(g_last - gcums[h])).astype(BF16)
        state_ref[h] = states[h] * jnp.exp(g_last) + _dot_tn(k_end, v_new_bs[h])
    for h in heads:
        gate = _silu(z_ref[0, :, h * dk:(h + 1) * dk])
        o_ref[0, :, h * dk:(h + 1) * dk] = (_rmsnorm(outs[h], normw_ref[...]) * gate).astype(o_ref.dtype)

    xbuf_ref[0:CARRY_ROWS, :] = xbuf_ref[c:c + CARRY_ROWS, :]


def _gdn_core(qkv, z, small, conv_w, a_log, dt_bias, norm_w):
    b, s, _ = qkv.shape
    c = GDN_CHUNK
    nh, dk = GDN_HEADS, GDN_HEAD_DIM
    width = nh * dk
    pad = lambda a: jnp.zeros((1, LANES), F32).at[0, nh:2 * nh].set(a.astype(F32))
    return pl.pallas_call(
        _gdn_body,
        grid=(b, s // c),
        in_specs=[pl.BlockSpec((1, c, 3 * width), lambda bi, i: (bi, i, 0)),
                  pl.BlockSpec((1, c, width), lambda bi, i: (bi, i, 0)),
                  pl.BlockSpec((1, c, LANES), lambda bi, i: (bi, i, 0)),
                  _resident((GDN_CONV, 3 * width)),
                  _resident((1, LANES)),
                  _resident((1, LANES)),
                  _resident((1, dk))],
        out_specs=pl.BlockSpec((1, c, width), lambda bi, i: (bi, i, 0)),
        out_shape=jax.ShapeDtypeStruct((b, s, width), BF16),
        scratch_shapes=[pltpu.VMEM((CARRY_ROWS + c, 3 * width), F32),
                        pltpu.VMEM((nh, dk, dk), F32)],
        compiler_params=_params(("parallel", "arbitrary")),
        name="gdn",
    )(qkv, z, small, conv_w.astype(F32), pad(a_log), pad(dt_bias), norm_w.reshape(1, dk).astype(F32))


def _ssd_body(z_ref, xbc_ref, dt_ref, convw_ref, convb_ref, dtb_ref, alog_ref, dskip_ref, normw_ref,
              o_ref, xbuf_ref, state_ref):
    c = SSD_CHUNK
    di, ng, p = SSD_D_INNER, SSD_GROUPS, SSD_HEAD_DIM
    heads_per_group = SSD_HEADS // ng
    pairs_per_group = heads_per_group // 2
    group_width = di // ng

    @pl.when(pl.program_id(1) == 0)
    def _():
        xbuf_ref[0:CARRY_ROWS, :] = jnp.zeros((CARRY_ROWS, xbuf_ref.shape[1]), F32)
        state_ref[...] = jnp.zeros_like(state_ref)

    xbuf_ref[CARRY_ROWS:CARRY_ROWS + c, :] = xbc_ref[0]

    def conv(c0):
        return _silu(_conv_taps(xbuf_ref, convw_ref, c0, c, SSD_CONV) + convb_ref[:, c0:c0 + LANES])

    dt_all = _softplus(dt_ref[0] + dtb_ref[...])
    adt = dt_all * -jnp.exp(alog_ref[...])
    acum_all = jnp.dot(_tril(c), adt, preferred_element_type=F32, precision=HI)
    acum_rows = _dot_nt(_eye(LANES), acum_all, precision=HI)
    from_start_all = jnp.exp(acum_all)
    to_end_all = jnp.exp(acum_all[c - 1:c, :] - acum_all)
    ri = lax.broadcasted_iota(jnp.int32, (c, c), 0)
    ci = lax.broadcasted_iota(jnp.int32, (c, c), 1)
    causal = ri >= ci
    low_half = lax.broadcasted_iota(jnp.int32, (c, LANES), 1) < p
    eye_b = _eye(LANES, BF16)
    groups, pairs = range(ng), range(SSD_HEADS // 2)
    group_of = [pr // pairs_per_group for pr in pairs]

    def per_head(all_heads, pr):
        return jnp.where(low_half, all_heads[:, 2 * pr:2 * pr + 1], all_heads[:, 2 * pr + 1:2 * pr + 2])

    def seg(hd, cb):
        diff = acum_all[:, hd:hd + 1] - acum_rows[hd:hd + 1, :]
        return (cb * jnp.exp(jnp.where(causal, diff, -jnp.inf))).astype(BF16)

    bms = [conv(di + g * SSD_STATE).astype(BF16) for g in groups]
    cms = [conv(di + ng * SSD_STATE + g * SSD_STATE).astype(BF16) for g in groups]
    cbs = [_dot_nt(cms[g], bms[g]) for g in groups]
    bm_ts = [_dot_nt(eye_b, bms[g]).astype(BF16) for g in groups]
    xs = [conv(pr * LANES) for pr in pairs]
    x_dts = [xs[pr] * per_head(dt_all, pr) for pr in pairs]
    x_dt_bs = [v.astype(BF16) for v in x_dts]
    y_diags = [jnp.where(low_half, _dot(seg(2 * pr, cbs[group_of[pr]]), x_dt_bs[pr]),
                         _dot(seg(2 * pr + 1, cbs[group_of[pr]]), x_dt_bs[pr])) for pr in pairs]
    states = [state_ref[pr] for pr in pairs]
    y_offs = [_dot(cms[group_of[pr]], states[pr].astype(BF16)) * per_head(from_start_all, pr) for pr in pairs]
    for pr in pairs:
        decay_all = jnp.where(low_half[0:1, :], from_start_all[c - 1:c, 2 * pr:2 * pr + 1],
                              from_start_all[c - 1:c, 2 * pr + 1:2 * pr + 2])
        state_ref[pr] = states[pr] * decay_all + _dot(bm_ts[group_of[pr]],
                                                      (x_dts[pr] * per_head(to_end_all, pr)).astype(BF16))
    ys = [(y_diags[pr] + y_offs[pr] + xs[pr] * dskip_ref[:, pr * LANES:(pr + 1) * LANES])
          * _silu(z_ref[0, :, pr * LANES:(pr + 1) * LANES]) for pr in pairs]
    sumsqs = [jnp.sum(y * y, axis=-1, keepdims=True) for y in ys]
    for g in groups:
        members = [pr for pr in pairs if group_of[pr] == g]
        scale = lax.rsqrt(functools.reduce(jnp.add, [sumsqs[pr] for pr in members]) * (1.0 / group_width) + NORM_EPS)
        for pr in members:
            c0 = pr * LANES
            o_ref[0, :, c0:c0 + LANES] = (ys[pr] * scale * normw_ref[:, c0:c0 + LANES]).astype(o_ref.dtype)

    xbuf_ref[0:CARRY_ROWS, :] = xbuf_ref[c:c + CARRY_ROWS, :]


def _ssd_core(z, xbc, dt_raw, conv_w, conv_b, dt_bias, a_log, d_skip, norm_w):
    b, s, conv_ch = xbc.shape
    c = SSD_CHUNK
    di = SSD_D_INNER
    pad = lambda a: jnp.zeros((1, LANES), F32).at[0, :SSD_HEADS].set(a.astype(F32))
    return pl.pallas_call(
        _ssd_body,
        grid=(b, s // c),
        in_specs=[pl.BlockSpec((1, c, di), lambda bi, i: (bi, i, 0)),
                  pl.BlockSpec((1, c, conv_ch), lambda bi, i: (bi, i, 0)),
                  pl.BlockSpec((1, c, LANES), lambda bi, i: (bi, i, 0)),
                  _resident((SSD_CONV, conv_ch)),
                  _resident((1, conv_ch)),
                  _resident((1, LANES)),
                  _resident((1, LANES)),
                  _resident((1, di)),
                  _resident((1, di))],
        out_specs=pl.BlockSpec((1, c, di), lambda bi, i: (bi, i, 0)),
        out_shape=jax.ShapeDtypeStruct((b, s, di), BF16),
        scratch_shapes=[pltpu.VMEM((CARRY_ROWS + c, conv_ch), F32),
                        pltpu.VMEM((SSD_HEADS // 2, SSD_STATE, LANES), F32)],
        compiler_params=_params(("parallel", "arbitrary")),
        name="ssd",
    )(z, xbc, dt_raw, conv_w.astype(F32), conv_b.reshape(1, conv_ch).astype(F32), pad(dt_bias), pad(a_log),
      jnp.repeat(d_skip.astype(F32), SSD_HEAD_DIM).reshape(1, di), norm_w.reshape(1, di).astype(F32))


def _pad_cols(w, n):
    return jnp.pad(w, ((0, 0), (0, n - w.shape[1])))


def _swa_mixer(x, b, s, gain, pos_b, freq, w_in, b_in, sinks, w_out, gain_post, ffn):
    nq, nkv = SWA_HEADS * HEAD_DIM, SWA_KV_HEADS * HEAD_DIM
    layout = (("q", 0, nq, 1), ("k", nq, nkv, 1), ("v", nq + nkv, nkv, 1))
    q, k, v = _proj_attn(x, b, s, gain, pos_b, freq, w_in, b_in, layout)
    o = _attention(q, k, v, n_heads=SWA_HEADS, kv_heads=SWA_KV_HEADS, n_back=SWA_WINDOW - 1, sinks=sinks)
    return _outproj_ffn(o.reshape(b * s, nq), x, gain_post, w_out, *ffn)


def _dilated_mixer(x, b, s, gain, pos_b, freq, w_in, w_out, gain_post, ffn):
    nq, nkv = DIL_HEADS * HEAD_DIM, DIL_KV_HEADS * HEAD_DIM
    per_group = nq + 2 * nkv
    layout = []
    for gi, (_, dilation) in enumerate(DIL_PATTERN):
        base = gi * per_group
        layout += [("q", base, nq, dilation), ("k", base + nq, nkv, dilation), ("v", base + nq + nkv, nkv, dilation)]
    qkv = _proj_attn(x, b, s, gain, pos_b, freq, w_in, None, tuple(layout))
    group_outs, group_lses = [], []
    for gi, (window, dilation) in enumerate(DIL_PATTERN):
        o, lse = _attention(*qkv[3 * gi:3 * gi + 3], n_heads=DIL_HEADS, kv_heads=DIL_KV_HEADS,
                            n_back=window // dilation)
        group_outs.append(o)
        group_lses.append(lse)
    return _outproj_merge_ffn(group_outs, group_lses, x, b, s, gain_post, w_out, DIL_HEADS, *ffn)


def _gdn_mixer(x, b, s, gain, w_in, conv_w, a_log, dt_bias, norm_w, w_out, gain_post, ffn):
    width = GDN_HEADS * GDN_HEAD_DIM
    w = jnp.concatenate([w_in[:, :4 * width], _pad_cols(w_in[:, 4 * width:], LANES)], axis=1)
    layout = ((0, 3 * width), (3 * width, width), (4 * width, LANES))
    qkv, z, small = _proj_plain(x, gain, w, layout)
    o = _gdn_core(qkv.reshape(b, s, 3 * width), z.reshape(b, s, width), small.reshape(b, s, LANES),
                  conv_w, a_log, dt_bias, norm_w)
    return _outproj_ffn(o.reshape(b * s, width), x, gain_post, w_out, *ffn)


def _ssd_mixer(x, b, s, gain, w_in, conv_w, conv_b, dt_bias, a_log, d_skip, norm_w, w_out, gain_post, ffn):
    di = SSD_D_INNER
    conv_ch = di + 2 * SSD_GROUPS * SSD_STATE
    w = jnp.concatenate([w_in[:, :di + conv_ch], _pad_cols(w_in[:, di + conv_ch:], LANES)], axis=1)
    layout = ((0, di), (di, conv_ch), (di + conv_ch, LANES))
    z, xbc, dt_raw = _proj_plain(x, gain, w, layout)
    o = _ssd_core(z.reshape(b, s, di), xbc.reshape(b, s, conv_ch), dt_raw.reshape(b, s, LANES),
                  conv_w, conv_b, dt_bias, a_log, d_skip, norm_w)
    return _outproj_ffn(o.reshape(b * s, di), x, gain_post, w_out, *ffn)


def kernel(x, positions, l0_norms, l0_ffn_w_in, l0_ffn_w_out, a_w_in, a_b_in, a_sinks, a_w_out, l1_norms, l1_ffn_w_in, l1_ffn_w_out, b_w_in, b_conv_w, b_A_log, b_dt_bias, b_norm, b_w_out, l2_norms, l2_ffn_w_in, l2_ffn_w_out, c_w_in, c_conv_w, c_conv_b, c_dt_bias, c_A_log, c_D, c_norm, c_w_out, l3_norms, l3_ffn_w_in, l3_ffn_w_out, d_w_in, d_w_out):
    b, s, d = x.shape
    t = b * s
    pos_b = jnp.broadcast_to(positions.astype(F32).reshape(t, 1), (t, LANES))
    inv_freq = ROPE_THETA ** (-jnp.arange(0, ROPE_DIMS, 2, dtype=F32) / ROPE_DIMS)
    freq = jnp.tile(inv_freq, LANES // inv_freq.shape[0]).reshape(1, LANES)

    mixers = (
        lambda h, n, f: _swa_mixer(h, b, s, n[2], pos_b, freq, a_w_in, a_b_in, a_sinks, a_w_out, n[3], f),
        lambda h, n, f: _gdn_mixer(h, b, s, n[2], b_w_in, b_conv_w, b_A_log, b_dt_bias, b_norm, b_w_out, n[3], f),
        lambda h, n, f: _ssd_mixer(h, b, s, n[2], c_w_in, c_conv_w, c_conv_b, c_dt_bias, c_A_log, c_D, c_norm,
                                   c_w_out, n[3], f),
        lambda h, n, f: _dilated_mixer(h, b, s, n[2], pos_b, freq, d_w_in, d_w_out, n[3], f),
    )
    layers = ((l0_norms, l0_ffn_w_in, l0_ffn_w_out), (l1_norms, l1_ffn_w_in, l1_ffn_w_out),
              (l2_norms, l2_ffn_w_in, l2_ffn_w_out), (l3_norms, l3_ffn_w_in, l3_ffn_w_out))
    h = x.reshape(t, d)
    for i, (norms, ffn_w_in, ffn_w_out) in enumerate(layers):
        norms = norms.astype(F32)
        h = _ffn(h, norms[0:2], ffn_w_in[0], ffn_w_out[0])
        h = mixers[i % len(mixers)](h, norms, (norms[4:6], ffn_w_in[1], ffn_w_out[1]))
    return h.reshape(b, s, d)
```

```python
import functools
import math

import jax
import jax.numpy as jnp
from jax import lax
from jax.experimental import pallas as pl
from jax.experimental.pallas import tpu as pltpu

F32 = jnp.float32
BF16 = jnp.bfloat16

D_MODEL = 1024
D_FF = 2816
NORM_EPS = 1e-6
ROPE_THETA = 500000.0
ROPE_DIMS = 16
HEAD_DIM = 64
BLOCK = 128
NEG_INF = -1e30

SWA_HEADS, SWA_KV_HEADS, SWA_WINDOW = 16, 2, 128
GDN_HEADS, GDN_HEAD_DIM, GDN_CONV, GDN_CHUNK = 8, 128, 4, 64
SSD_D_INNER, SSD_HEAD_DIM, SSD_GROUPS, SSD_STATE, SSD_CONV, SSD_CHUNK = 2048, 64, 4, 128, 4, 128
SSD_HEADS = SSD_D_INNER // SSD_HEAD_DIM
DIL_PATTERN = ((128, 1), (512, 4), (2048, 16))
DIL_HEADS, DIL_KV_HEADS = 16, 4

LANES = 128
CARRY_ROWS = 8
VMEM_LIMIT = 56 * 1024 * 1024
HI = lax.Precision.HIGHEST

TM_FFN = 512
TM_PROJ = 512
FF_CHUNK = 1536
ATTN_BLOCKS = 2


def _params(sem):
    return pltpu.CompilerParams(dimension_semantics=sem, vmem_limit_bytes=VMEM_LIMIT)


def _resident(shape):
    nd = len(shape)
    return pl.BlockSpec(shape, lambda *_: (0,) * nd, pipeline_mode=pl.Buffered(1))


def _rmsnorm(x, gain):
    return x * lax.rsqrt(jnp.mean(x * x, axis=-1, keepdims=True) + NORM_EPS) * gain


def _silu(x):
    half = 0.5 * x
    return half * (jnp.tanh(half) + 1.0)


def _softplus(x):
    return jnp.maximum(x, 0.0) + jnp.log(1.0 + jnp.exp(-jnp.abs(x)))


def _dot(a, b):
    return jnp.dot(a, b, preferred_element_type=F32)


def _dot_nt(a, b, precision=None):
    return lax.dot_general(a, b, (((1,), (1,)), ((), ())), preferred_element_type=F32, precision=precision)


def _dot_tn(a, b):
    return lax.dot_general(a, b, (((0,), (0,)), ((), ())), preferred_element_type=F32)


def _eye(n, dtype=F32):
    return (lax.broadcasted_iota(jnp.int32, (n, n), 0) == lax.broadcasted_iota(jnp.int32, (n, n), 1)).astype(dtype)


def _tril(n, dtype=F32):
    return (lax.broadcasted_iota(jnp.int32, (n, n), 0) >= lax.broadcasted_iota(jnp.int32, (n, n), 1)).astype(dtype)


def _ffn_value(x, g_ref, win_ref, wout_ref, act_ref):
    h = _rmsnorm(x, g_ref[0:1, :]).astype(BF16)
    for lo in range(0, D_FF, FF_CHUNK):
        width = min(FF_CHUNK, D_FF - lo)
        gate = _dot(h, win_ref[:, lo:lo + width])
        up = _dot(h, win_ref[:, D_FF + lo:D_FF + lo + width])
        act_ref[:, lo:lo + width] = (_silu(gate) * up).astype(BF16)
    y = _dot(act_ref[...], wout_ref[...])
    return x + 0.5 * _rmsnorm(y, g_ref[1:2, :])


def _ffn_specs():
    return [_resident((2, D_MODEL)), _resident((D_MODEL, 2 * D_FF)), _resident((D_FF, D_MODEL))]


def _ffn_body(x_ref, g_ref, win_ref, wout_ref, o_ref, act_ref):
    o_ref[...] = _ffn_value(x_ref[...], g_ref, win_ref, wout_ref, act_ref)


def _ffn(x, gains, w_in, w_out):
    t = x.shape[0]
    tm = min(TM_FFN, t)
    return pl.pallas_call(
        _ffn_body,
        grid=(t // tm,),
        in_specs=[pl.BlockSpec((tm, D_MODEL), lambda i: (i, 0))] + _ffn_specs(),
        out_specs=pl.BlockSpec((tm, D_MODEL), lambda i: (i, 0)),
        out_shape=jax.ShapeDtypeStruct((t, D_MODEL), F32),
        scratch_shapes=[pltpu.VMEM((tm, D_FF), BF16)],
        compiler_params=_params(("parallel",)),
        name="ffn",
    )(x, gains, w_in.astype(BF16), w_out.astype(BF16))


def _outproj_ffn_body(y_ref, x_ref, gp_ref, wo_ref, g_ref, win_ref, wout_ref, o_ref, act_ref):
    x = x_ref[...] + _rmsnorm(_dot(y_ref[...], wo_ref[...]), gp_ref[...])
    o_ref[...] = _ffn_value(x, g_ref, win_ref, wout_ref, act_ref)


def _outproj_ffn(y, x, gain, w_out, ffn_gains, ffn_w_in, ffn_w_out):
    t, k = y.shape
    tm = min(TM_FFN, t)
    return pl.pallas_call(
        _outproj_ffn_body,
        grid=(t // tm,),
        in_specs=[pl.BlockSpec((tm, k), lambda i: (i, 0)),
                  pl.BlockSpec((tm, D_MODEL), lambda i: (i, 0)),
                  _resident((1, D_MODEL)),
                  _resident((k, D_MODEL))] + _ffn_specs(),
        out_specs=pl.BlockSpec((tm, D_MODEL), lambda i: (i, 0)),
        out_shape=jax.ShapeDtypeStruct((t, D_MODEL), F32),
        scratch_shapes=[pltpu.VMEM((tm, D_FF), BF16)],
        compiler_params=_params(("parallel",)),
        name="outproj_ffn",
    )(y, x, gain.reshape(1, D_MODEL), w_out.astype(BF16), ffn_gains, ffn_w_in.astype(BF16), ffn_w_out.astype(BF16))


def _rope_tables(pos_ref, freq_ref):
    ang = pos_ref[...] * freq_ref[...]
    d = lax.broadcasted_iota(jnp.int32, ang.shape, 1) % HEAD_DIM
    half = ROPE_DIMS // 2
    cos, sin = jnp.cos(ang), jnp.sin(ang)
    c = jnp.where(d < ROPE_DIMS, cos, 1.0)
    s_lo = jnp.where(d < half, -sin, 0.0)
    s_hi = jnp.where((d >= half) & (d < ROPE_DIMS), sin, 0.0)
    return c, s_lo, s_hi


def _rope(y, tables):
    c, s_lo, s_hi = tables
    half = ROPE_DIMS // 2
    return y * c + pltpu.roll(y, LANES - half, 1) * s_lo + pltpu.roll(y, half, 1) * s_hi


PROJ_CHUNK = 512
CONV_CHUNK = 256


def _proj_attn_body(*refs, layout, q_scale, has_bias):
    x_ref, g_ref, pos_ref, freq_ref, w_ref = refs[:5]
    refs = refs[5:]
    if has_bias:
        b_ref, refs = refs[0], refs[1:]
    out_refs, tmp_ref = refs[:len(layout)], refs[len(layout)]
    rows = x_ref.shape[0]
    h = _rmsnorm(x_ref[...], g_ref[...]).astype(BF16)
    k_tables = _rope_tables(pos_ref, freq_ref)
    q_tables = tuple(t * q_scale for t in k_tables)
    for (kind, start, width, dilation), o_ref in zip(layout, out_refs):
        chunk = min(width, PROJ_CHUNK)
        for lo in range(0, width, chunk):
            y = _dot(h, w_ref[:, start + lo:start + lo + chunk])
            if has_bias:
                y = y + b_ref[:, start + lo:start + lo + chunk]
            for sub in range(0, chunk, LANES):
                ys = y[:, sub:sub + LANES]
                if kind == "q":
                    ys = _rope(ys, q_tables)
                elif kind == "k":
                    ys = _rope(ys, k_tables)
                if dilation == 1:
                    o_ref[0, :, lo + sub:lo + sub + LANES] = ys.astype(o_ref.dtype)
                else:
                    tmp_ref[sub // LANES] = ys
            if dilation > 1:
                for sub in range(0, chunk, LANES):
                    for r in range(dilation):
                        o_ref[r, :, lo + sub:lo + sub + LANES] = tmp_ref[
                            sub // LANES, pl.ds(r, rows // dilation, stride=dilation), :].astype(o_ref.dtype)


def _proj_attn(x, b, s, gain, pos_b, freq, w_in, b_in, layout):
    n = w_in.shape[1]
    tm = min(TM_PROJ, s)
    steps = s // tm
    has_bias = b_in is not None
    body = functools.partial(_proj_attn_body, layout=layout, q_scale=HEAD_DIM ** -0.5, has_bias=has_bias)
    row = lambda width: pl.BlockSpec((tm, width), lambda bi, i: (bi * steps + i, 0))
    in_specs = [row(D_MODEL), _resident((1, D_MODEL)), row(LANES), _resident((1, LANES)), _resident((D_MODEL, n))]
    args = [x, gain.reshape(1, D_MODEL), pos_b, freq, w_in.astype(BF16)]
    if has_bias:
        in_specs.append(_resident((1, n)))
        args.append(b_in.reshape(1, n).astype(F32))
    return pl.pallas_call(
        body,
        grid=(b, steps),
        in_specs=in_specs,
        out_specs=[pl.BlockSpec((None, d, tm // d, w), lambda bi, i: (bi, 0, i, 0)) for _, _, w, d in layout],
        out_shape=[jax.ShapeDtypeStruct((b, d, s // d, w), BF16) for _, _, w, d in layout],
        scratch_shapes=[pltpu.VMEM((PROJ_CHUNK // LANES, tm, LANES), F32)],
        compiler_params=_params(("parallel", "parallel")),
        name="proj_attn",
    )(*args)


def _proj_conv_body(*refs, layout, conv_width, has_bias):
    x_ref, g_ref, w_ref, convw_ref = refs[:4]
    refs = refs[4:]
    if has_bias:
        convb_ref, refs = refs[0], refs[1:]
    out_refs, (carry_ref, buf_ref) = refs[:len(layout)], refs[len(layout):]
    rows = x_ref.shape[0]

    @pl.when(pl.program_id(1) == 0)
    def _():
        carry_ref[...] = jnp.zeros_like(carry_ref)

    h = _rmsnorm(x_ref[...], g_ref[...]).astype(BF16)
    slot = 0
    for (start, width, conv_col), o_ref in zip(layout, out_refs):
        chunk = PROJ_CHUNK if width % PROJ_CHUNK == 0 else LANES
        if conv_col is not None:
            chunk = CONV_CHUNK
        for lo in range(0, width, chunk):
            y = _dot(h, w_ref[:, start + lo:start + lo + chunk])
            if conv_col is None:
                o_ref[:, lo:lo + chunk] = y
                continue
            cols = slice(conv_col + lo, conv_col + lo + chunk)
            buf = buf_ref.at[slot % buf_ref.shape[0]]
            slot += 1
            buf[0:CARRY_ROWS, 0:chunk] = carry_ref[:, cols]
            buf[CARRY_ROWS:CARRY_ROWS + rows, 0:chunk] = y
            carry_ref[:, cols] = buf[rows:rows + CARRY_ROWS, 0:chunk]
            acc = None
            for i in range(conv_width):
                first = CARRY_ROWS - (conv_width - 1) + i
                term = buf[first:first + rows, 0:chunk] * convw_ref[i:i + 1, cols]
                acc = term if acc is None else acc + term
            if has_bias:
                acc = acc + convb_ref[:, cols]
            o_ref[:, lo:lo + chunk] = _silu(acc)


def _proj_conv(x, b, s, gain, w_in, conv_w, conv_b, layout):
    n = w_in.shape[1]
    conv_ch = conv_w.shape[1]
    tm = min(TM_PROJ, s)
    steps = s // tm
    has_bias = conv_b is not None
    row = lambda width: pl.BlockSpec((tm, width), lambda bi, i: (bi * steps + i, 0))
    in_specs = [row(D_MODEL), _resident((1, D_MODEL)), _resident((D_MODEL, n)), _resident(conv_w.shape)]
    args = [x, gain.reshape(1, D_MODEL), w_in.astype(BF16), conv_w.astype(F32)]
    if has_bias:
        in_specs.append(_resident((1, conv_ch)))
        args.append(conv_b.reshape(1, conv_ch).astype(F32))
    return pl.pallas_call(
        functools.partial(_proj_conv_body, layout=layout, conv_width=conv_w.shape[0], has_bias=has_bias),
        grid=(b, steps),
        in_specs=in_specs,
        out_specs=[row(w) for _, w, _ in layout],
        out_shape=[jax.ShapeDtypeStruct((b * s, w), F32) for _, w, _ in layout],
        scratch_shapes=[pltpu.VMEM((CARRY_ROWS, conv_ch), F32),
                        pltpu.VMEM((2, CARRY_ROWS + tm, PROJ_CHUNK), F32)],
        compiler_params=_params(("parallel", "arbitrary")),
        name="proj_conv",
    )(*args)


def _attn_body(*refs, n_heads, kv_heads, n_back, n_blocks, with_sink):
    if with_sink:
        sink_ref, q_ref, kc_ref, kp_ref, vc_ref, vp_ref, o_ref = refs
    else:
        q_ref, kc_ref, kp_ref, vc_ref, vp_ref, o_ref, lse_ref = refs
    gq = n_heads // kv_heads
    assert gq % 2 == 0 and kv_heads % 2 == 0
    kw = 2 * BLOCK
    step = pl.program_id(2)
    k_all = jnp.concatenate([kp_ref[...], kc_ref[...]], axis=0)
    v_all = jnp.concatenate([vp_ref[...], vc_ref[...]], axis=0)
    rows = lax.broadcasted_iota(jnp.int32, (BLOCK, 2 * kw), 0) + BLOCK
    cols = lax.broadcasted_iota(jnp.int32, (BLOCK, 2 * kw), 1) % kw
    band = (rows - cols >= 0) & (rows - cols <= n_back)
    lane = lax.broadcasted_iota(jnp.int32, (BLOCK, LANES), 1)
    low = lane < HEAD_DIM
    key_low = lax.broadcasted_iota(jnp.int32, (kw, LANES), 1) < HEAD_DIM
    zeros_b = jnp.zeros((kw, LANES), BF16)
    ones_low, ones_high = key_low.astype(BF16), jnp.logical_not(key_low).astype(BF16)
    for j in range(n_blocks):
        r0 = j * BLOCK
        has_prev = (step * n_blocks + j) > 0
        visible = band & ((cols >= BLOCK) | has_prev)
        keys, values = [], []
        for t in range(kv_heads // 2):
            k_tile = k_all[r0:r0 + kw, t * LANES:(t + 1) * LANES]
            v_tile = v_all[r0:r0 + kw, t * LANES:(t + 1) * LANES]
            k_swap, v_swap = pltpu.roll(k_tile, HEAD_DIM, 1), pltpu.roll(v_tile, HEAD_DIM, 1)
            for first in (True, False):
                k_dup = jnp.where(key_low == first, k_tile, k_swap)
                v_dup = jnp.where(key_low == first, v_tile, v_swap)
                keys.append(jnp.concatenate([jnp.where(key_low, k_dup, zeros_b),
                                             jnp.where(key_low, zeros_b, k_dup)], axis=0))
                values.append(jnp.concatenate(
                    [jnp.concatenate([jnp.where(key_low, v_dup, zeros_b), ones_low], axis=1),
                     jnp.concatenate([jnp.where(key_low, zeros_b, v_dup), ones_high], axis=1)], axis=0))
        pairs = range(n_heads // 2)
        kv_of = [2 * p // gq for p in pairs]
        scores = [jnp.where(visible, _dot_nt(q_ref[r0:r0 + BLOCK, p * LANES:(p + 1) * LANES], keys[kv_of[p]]), NEG_INF)
                  for p in pairs]
        max_a = [jnp.max(s[:, :kw], axis=-1, keepdims=True) for s in scores]
        max_b = [jnp.max(s[:, kw:], axis=-1, keepdims=True) for s in scores]
        probs = [jnp.concatenate([jnp.exp(s[:, :kw] - ma), jnp.exp(s[:, kw:] - mb)], axis=1).astype(BF16)
                 for s, ma, mb in zip(scores, max_a, max_b)]
        accs = [_dot(pr, values[kv_of[p]]) for p, pr in zip(pairs, probs)]
        lse_tile = jnp.zeros((BLOCK, LANES), F32)
        for p in pairs:
            acc, l = accs[p][:, :LANES], accs[p][:, LANES:]
            m = jnp.where(low, max_a[p], max_b[p])
            if with_sink:
                sink = jnp.where(low, sink_ref[2 * p], sink_ref[2 * p + 1])
                m_all = jnp.maximum(m, sink)
                rescale = jnp.exp(m - m_all)
                out = acc * (rescale / (l * rescale + jnp.exp(sink - m_all)))
            else:
                out = acc / l
                lse_tile = jnp.where((lane == 2 * p) | (lane == HEAD_DIM + 2 * p + 1), m + jnp.log(l), lse_tile)
            o_ref[r0:r0 + BLOCK, p * LANES:(p + 1) * LANES] = out.astype(o_ref.dtype)
        if not with_sink:
            lse_ref[r0:r0 + BLOCK, :] = lse_tile


def _lse_lane_of_head(h):
    return h if h % 2 == 0 else HEAD_DIM + h


def _attention(q, k, v, *, n_heads, kv_heads, n_back, sinks=None):
    b, d, length, qc = q.shape
    kc = k.shape[-1]
    assert length % BLOCK == 0
    n_blocks = ATTN_BLOCKS if length % (ATTN_BLOCKS * BLOCK) == 0 else 1
    tq = n_blocks * BLOCK
    with_sink = sinks is not None

    def cur(width):
        return pl.BlockSpec((None, None, tq, width), lambda bi, r, i: (bi, r, i, 0))

    def prev(width):
        return pl.BlockSpec((None, None, BLOCK, width), lambda bi, r, i: (bi, r, jnp.maximum(i * n_blocks - 1, 0), 0))

    in_specs = [cur(qc), cur(kc), prev(kc), cur(kc), prev(kc)]
    args = [q, k, k, v, v]
    if with_sink:
        in_specs = [pl.BlockSpec(memory_space=pltpu.SMEM)] + in_specs
        args = [sinks.astype(F32)] + args
        out_specs = cur(qc)
        out_shape = jax.ShapeDtypeStruct((b, d, length, qc), BF16)
    else:
        out_specs = [cur(qc), cur(LANES)]
        out_shape = [jax.ShapeDtypeStruct((b, d, length, qc), BF16),
                     jax.ShapeDtypeStruct((b, d, length, LANES), F32)]
    body = functools.partial(_attn_body, n_heads=n_heads, kv_heads=kv_heads, n_back=n_back,
                             n_blocks=n_blocks, with_sink=with_sink)
    return pl.pallas_call(
        body,
        grid=(b, d, length // tq),
        in_specs=in_specs,
        out_specs=out_specs,
        out_shape=out_shape,
        compiler_params=_params(("parallel", "parallel", "parallel")),
        name="attn_sink" if with_sink else "attn_lse",
    )(*args)


def _outproj_merge_body(*refs, dilations, n_heads):
    n_groups = len(dilations)
    o_refs, lse_refs = refs[:n_groups], refs[n_groups:2 * n_groups]
    x_ref, g_ref, w_ref, fg_ref, win_ref, wout_ref, out_ref, o_nat_ref, lse_nat_ref, act_ref = refs[2 * n_groups:]
    rows = x_ref.shape[0]

    def natural(ref, scratch, d):
        if d == 1:
            return ref[0].astype(F32)
        slabs = scratch.shape[0]
        for r in range(d):
            for c in range(slabs):
                scratch[c, pl.ds(r, rows // d, stride=d), :] = ref[r, :, c * LANES:(c + 1) * LANES].astype(F32)
        return jnp.concatenate([scratch[c] for c in range(slabs)], axis=1) if slabs > 1 else scratch[0]

    lses = [natural(lse_refs[g], lse_nat_ref.at[g:g + 1], d) for g, d in enumerate(dilations)]
    top = functools.reduce(jnp.maximum, lses)
    exps = [jnp.exp(s - top) for s in lses]
    den = functools.reduce(jnp.add, exps)
    n = o_refs[0].shape[-1]
    lane = lax.broadcasted_iota(jnp.int32, (LANES, n), 0)
    head_of_col = lax.broadcasted_iota(jnp.int32, (LANES, n), 1) // HEAD_DIM
    expand = functools.reduce(jnp.logical_or, [(lane == _lse_lane_of_head(h)) & (head_of_col == h)
                                               for h in range(n_heads)]).astype(BF16)
    merged = None
    for e, o_ref, d in zip(exps, o_refs, dilations):
        w_hi, w_lo = _split(e / den)
        term = (_dot(w_hi, expand) + _dot(w_lo, expand)) * natural(o_ref, o_nat_ref, d)
        merged = term if merged is None else merged + term
    x = x_ref[...] + _rmsnorm(_dot(merged.astype(BF16), w_ref[...]), g_ref[...])
    out_ref[...] = _ffn_value(x, fg_ref, win_ref, wout_ref, act_ref)


def _outproj_merge_ffn(outs, lses, x, b, s, gain, w_out, n_heads, ffn_gains, ffn_w_in, ffn_w_out):
    k = outs[0].shape[-1]
    dilations = tuple(o.shape[1] for o in outs)
    tm = min(TM_FFN, s)
    steps = s // tm
    grouped = lambda d, width: pl.BlockSpec((None, d, tm // d, width), lambda bi, i: (bi, 0, i, 0))
    row = pl.BlockSpec((tm, D_MODEL), lambda bi, i: (bi * steps + i, 0))
    return pl.pallas_call(
        functools.partial(_outproj_merge_body, dilations=dilations, n_heads=n_heads),
        grid=(b, steps),
        in_specs=[grouped(d, k) for d in dilations] + [grouped(d, LANES) for d in dilations]
                 + [row, _resident((1, D_MODEL)), _resident((k, D_MODEL))] + _ffn_specs(),
        out_specs=row,
        out_shape=jax.ShapeDtypeStruct((b * s, D_MODEL), F32),
        scratch_shapes=[pltpu.VMEM((k // LANES, tm, LANES), F32), pltpu.VMEM((len(dilations), tm, LANES), F32),
                        pltpu.VMEM((tm, D_FF), BF16)],
        compiler_params=_params(("parallel", "parallel")),
        name="outproj_merge_ffn",
    )(*outs, *lses, x, gain.reshape(1, D_MODEL), w_out.astype(BF16), ffn_gains, ffn_w_in.astype(BF16),
      ffn_w_out.astype(BF16))


def _l2norm(x):
    return x * lax.rsqrt(jnp.sum(x * x, axis=-1, keepdims=True) + NORM_EPS)


def _split(a):
    hi = a.astype(BF16)
    return hi, (a - hi.astype(F32)).astype(BF16)


def _dot_split(a, b):
    (ah, al), (bh, bl) = a, b
    return _dot(ah, bh) + (_dot(ah, bl) + _dot(al, bh))


def _unit_lower_inverses(strict_lowers):
    n = strict_lowers[0].shape[0]
    ri = lax.broadcasted_iota(jnp.int32, (n, n), 0)
    ci = lax.broadcasted_iota(jnp.int32, (n, n), 1)
    invs = None
    for level in range(int(math.log2(n))):
        joins = ((ri >> (level + 1)) == (ci >> (level + 1))) & ((ri >> level) != (ci >> level))
        couplings = [jnp.where(joins, l, 0.0) for l in strict_lowers]
        if invs is None:
            invs = [_eye(n) - c for c in couplings]
            continue
        inv_splits = [_split(inv) for inv in invs]
        right = [_dot_split(_split(c), d) for c, d in zip(couplings, inv_splits)]
        invs = [inv - _dot_split(d, _split(r)) for inv, d, r in zip(invs, inv_splits, right)]
    return invs


def _gdn_body(qkv_ref, z_ref, small_ref, alog_ref, dtb_ref, normw_ref, o_ref, state_ref):
    c = GDN_CHUNK
    nh, dk = GDN_HEADS, GDN_HEAD_DIM
    width = nh * dk
    rows = qkv_ref.shape[1]
    chunks = range(rows // c)

    @pl.when(pl.program_id(1) == 0)
    def _():
        state_ref[...] = jnp.zeros_like(state_ref)

    small = small_ref[0]
    beta_all = jax.nn.sigmoid(small)
    g_all = -jnp.exp(alog_ref[...]) * _softplus(small + dtb_ref[...])
    tril = _tril(c)
    gcum_alls = [jnp.dot(tril, g_all[j * c:(j + 1) * c], preferred_element_type=F32, precision=HI)
                 for j in chunks]
    gcum_rows = [_dot_nt(_eye(LANES), g, precision=HI) for g in gcum_alls]
    ri = lax.broadcasted_iota(jnp.int32, (c, c), 0)
    ci = lax.broadcasted_iota(jnp.int32, (c, c), 1)
    causal, strict = ri >= ci, ri > ci
    units = [(j, h) for j in chunks for h in range(nh)]

    def act(c0, j):
        return qkv_ref[0, j * c:(j + 1) * c, c0:c0 + dk]

    qs = [_l2norm(act(h * dk, j)) * (dk ** -0.5) for j, h in units]
    ks = [_l2norm(act(width + h * dk, j)) for j, h in units]
    vs = [act(2 * width + h * dk, j) for j, h in units]
    betas = [beta_all[j * c:(j + 1) * c, h:h + 1] for j, h in units]
    gcums = [gcum_alls[j][:, nh + h:nh + h + 1] for j, h in units]
    decays = [jnp.exp(jnp.where(causal, gcums[u] - gcum_rows[j][nh + h:nh + h + 1, :], -jnp.inf))
              for u, (j, h) in enumerate(units)]
    idx = range(len(units))
    k_bs = [k.astype(BF16) for k in ks]
    q_bs = [q.astype(BF16) for q in qs]
    lowers = [jnp.where(strict, _dot_nt(k_bs[u], k_bs[u]) * decays[u] * betas[u], 0.0) for u in idx]
    intras = [jnp.where(causal, _dot_nt(q_bs[u], k_bs[u]) * decays[u], 0.0).astype(BF16) for u in idx]
    invs = _unit_lower_inverses(lowers)
    rhss = [jnp.concatenate([vs[u] * betas[u], ks[u] * (betas[u] * jnp.exp(gcums[u]))], axis=-1) for u in idx]
    sols = [_dot_split(_split(invs[u]), _split(rhss[u])) for u in idx]
    us = [sol[:, :dk] for sol in sols]
    w_bs = [sol[:, dk:].astype(BF16) for sol in sols]
    q_ins = [(qs[u] * jnp.exp(gcums[u])).astype(BF16) for u in idx]
    g_lasts = [g[c - 1:c, :] for g in gcums]
    k_ends = [(ks[u] * jnp.exp(g_lasts[u] - gcums[u])).astype(BF16) for u in idx]
    states = [state_ref[h] for h in range(nh)]
    for j in chunks:
        state_bs = [s.astype(BF16) for s in states]
        here = [j * nh + h for h in range(nh)]
        v_new_bs = [(us[u] - _dot(w_bs[u], state_bs[h])).astype(BF16) for h, u in enumerate(here)]
        outs = [_dot(q_ins[u], state_bs[h]) + _dot(intras[u], v_new_bs[h]) for h, u in enumerate(here)]
        states = [states[h] * jnp.exp(g_lasts[u]) + _dot_tn(k_ends[u], v_new_bs[h]) for h, u in enumerate(here)]
        for h in range(nh):
            gate = _silu(z_ref[0, j * c:(j + 1) * c, h * dk:(h + 1) * dk])
            o_ref[0, j * c:(j + 1) * c, h * dk:(h + 1) * dk] = (
                _rmsnorm(outs[h], normw_ref[...]) * gate).astype(o_ref.dtype)
    for h in range(nh):
        state_ref[h] = states[h]


GDN_CHUNKS_PER_STEP = 4


def _gdn_core(qkv, z, small, a_log, dt_bias, norm_w):
    b, s, _ = qkv.shape
    nh, dk = GDN_HEADS, GDN_HEAD_DIM
    width = nh * dk
    rows = GDN_CHUNKS_PER_STEP * GDN_CHUNK
    assert s % rows == 0
    pad = lambda a: jnp.zeros((1, LANES), F32).at[0, nh:2 * nh].set(a.astype(F32))
    return pl.pallas_call(
        _gdn_body,
        grid=(b, s // rows),
        in_specs=[pl.BlockSpec((1, rows, 3 * width), lambda bi, i: (bi, i, 0)),
                  pl.BlockSpec((1, rows, width), lambda bi, i: (bi, i, 0)),
                  pl.BlockSpec((1, rows, LANES), lambda bi, i: (bi, i, 0)),
                  _resident((1, LANES)),
                  _resident((1, LANES)),
                  _resident((1, dk))],
        out_specs=pl.BlockSpec((1, rows, width), lambda bi, i: (bi, i, 0)),
        out_shape=jax.ShapeDtypeStruct((b, s, width), BF16),
        scratch_shapes=[pltpu.VMEM((nh, dk, dk), F32)],
        compiler_params=_params(("parallel", "arbitrary")),
        name="gdn",
    )(qkv, z, small, pad(a_log), pad(dt_bias), norm_w.reshape(1, dk).astype(F32))


def _ssd_body(z_ref, xbc_ref, dt_ref, dtb_ref, alog_ref, dskip_ref, normw_ref, o_ref, state_ref):
    c = SSD_CHUNK
    di, ng, p = SSD_D_INNER, SSD_GROUPS, SSD_HEAD_DIM
    heads_per_group = SSD_HEADS // ng
    pairs_per_group = heads_per_group // 2
    group_width = di // ng

    @pl.when(pl.program_id(1) == 0)
    def _():
        state_ref[...] = jnp.zeros_like(state_ref)

    def conv(c0):
        return xbc_ref[0, :, c0:c0 + LANES]

    dt_all = _softplus(dt_ref[0] + dtb_ref[...])
    adt = dt_all * -jnp.exp(alog_ref[...])
    acum_all = jnp.dot(_tril(c), adt, preferred_element_type=F32, precision=HI)
    acum_rows = _dot_nt(_eye(LANES), acum_all, precision=HI)
    from_start_all = jnp.exp(acum_all)
    to_end_all = jnp.exp(acum_all[c - 1:c, :] - acum_all)
    ri = lax.broadcasted_iota(jnp.int32, (c, c), 0)
    ci = lax.broadcasted_iota(jnp.int32, (c, c), 1)
    causal = ri >= ci
    low_half = lax.broadcasted_iota(jnp.int32, (c, LANES), 1) < p
    eye_b = _eye(LANES, BF16)
    groups, pairs = range(ng), range(SSD_HEADS // 2)
    group_of = [pr // pairs_per_group for pr in pairs]

    def per_head(all_heads, pr):
        return jnp.where(low_half, all_heads[:, 2 * pr:2 * pr + 1], all_heads[:, 2 * pr + 1:2 * pr + 2])

    def seg(hd, cb):
        diff = acum_all[:, hd:hd + 1] - acum_rows[hd:hd + 1, :]
        return (cb * jnp.exp(jnp.where(causal, diff, -jnp.inf))).astype(BF16)

    bms = [conv(di + g * SSD_STATE).astype(BF16) for g in groups]
    cms = [conv(di + ng * SSD_STATE + g * SSD_STATE).astype(BF16) for g in groups]
    cbs = [_dot_nt(cms[g], bms[g]) for g in groups]
    bm_ts = [_dot_nt(eye_b, bms[g]).astype(BF16) for g in groups]
    xs = [conv(pr * LANES) for pr in pairs]
    x_dts = [xs[pr] * per_head(dt_all, pr) for pr in pairs]
    x_dt_bs = [v.astype(BF16) for v in x_dts]
    y_diags = [jnp.where(low_half, _dot(seg(2 * pr, cbs[group_of[pr]]), x_dt_bs[pr]),
                         _dot(seg(2 * pr + 1, cbs[group_of[pr]]), x_dt_bs[pr])) for pr in pairs]
    states = [state_ref[pr] for pr in pairs]
    y_offs = [_dot(cms[group_of[pr]], states[pr].astype(BF16)) * per_head(from_start_all, pr) for pr in pairs]
    for pr in pairs:
        decay_all = jnp.where(low_half[0:1, :], from_start_all[c - 1:c, 2 * pr:2 * pr + 1],
                              from_start_all[c - 1:c, 2 * pr + 1:2 * pr + 2])
        state_ref[pr] = states[pr] * decay_all + _dot(bm_ts[group_of[pr]],
                                                      (x_dts[pr] * per_head(to_end_all, pr)).astype(BF16))
    ys = [(y_diags[pr] + y_offs[pr] + xs[pr] * dskip_ref[:, pr * LANES:(pr + 1) * LANES])
          * _silu(z_ref[0, :, pr * LANES:(pr + 1) * LANES]) for pr in pairs]
    sumsqs = [jnp.sum(y * y, axis=-1, keepdims=True) for y in ys]
    for g in groups:
        members = [pr for pr in pairs if group_of[pr] == g]
        scale = lax.rsqrt(functools.reduce(jnp.add, [sumsqs[pr] for pr in members]) * (1.0 / group_width) + NORM_EPS)
        for pr in members:
            c0 = pr * LANES
            o_ref[0, :, c0:c0 + LANES] = (ys[pr] * scale * normw_ref[:, c0:c0 + LANES]).astype(o_ref.dtype)


def _ssd_core(z, xbc, dt_raw, dt_bias, a_log, d_skip, norm_w):
    b, s, conv_ch = xbc.shape
    c = SSD_CHUNK
    di = SSD_D_INNER
    pad = lambda a: jnp.zeros((1, LANES), F32).at[0, :SSD_HEADS].set(a.astype(F32))
    return pl.pallas_call(
        _ssd_body,
        grid=(b, s // c),
        in_specs=[pl.BlockSpec((1, c, di), lambda bi, i: (bi, i, 0)),
                  pl.BlockSpec((1, c, conv_ch), lambda bi, i: (bi, i, 0)),
                  pl.BlockSpec((1, c, LANES), lambda bi, i: (bi, i, 0)),
                  _resident((1, LANES)),
                  _resident((1, LANES)),
                  _resident((1, di)),
                  _resident((1, di))],
        out_specs=pl.BlockSpec((1, c, di), lambda bi, i: (bi, i, 0)),
        out_shape=jax.ShapeDtypeStruct((b, s, di), BF16),
        scratch_shapes=[pltpu.VMEM((SSD_HEADS // 2, SSD_STATE, LANES), F32)],
        compiler_params=_params(("parallel", "arbitrary")),
        name="ssd",
    )(z, xbc, dt_raw, pad(dt_bias), pad(a_log),
      jnp.repeat(d_skip.astype(F32), SSD_HEAD_DIM).reshape(1, di), norm_w.reshape(1, di).astype(F32))


def _pad_cols(w, n):
    return jnp.pad(w, ((0, 0), (0, n - w.shape[1])))


def _swa_mixer(x, b, s, gain, pos_b, freq, w_in, b_in, sinks, w_out, gain_post, ffn):
    nq, nkv = SWA_HEADS * HEAD_DIM, SWA_KV_HEADS * HEAD_DIM
    layout = (("q", 0, nq, 1), ("k", nq, nkv, 1), ("v", nq + nkv, nkv, 1))
    q, k, v = _proj_attn(x, b, s, gain, pos_b, freq, w_in, b_in, layout)
    o = _attention(q, k, v, n_heads=SWA_HEADS, kv_heads=SWA_KV_HEADS, n_back=SWA_WINDOW - 1, sinks=sinks)
    return _outproj_ffn(o.reshape(b * s, nq), x, gain_post, w_out, *ffn)


def _dilated_mixer(x, b, s, gain, pos_b, freq, w_in, w_out, gain_post, ffn):
    nq, nkv = DIL_HEADS * HEAD_DIM, DIL_KV_HEADS * HEAD_DIM
    per_group = nq + 2 * nkv
    layout = []
    for gi, (_, dilation) in enumerate(DIL_PATTERN):
        base = gi * per_group
        layout += [("q", base, nq, dilation), ("k", base + nq, nkv, dilation), ("v", base + nq + nkv, nkv, dilation)]
    qkv = _proj_attn(x, b, s, gain, pos_b, freq, w_in, None, tuple(layout))
    group_outs, group_lses = [], []
    for gi, (window, dilation) in enumerate(DIL_PATTERN):
        o, lse = _attention(*qkv[3 * gi:3 * gi + 3], n_heads=DIL_HEADS, kv_heads=DIL_KV_HEADS,
                            n_back=window // dilation)
        group_outs.append(o)
        group_lses.append(lse)
    return _outproj_merge_ffn(group_outs, group_lses, x, b, s, gain_post, w_out, DIL_HEADS, *ffn)


def _gdn_mixer(x, b, s, gain, w_in, conv_w, a_log, dt_bias, norm_w, w_out, gain_post, ffn):
    width = GDN_HEADS * GDN_HEAD_DIM
    w = jnp.concatenate([w_in[:, :4 * width], _pad_cols(w_in[:, 4 * width:], LANES)], axis=1)
    layout = ((0, 3 * width, 0), (3 * width, width, None), (4 * width, LANES, None))
    qkv, z, small = _proj_conv(x, b, s, gain, w, conv_w, None, layout)
    o = _gdn_core(qkv.reshape(b, s, 3 * width), z.reshape(b, s, width), small.reshape(b, s, LANES),
                  a_log, dt_bias, norm_w)
    return _outproj_ffn(o.reshape(b * s, width), x, gain_post, w_out, *ffn)


def _ssd_mixer(x, b, s, gain, w_in, conv_w, conv_b, dt_bias, a_log, d_skip, norm_w, w_out, gain_post, ffn):
    di = SSD_D_INNER
    conv_ch = di + 2 * SSD_GROUPS * SSD_STATE
    w = jnp.concatenate([w_in[:, :di + conv_ch], _pad_cols(w_in[:, di + conv_ch:], LANES)], axis=1)
    layout = ((0, di, None), (di, conv_ch, 0), (di + conv_ch, LANES, None))
    z, xbc, dt_raw = _proj_conv(x, b, s, gain, w, conv_w, conv_b, layout)
    o = _ssd_core(z.reshape(b, s, di), xbc.reshape(b, s, conv_ch), dt_raw.reshape(b, s, LANES),
                  dt_bias, a_log, d_skip, norm_w)
    return _outproj_ffn(o.reshape(b * s, di), x, gain_post, w_out, *ffn)


def kernel(x, positions, l0_norms, l0_ffn_w_in, l0_ffn_w_out, a_w_in, a_b_in, a_sinks, a_w_out, l1_norms, l1_ffn_w_in, l1_ffn_w_out, b_w_in, b_conv_w, b_A_log, b_dt_bias, b_norm, b_w_out, l2_norms, l2_ffn_w_in, l2_ffn_w_out, c_w_in, c_conv_w, c_conv_b, c_dt_bias, c_A_log, c_D, c_norm, c_w_out, l3_norms, l3_ffn_w_in, l3_ffn_w_out, d_w_in, d_w_out):
    b, s, d = x.shape
    t = b * s
    pos_b = jnp.broadcast_to(positions.astype(F32).reshape(t, 1), (t, LANES))
    inv_freq = ROPE_THETA ** (-jnp.arange(0, ROPE_DIMS, 2, dtype=F32) / ROPE_DIMS)
    freq = jnp.tile(inv_freq, LANES // inv_freq.shape[0]).reshape(1, LANES)

    mixers = (
        lambda h, n, f: _swa_mixer(h, b, s, n[2], pos_b, freq, a_w_in, a_b_in, a_sinks, a_w_out, n[3], f),
        lambda h, n, f: _gdn_mixer(h, b, s, n[2], b_w_in, b_conv_w, b_A_log, b_dt_bias, b_norm, b_w_out, n[3], f),
        lambda h, n, f: _ssd_mixer(h, b, s, n[2], c_w_in, c_conv_w, c_conv_b, c_dt_bias, c_A_log, c_D, c_norm,
                                   c_w_out, n[3], f),
        lambda h, n, f: _dilated_mixer(h, b, s, n[2], pos_b, freq, d_w_in, d_w_out, n[3], f),
    )
    layers = ((l0_norms, l0_ffn_w_in, l0_ffn_w_out), (l1_norms, l1_ffn_w_in, l1_ffn_w_out),
              (l2_norms, l2_ffn_w_in, l2_ffn_w_out), (l3_norms, l3_ffn_w_in, l3_ffn_w_out))
    h = x.reshape(t, d)
    for i, (norms, ffn_w_in, ffn_w_out) in enumerate(layers):
        norms = norms.astype(F32)
        h = _ffn(h, norms[0:2], ffn_w_in[0], ffn_w_out[0])
        h = mixers[i % len(mixers)](h, norms, (norms[4:6], ffn_w_in[1], ffn_w_out[1]))
    return h.reshape(b, s, d)
```

```python
import functools
import math

import jax
import jax.numpy as jnp
from jax import lax
from jax.experimental import pallas as pl
from jax.experimental.pallas import tpu as pltpu

F32 = jnp.float32
BF16 = jnp.bfloat16

D_MODEL = 1024
D_FF = 2816
NORM_EPS = 1e-6
ROPE_THETA = 500000.0
ROPE_DIMS = 16
HEAD_DIM = 64
BLOCK = 128
NEG_INF = -1e30

SWA_HEADS, SWA_KV_HEADS, SWA_WINDOW = 16, 2, 128
GDN_HEADS, GDN_HEAD_DIM, GDN_CONV, GDN_CHUNK = 8, 128, 4, 64
SSD_D_INNER, SSD_HEAD_DIM, SSD_GROUPS, SSD_STATE, SSD_CONV, SSD_CHUNK = 2048, 64, 4, 128, 4, 128
SSD_HEADS = SSD_D_INNER // SSD_HEAD_DIM
DIL_PATTERN = ((128, 1), (512, 4), (2048, 16))
DIL_HEADS, DIL_KV_HEADS = 16, 4

LANES = 128
CARRY_ROWS = 8
VMEM_LIMIT = 56 * 1024 * 1024
HI = lax.Precision.HIGHEST

TM_FFN = 512
TM_PROJ = 512
FF_CHUNK = 1536
ATTN_BLOCKS = 2


def _params(sem):
    return pltpu.CompilerParams(dimension_semantics=sem, vmem_limit_bytes=VMEM_LIMIT)


def _resident(shape):
    nd = len(shape)
    return pl.BlockSpec(shape, lambda *_: (0,) * nd, pipeline_mode=pl.Buffered(1))


def _rmsnorm(x, gain):
    return x * lax.rsqrt(jnp.mean(x * x, axis=-1, keepdims=True) + NORM_EPS) * gain


def _silu(x):
    half = 0.5 * x
    return half * (jnp.tanh(half) + 1.0)


def _softplus(x):
    return jnp.maximum(x, 0.0) + jnp.log(1.0 + jnp.exp(-jnp.abs(x)))


def _dot(a, b):
    return jnp.dot(a, b, preferred_element_type=F32)


def _dot_nt(a, b, precision=None):
    return lax.dot_general(a, b, (((1,), (1,)), ((), ())), preferred_element_type=F32, precision=precision)


def _dot_tn(a, b):
    return lax.dot_general(a, b, (((0,), (0,)), ((), ())), preferred_element_type=F32)


def _eye(n, dtype=F32):
    return (lax.broadcasted_iota(jnp.int32, (n, n), 0) == lax.broadcasted_iota(jnp.int32, (n, n), 1)).astype(dtype)


def _tril(n, dtype=F32):
    return (lax.broadcasted_iota(jnp.int32, (n, n), 0) >= lax.broadcasted_iota(jnp.int32, (n, n), 1)).astype(dtype)


def _ffn_value(x, g_ref, win_ref, wout_ref, act_ref):
    h = _rmsnorm(x, g_ref[0:1, :]).astype(BF16)
    for lo in range(0, D_FF, FF_CHUNK):
        width = min(FF_CHUNK, D_FF - lo)
        gate = _dot(h, win_ref[:, lo:lo + width])
        up = _dot(h, win_ref[:, D_FF + lo:D_FF + lo + width])
        act_ref[:, lo:lo + width] = (_silu(gate) * up).astype(BF16)
    y = _dot(act_ref[...], wout_ref[...])
    return x + 0.5 * _rmsnorm(y, g_ref[1:2, :])


def _ffn_specs():
    return [_resident((2, D_MODEL)), _resident((D_MODEL, 2 * D_FF)), _resident((D_FF, D_MODEL))]


def _ffn_body(x_ref, g_ref, win_ref, wout_ref, o_ref, act_ref):
    o_ref[...] = _ffn_value(x_ref[...], g_ref, win_ref, wout_ref, act_ref)


def _ffn(x, gains, w_in, w_out):
    t = x.shape[0]
    tm = min(TM_FFN, t)
    return pl.pallas_call(
        _ffn_body,
        grid=(t // tm,),
        in_specs=[pl.BlockSpec((tm, D_MODEL), lambda i: (i, 0))] + _ffn_specs(),
        out_specs=pl.BlockSpec((tm, D_MODEL), lambda i: (i, 0)),
        out_shape=jax.ShapeDtypeStruct((t, D_MODEL), F32),
        scratch_shapes=[pltpu.VMEM((tm, D_FF), BF16)],
        compiler_params=_params(("parallel",)),
        name="ffn",
    )(x, gains, w_in.astype(BF16), w_out.astype(BF16))


def _outproj_ffn_body(y_ref, x_ref, gp_ref, wo_ref, g_ref, win_ref, wout_ref, o_ref, act_ref):
    x = x_ref[...] + _rmsnorm(_dot(y_ref[...], wo_ref[...]), gp_ref[...])
    o_ref[...] = _ffn_value(x, g_ref, win_ref, wout_ref, act_ref)


def _outproj_ffn(y, x, gain, w_out, ffn_gains, ffn_w_in, ffn_w_out):
    t, k = y.shape
    tm = min(TM_FFN, t)
    return pl.pallas_call(
        _outproj_ffn_body,
        grid=(t // tm,),
        in_specs=[pl.BlockSpec((tm, k), lambda i: (i, 0)),
                  pl.BlockSpec((tm, D_MODEL), lambda i: (i, 0)),
                  _resident((1, D_MODEL)),
                  _resident((k, D_MODEL))] + _ffn_specs(),
        out_specs=pl.BlockSpec((tm, D_MODEL), lambda i: (i, 0)),
        out_shape=jax.ShapeDtypeStruct((t, D_MODEL), F32),
        scratch_shapes=[pltpu.VMEM((tm, D_FF), BF16)],
        compiler_params=_params(("parallel",)),
        name="outproj_ffn",
    )(y, x, gain.reshape(1, D_MODEL), w_out.astype(BF16), ffn_gains, ffn_w_in.astype(BF16), ffn_w_out.astype(BF16))


def _rope_tables(pos_ref, freq_ref):
    ang = pos_ref[...] * freq_ref[...]
    d = lax.broadcasted_iota(jnp.int32, ang.shape, 1) % HEAD_DIM
    half = ROPE_DIMS // 2
    cos, sin = jnp.cos(ang), jnp.sin(ang)
    c = jnp.where(d < ROPE_DIMS, cos, 1.0)
    s_lo = jnp.where(d < half, -sin, 0.0)
    s_hi = jnp.where((d >= half) & (d < ROPE_DIMS), sin, 0.0)
    return c, s_lo, s_hi


def _rope(y, tables):
    c, s_lo, s_hi = tables
    half = ROPE_DIMS // 2
    return y * c + pltpu.roll(y, LANES - half, 1) * s_lo + pltpu.roll(y, half, 1) * s_hi


PROJ_CHUNK = 512
CONV_CHUNK = 256


def _proj_attn_body(*refs, layout, q_scale, has_bias):
    x_ref, g_ref, pos_ref, freq_ref, w_ref = refs[:5]
    refs = refs[5:]
    if has_bias:
        b_ref, refs = refs[0], refs[1:]
    out_refs, tmp_ref = refs[:len(layout)], refs[len(layout)]
    rows = x_ref.shape[0]
    h = _rmsnorm(x_ref[...], g_ref[...]).astype(BF16)
    k_tables = _rope_tables(pos_ref, freq_ref)
    q_tables = tuple(t * q_scale for t in k_tables)
    for (kind, start, width, dilation), o_ref in zip(layout, out_refs):
        chunk = min(width, PROJ_CHUNK)
        for lo in range(0, width, chunk):
            y = _dot(h, w_ref[:, start + lo:start + lo + chunk])
            if has_bias:
                y = y + b_ref[:, start + lo:start + lo + chunk]
            for sub in range(0, chunk, LANES):
                ys = y[:, sub:sub + LANES]
                if kind == "q":
                    ys = _rope(ys, q_tables)
                elif kind == "k":
                    ys = _rope(ys, k_tables)
                if dilation == 1:
                    o_ref[0, :, lo + sub:lo + sub + LANES] = ys.astype(o_ref.dtype)
                else:
                    tmp_ref[sub // LANES] = ys
            if dilation > 1:
                for sub in range(0, chunk, LANES):
                    for r in range(dilation):
                        o_ref[r, :, lo + sub:lo + sub + LANES] = tmp_ref[
                            sub // LANES, pl.ds(r, rows // dilation, stride=dilation), :].astype(o_ref.dtype)


def _proj_attn(x, b, s, gain, pos_b, freq, w_in, b_in, layout):
    n = w_in.shape[1]
    tm = min(TM_PROJ, s)
    steps = s // tm
    has_bias = b_in is not None
    body = functools.partial(_proj_attn_body, layout=layout, q_scale=HEAD_DIM ** -0.5, has_bias=has_bias)
    row = lambda width: pl.BlockSpec((tm, width), lambda bi, i: (bi * steps + i, 0))
    in_specs = [row(D_MODEL), _resident((1, D_MODEL)), row(LANES), _resident((1, LANES)), _resident((D_MODEL, n))]
    args = [x, gain.reshape(1, D_MODEL), pos_b, freq, w_in.astype(BF16)]
    if has_bias:
        in_specs.append(_resident((1, n)))
        args.append(b_in.reshape(1, n).astype(F32))
    return pl.pallas_call(
        body,
        grid=(b, steps),
        in_specs=in_specs,
        out_specs=[pl.BlockSpec((None, d, tm // d, w), lambda bi, i: (bi, 0, i, 0)) for _, _, w, d in layout],
        out_shape=[jax.ShapeDtypeStruct((b, d, s // d, w), BF16) for _, _, w, d in layout],
        scratch_shapes=[pltpu.VMEM((PROJ_CHUNK // LANES, tm, LANES), F32)],
        compiler_params=_params(("parallel", "parallel")),
        name="proj_attn",
    )(*args)


def _proj_conv_body(*refs, layout, conv_width, has_bias):
    x_ref, g_ref, w_ref, convw_ref = refs[:4]
    refs = refs[4:]
    if has_bias:
        convb_ref, refs = refs[0], refs[1:]
    out_refs, (carry_ref, buf_ref) = refs[:len(layout)], refs[len(layout):]
    rows = x_ref.shape[0]

    @pl.when(pl.program_id(1) == 0)
    def _():
        carry_ref[...] = jnp.zeros_like(carry_ref)

    h = _rmsnorm(x_ref[...], g_ref[...]).astype(BF16)
    slot = 0
    for (start, width, conv_col), o_ref in zip(layout, out_refs):
        chunk = PROJ_CHUNK if width % PROJ_CHUNK == 0 else LANES
        if conv_col is not None:
            chunk = CONV_CHUNK
        for lo in range(0, width, chunk):
            y = _dot(h, w_ref[:, start + lo:start + lo + chunk])
            if conv_col is None:
                o_ref[:, lo:lo + chunk] = y
                continue
            cols = slice(conv_col + lo, conv_col + lo + chunk)
            buf = buf_ref.at[slot % buf_ref.shape[0]]
            slot += 1
            buf[0:CARRY_ROWS, 0:chunk] = carry_ref[:, cols]
            buf[CARRY_ROWS:CARRY_ROWS + rows, 0:chunk] = y
            carry_ref[:, cols] = buf[rows:rows + CARRY_ROWS, 0:chunk]
            acc = None
            for i in range(conv_width):
                first = CARRY_ROWS - (conv_width - 1) + i
                term = buf[first:first + rows, 0:chunk] * convw_ref[i:i + 1, cols]
                acc = term if acc is None else acc + term
            if has_bias:
                acc = acc + convb_ref[:, cols]
            o_ref[:, lo:lo + chunk] = _silu(acc)


def _proj_conv(x, b, s, gain, w_in, conv_w, conv_b, layout):
    n = w_in.shape[1]
    conv_ch = conv_w.shape[1]
    tm = min(TM_PROJ, s)
    steps = s // tm
    has_bias = conv_b is not None
    row = lambda width: pl.BlockSpec((tm, width), lambda bi, i: (bi * steps + i, 0))
    in_specs = [row(D_MODEL), _resident((1, D_MODEL)), _resident((D_MODEL, n)), _resident(conv_w.shape)]
    args = [x, gain.reshape(1, D_MODEL), w_in.astype(BF16), conv_w.astype(F32)]
    if has_bias:
        in_specs.append(_resident((1, conv_ch)))
        args.append(conv_b.reshape(1, conv_ch).astype(F32))
    return pl.pallas_call(
        functools.partial(_proj_conv_body, layout=layout, conv_width=conv_w.shape[0], has_bias=has_bias),
        grid=(b, steps),
        in_specs=in_specs,
        out_specs=[row(w) for _, w, _ in layout],
        out_shape=[jax.ShapeDtypeStruct((b * s, w), F32) for _, w, _ in layout],
        scratch_shapes=[pltpu.VMEM((CARRY_ROWS, conv_ch), F32),
                        pltpu.VMEM((2, CARRY_ROWS + tm, PROJ_CHUNK), F32)],
        compiler_params=_params(("parallel", "arbitrary")),
        name="proj_conv",
    )(*args)


def _attn_body(*refs, n_heads, kv_heads, n_back, n_blocks, with_sink):
    if with_sink:
        sink_ref, q_ref, kc_ref, kp_ref, vc_ref, vp_ref, o_ref = refs
    else:
        q_ref, kc_ref, kp_ref, vc_ref, vp_ref, o_ref, lse_ref = refs
    gq = n_heads // kv_heads
    assert gq % 2 == 0 and kv_heads % 2 == 0
    kw = 2 * BLOCK
    step = pl.program_id(2)
    k_all = jnp.concatenate([kp_ref[...], kc_ref[...]], axis=0)
    v_all = jnp.concatenate([vp_ref[...], vc_ref[...]], axis=0)
    rows = lax.broadcasted_iota(jnp.int32, (BLOCK, 2 * kw), 0) + BLOCK
    cols = lax.broadcasted_iota(jnp.int32, (BLOCK, 2 * kw), 1) % kw
    band = (rows - cols >= 0) & (rows - cols <= n_back)
    lane = lax.broadcasted_iota(jnp.int32, (BLOCK, LANES), 1)
    low = lane < HEAD_DIM
    key_low = lax.broadcasted_iota(jnp.int32, (kw, LANES), 1) < HEAD_DIM
    zeros_b = jnp.zeros((kw, LANES), BF16)
    ones_low, ones_high = key_low.astype(BF16), jnp.logical_not(key_low).astype(BF16)
    for j in range(n_blocks):
        r0 = j * BLOCK
        has_prev = (step * n_blocks + j) > 0
        visible = band & ((cols >= BLOCK) | has_prev)
        keys, values = [], []
        for t in range(kv_heads // 2):
            k_tile = k_all[r0:r0 + kw, t * LANES:(t + 1) * LANES]
            v_tile = v_all[r0:r0 + kw, t * LANES:(t + 1) * LANES]
            k_swap, v_swap = pltpu.roll(k_tile, HEAD_DIM, 1), pltpu.roll(v_tile, HEAD_DIM, 1)
            for first in (True, False):
                k_dup = jnp.where(key_low == first, k_tile, k_swap)
                v_dup = jnp.where(key_low == first, v_tile, v_swap)
                keys.append(jnp.concatenate([jnp.where(key_low, k_dup, zeros_b),
                                             jnp.where(key_low, zeros_b, k_dup)], axis=0))
                values.append(jnp.concatenate(
                    [jnp.concatenate([jnp.where(key_low, v_dup, zeros_b), ones_low], axis=1),
                     jnp.concatenate([jnp.where(key_low, zeros_b, v_dup), ones_high], axis=1)], axis=0))
        pairs = range(n_heads // 2)
        kv_of = [2 * p // gq for p in pairs]
        scores = [jnp.where(visible, _dot_nt(q_ref[r0:r0 + BLOCK, p * LANES:(p + 1) * LANES], keys[kv_of[p]]), NEG_INF)
                  for p in pairs]
        max_a = [jnp.max(s[:, :kw], axis=-1, keepdims=True) for s in scores]
        max_b = [jnp.max(s[:, kw:], axis=-1, keepdims=True) for s in scores]
        probs = [jnp.concatenate([jnp.exp(s[:, :kw] - ma), jnp.exp(s[:, kw:] - mb)], axis=1).astype(BF16)
                 for s, ma, mb in zip(scores, max_a, max_b)]
        accs = [_dot(pr, values[kv_of[p]]) for p, pr in zip(pairs, probs)]
        lse_tile = jnp.zeros((BLOCK, LANES), F32)
        for p in pairs:
            acc, l = accs[p][:, :LANES], accs[p][:, LANES:]
            m = jnp.where(low, max_a[p], max_b[p])
            if with_sink:
                sink = jnp.where(low, sink_ref[2 * p], sink_ref[2 * p + 1])
                m_all = jnp.maximum(m, sink)
                rescale = jnp.exp(m - m_all)
                out = acc * (rescale / (l * rescale + jnp.exp(sink - m_all)))
            else:
                out = acc / l
                lse_tile = jnp.where((lane == 2 * p) | (lane == HEAD_DIM + 2 * p + 1), m + jnp.log(l), lse_tile)
            o_ref[r0:r0 + BLOCK, p * LANES:(p + 1) * LANES] = out.astype(o_ref.dtype)
        if not with_sink:
            lse_ref[r0:r0 + BLOCK, :] = lse_tile


def _lse_lane_of_head(h):
    return h if h % 2 == 0 else HEAD_DIM + h


def _attention(q, k, v, *, n_heads, kv_heads, n_back, sinks=None):
    b, d, length, qc = q.shape
    kc = k.shape[-1]
    assert length % BLOCK == 0
    n_blocks = ATTN_BLOCKS if length % (ATTN_BLOCKS * BLOCK) == 0 else 1
    tq = n_blocks * BLOCK
    with_sink = sinks is not None

    def cur(width):
        return pl.BlockSpec((None, None, tq, width), lambda bi, r, i: (bi, r, i, 0))

    def prev(width):
        return pl.BlockSpec((None, None, BLOCK, width), lambda bi, r, i: (bi, r, jnp.maximum(i * n_blocks - 1, 0), 0))

    in_specs = [cur(qc), cur(kc), prev(kc), cur(kc), prev(kc)]
    args = [q, k, k, v, v]
    if with_sink:
        in_specs = [pl.BlockSpec(memory_space=pltpu.SMEM)] + in_specs
        args = [sinks.astype(F32)] + args
        out_specs = cur(qc)
        out_shape = jax.ShapeDtypeStruct((b, d, length, qc), BF16)
    else:
        out_specs = [cur(qc), cur(LANES)]
        out_shape = [jax.ShapeDtypeStruct((b, d, length, qc), BF16),
                     jax.ShapeDtypeStruct((b, d, length, LANES), F32)]
    body = functools.partial(_attn_body, n_heads=n_heads, kv_heads=kv_heads, n_back=n_back,
                             n_blocks=n_blocks, with_sink=with_sink)
    return pl.pallas_call(
        body,
        grid=(b, d, length // tq),
        in_specs=in_specs,
        out_specs=out_specs,
        out_shape=out_shape,
        compiler_params=_params(("parallel", "parallel", "parallel")),
        name="attn_sink" if with_sink else "attn_lse",
    )(*args)


def _outproj_merge_body(*refs, dilations, n_heads):
    n_groups = len(dilations)
    o_refs, lse_refs = refs[:n_groups], refs[n_groups:2 * n_groups]
    x_ref, g_ref, w_ref, fg_ref, win_ref, wout_ref, out_ref, o_nat_ref, lse_nat_ref, act_ref = refs[2 * n_groups:]
    rows = x_ref.shape[0]

    def natural(ref, scratch, d):
        if d == 1:
            return ref[0].astype(F32)
        slabs = scratch.shape[0]
        for r in range(d):
            for c in range(slabs):
                scratch[c, pl.ds(r, rows // d, stride=d), :] = ref[r, :, c * LANES:(c + 1) * LANES].astype(F32)
        return jnp.concatenate([scratch[c] for c in range(slabs)], axis=1) if slabs > 1 else scratch[0]

    lses = [natural(lse_refs[g], lse_nat_ref.at[g:g + 1], d) for g, d in enumerate(dilations)]
    top = functools.reduce(jnp.maximum, lses)
    exps = [jnp.exp(s - top) for s in lses]
    den = functools.reduce(jnp.add, exps)
    n = o_refs[0].shape[-1]
    lane = lax.broadcasted_iota(jnp.int32, (LANES, n), 0)
    head_of_col = lax.broadcasted_iota(jnp.int32, (LANES, n), 1) // HEAD_DIM
    expand = functools.reduce(jnp.logical_or, [(lane == _lse_lane_of_head(h)) & (head_of_col == h)
                                               for h in range(n_heads)]).astype(BF16)
    merged = None
    for e, o_ref, d in zip(exps, o_refs, dilations):
        weights = jnp.concatenate(_split(e / den), axis=1)
        term = _dot(weights, jnp.concatenate([expand, expand], axis=0)) * natural(o_ref, o_nat_ref, d)
        merged = term if merged is None else merged + term
    x = x_ref[...] + _rmsnorm(_dot(merged.astype(BF16), w_ref[...]), g_ref[...])
    out_ref[...] = _ffn_value(x, fg_ref, win_ref, wout_ref, act_ref)


def _outproj_merge_ffn(outs, lses, x, b, s, gain, w_out, n_heads, ffn_gains, ffn_w_in, ffn_w_out):
    k = outs[0].shape[-1]
    dilations = tuple(o.shape[1] for o in outs)
    tm = min(TM_FFN, s)
    steps = s // tm
    grouped = lambda d, width: pl.BlockSpec((None, d, tm // d, width), lambda bi, i: (bi, 0, i, 0))
    row = pl.BlockSpec((tm, D_MODEL), lambda bi, i: (bi * steps + i, 0))
    return pl.pallas_call(
        functools.partial(_outproj_merge_body, dilations=dilations, n_heads=n_heads),
        grid=(b, steps),
        in_specs=[grouped(d, k) for d in dilations] + [grouped(d, LANES) for d in dilations]
                 + [row, _resident((1, D_MODEL)), _resident((k, D_MODEL))] + _ffn_specs(),
        out_specs=row,
        out_shape=jax.ShapeDtypeStruct((b * s, D_MODEL), F32),
        scratch_shapes=[pltpu.VMEM((k // LANES, tm, LANES), F32), pltpu.VMEM((len(dilations), tm, LANES), F32),
                        pltpu.VMEM((tm, D_FF), BF16)],
        compiler_params=_params(("parallel", "parallel")),
        name="outproj_merge_ffn",
    )(*outs, *lses, x, gain.reshape(1, D_MODEL), w_out.astype(BF16), ffn_gains, ffn_w_in.astype(BF16),
      ffn_w_out.astype(BF16))


def _l2norm(x):
    return x * lax.rsqrt(jnp.sum(x * x, axis=-1, keepdims=True) + NORM_EPS)


def _split(a):
    hi = a.astype(BF16)
    return hi, (a - hi.astype(F32)).astype(BF16)


def _dot_split(a, b):
    (ah, al), (bh, bl) = a, b
    return _dot(jnp.concatenate([ah, al, ah], axis=1), jnp.concatenate([bh, bh, bl], axis=0))


def _unit_lower_inverses(strict_lowers):
    n = strict_lowers[0].shape[0]
    ri = lax.broadcasted_iota(jnp.int32, (n, n), 0)
    ci = lax.broadcasted_iota(jnp.int32, (n, n), 1)
    invs = None
    for level in range(int(math.log2(n))):
        joins = ((ri >> (level + 1)) == (ci >> (level + 1))) & ((ri >> level) != (ci >> level))
        couplings = [jnp.where(joins, l, 0.0) for l in strict_lowers]
        if invs is None:
            invs = [_eye(n) - c for c in couplings]
            continue
        inv_splits = [_split(inv) for inv in invs]
        right = [_dot_split(_split(c), d) for c, d in zip(couplings, inv_splits)]
        yield
        invs = [inv - _dot_split(d, _split(r)) for inv, d, r in zip(invs, inv_splits, right)]
        yield
    return invs


def _gdn_body(qkv_ref, z_ref, small_ref, alog_ref, dtb_ref, normw_ref, o_ref, state_ref):
    c = GDN_CHUNK
    nh, dk = GDN_HEADS, GDN_HEAD_DIM
    width = nh * dk
    rows = qkv_ref.shape[1]
    chunks = range(rows // c)

    @pl.when(pl.program_id(1) == 0)
    def _():
        state_ref[...] = jnp.zeros_like(state_ref)

    small = small_ref[0]
    beta_all = jax.nn.sigmoid(small)
    g_all = -jnp.exp(alog_ref[...]) * _softplus(small + dtb_ref[...])
    tril = _tril(c)
    gcum_alls = [jnp.dot(tril, g_all[j * c:(j + 1) * c], preferred_element_type=F32, precision=HI)
                 for j in chunks]
    gcum_rows = [_dot_nt(_eye(LANES), g, precision=HI) for g in gcum_alls]
    ri = lax.broadcasted_iota(jnp.int32, (c, c), 0)
    ci = lax.broadcasted_iota(jnp.int32, (c, c), 1)
    causal, strict = ri >= ci, ri > ci
    def act(c0, j):
        return qkv_ref[0, j * c:(j + 1) * c, c0:c0 + dk]

    def prepare(j):
        heads = range(nh)
        qs = [_l2norm(act(h * dk, j)) * (dk ** -0.5) for h in heads]
        yield
        ks = [_l2norm(act(width + h * dk, j)) for h in heads]
        yield
        vs = [act(2 * width + h * dk, j) for h in heads]
        betas = [beta_all[j * c:(j + 1) * c, h:h + 1] for h in heads]
        gcums = [gcum_alls[j][:, nh + h:nh + h + 1] for h in heads]
        decays = [jnp.exp(jnp.where(causal, gcums[h] - gcum_rows[j][nh + h:nh + h + 1, :], -jnp.inf)) for h in heads]
        yield
        k_bs = [k.astype(BF16) for k in ks]
        q_bs = [q.astype(BF16) for q in qs]
        lowers = [jnp.where(strict, _dot_nt(k_bs[h], k_bs[h]) * decays[h] * betas[h], 0.0) for h in heads]
        yield
        intras = [jnp.where(causal, _dot_nt(q_bs[h], k_bs[h]) * decays[h], 0.0).astype(BF16) for h in heads]
        yield
        invs = yield from _unit_lower_inverses(lowers)
        rhss = [jnp.concatenate([vs[h] * betas[h], ks[h] * (betas[h] * jnp.exp(gcums[h]))], axis=-1) for h in heads]
        yield
        sols = [_dot_split(_split(invs[h]), _split(rhss[h])) for h in heads]
        yield
        g_lasts = [g[c - 1:c, :] for g in gcums]
        return dict(
            us=[sol[:, :dk] for sol in sols],
            w_bs=[sol[:, dk:].astype(BF16) for sol in sols],
            q_ins=[(qs[h] * jnp.exp(gcums[h])).astype(BF16) for h in heads],
            k_ends=[(ks[h] * jnp.exp(g_lasts[h] - gcums[h])).astype(BF16) for h in heads],
            g_lasts=g_lasts, intras=intras)

    def finish(j, prep, states):
        heads = range(nh)
        state_bs = [s.astype(BF16) for s in states]
        v_new_bs = [(prep["us"][h] - _dot(prep["w_bs"][h], state_bs[h])).astype(BF16) for h in heads]
        outs = [_dot(prep["q_ins"][h], state_bs[h]) + _dot(prep["intras"][h], v_new_bs[h]) for h in heads]
        new_states = [states[h] * jnp.exp(prep["g_lasts"][h]) + _dot_tn(prep["k_ends"][h], v_new_bs[h]) for h in heads]
        for h in heads:
            gate = _silu(z_ref[0, j * c:(j + 1) * c, h * dk:(h + 1) * dk])
            o_ref[0, j * c:(j + 1) * c, h * dk:(h + 1) * dk] = (
                _rmsnorm(outs[h], normw_ref[...]) * gate).astype(o_ref.dtype)
        return new_states

    states = [state_ref[h] for h in range(nh)]
    pipelines = [prepare(j) for j in chunks]
    done = 0
    tick = 0
    while done < len(pipelines):
        for j in range(done, len(pipelines)):
            if tick < j * GDN_STAGE_SKEW:
                break
            try:
                next(pipelines[j])
            except StopIteration as stop:
                assert j == done
                states = finish(j, stop.value, states)
                done += 1
        tick += 1
    for h in range(nh):
        state_ref[h] = states[h]


GDN_STAGE_SKEW = 2
GDN_CHUNKS_PER_STEP = 4


def _gdn_core(qkv, z, small, a_log, dt_bias, norm_w):
    b, s, _ = qkv.shape
    nh, dk = GDN_HEADS, GDN_HEAD_DIM
    width = nh * dk
    rows = GDN_CHUNKS_PER_STEP * GDN_CHUNK
    assert s % rows == 0
    pad = lambda a: jnp.zeros((1, LANES), F32).at[0, nh:2 * nh].set(a.astype(F32))
    return pl.pallas_call(
        _gdn_body,
        grid=(b, s // rows),
        in_specs=[pl.BlockSpec((1, rows, 3 * width), lambda bi, i: (bi, i, 0)),
                  pl.BlockSpec((1, rows, width), lambda bi, i: (bi, i, 0)),
                  pl.BlockSpec((1, rows, LANES), lambda bi, i: (bi, i, 0)),
                  _resident((1, LANES)),
                  _resident((1, LANES)),
                  _resident((1, dk))],
        out_specs=pl.BlockSpec((1, rows, width), lambda bi, i: (bi, i, 0)),
        out_shape=jax.ShapeDtypeStruct((b, s, width), BF16),
        scratch_shapes=[pltpu.VMEM((nh, dk, dk), F32)],
        compiler_params=_params(("parallel", "arbitrary")),
        name="gdn",
    )(qkv, z, small, pad(a_log), pad(dt_bias), norm_w.reshape(1, dk).astype(F32))


def _ssd_body(z_ref, xbc_ref, dt_ref, dtb_ref, alog_ref, dskip_ref, normw_ref, expand_ref, o_ref, state_ref):
    c = SSD_CHUNK
    di, ng, p = SSD_D_INNER, SSD_GROUPS, SSD_HEAD_DIM
    heads_per_group = SSD_HEADS // ng
    pairs_per_group = heads_per_group // 2
    group_width = di // ng

    @pl.when(pl.program_id(1) == 0)
    def _():
        state_ref[...] = jnp.zeros_like(state_ref)

    def conv(c0):
        return xbc_ref[0, :, c0:c0 + LANES]

    dt_all = _softplus(dt_ref[0] + dtb_ref[...])
    adt = dt_all * -jnp.exp(alog_ref[...])
    acum_all = jnp.dot(_tril(c), adt, preferred_element_type=F32, precision=HI)
    acum_rows = _dot_nt(_eye(LANES), acum_all, precision=HI)
    from_start_all = jnp.exp(acum_all)
    to_end_all = jnp.exp(acum_all[c - 1:c, :] - acum_all)
    ri = lax.broadcasted_iota(jnp.int32, (c, c), 0)
    ci = lax.broadcasted_iota(jnp.int32, (c, c), 1)
    causal = ri >= ci
    low_half = lax.broadcasted_iota(jnp.int32, (c, LANES), 1) < p
    eye_b = _eye(LANES, BF16)
    groups, pairs = range(ng), range(SSD_HEADS // 2)
    group_of = [pr // pairs_per_group for pr in pairs]

    def to_channels(all_heads):
        return _dot(jnp.concatenate(_split(all_heads), axis=1), expand_ref[...])

    dt_ch, from_start_ch, to_end_ch = to_channels(dt_all), to_channels(from_start_all), to_channels(to_end_all)

    def per_head(channels, pr):
        return channels[:, pr * LANES:(pr + 1) * LANES]

    def seg(hd, cb):
        diff = acum_all[:, hd:hd + 1] - acum_rows[hd:hd + 1, :]
        return (cb * jnp.exp(jnp.where(causal, diff, -jnp.inf))).astype(BF16)

    bms = [conv(di + g * SSD_STATE).astype(BF16) for g in groups]
    cms = [conv(di + ng * SSD_STATE + g * SSD_STATE).astype(BF16) for g in groups]
    cbs = [_dot_nt(cms[g], bms[g]) for g in groups]
    bm_ts = [_dot_nt(eye_b, bms[g]).astype(BF16) for g in groups]
    xs = [conv(pr * LANES) for pr in pairs]
    x_dts = [xs[pr] * per_head(dt_ch, pr) for pr in pairs]
    x_dt_bs = [v.astype(BF16) for v in x_dts]
    y_diags = [jnp.where(low_half, _dot(seg(2 * pr, cbs[group_of[pr]]), x_dt_bs[pr]),
                         _dot(seg(2 * pr + 1, cbs[group_of[pr]]), x_dt_bs[pr])) for pr in pairs]
    states = [state_ref[pr] for pr in pairs]
    y_offs = [_dot(cms[group_of[pr]], states[pr].astype(BF16)) * per_head(from_start_ch, pr) for pr in pairs]
    for pr in pairs:
        decay_all = per_head(from_start_ch, pr)[c - 1:c, :]
        state_ref[pr] = states[pr] * decay_all + _dot(bm_ts[group_of[pr]],
                                                      (x_dts[pr] * per_head(to_end_ch, pr)).astype(BF16))
    ys = [(y_diags[pr] + y_offs[pr] + xs[pr] * dskip_ref[:, pr * LANES:(pr + 1) * LANES])
          * _silu(z_ref[0, :, pr * LANES:(pr + 1) * LANES]) for pr in pairs]
    sumsqs = [jnp.sum(y * y, axis=-1, keepdims=True) for y in ys]
    for g in groups:
        members = [pr for pr in pairs if group_of[pr] == g]
        scale = lax.rsqrt(functools.reduce(jnp.add, [sumsqs[pr] for pr in members]) * (1.0 / group_width) + NORM_EPS)
        for pr in members:
            c0 = pr * LANES
            o_ref[0, :, c0:c0 + LANES] = (ys[pr] * scale * normw_ref[:, c0:c0 + LANES]).astype(o_ref.dtype)


def _ssd_core(z, xbc, dt_raw, dt_bias, a_log, d_skip, norm_w):
    b, s, conv_ch = xbc.shape
    c = SSD_CHUNK
    di = SSD_D_INNER
    pad = lambda a: jnp.zeros((1, LANES), F32).at[0, :SSD_HEADS].set(a.astype(F32))
    expand = (jnp.arange(2 * LANES)[:, None] % LANES == jnp.arange(di)[None, :] // SSD_HEAD_DIM).astype(BF16)
    return pl.pallas_call(
        _ssd_body,
        grid=(b, s // c),
        in_specs=[pl.BlockSpec((1, c, di), lambda bi, i: (bi, i, 0)),
                  pl.BlockSpec((1, c, conv_ch), lambda bi, i: (bi, i, 0)),
                  pl.BlockSpec((1, c, LANES), lambda bi, i: (bi, i, 0)),
                  _resident((1, LANES)),
                  _resident((1, LANES)),
                  _resident((1, di)),
                  _resident((1, di)),
                  _resident((2 * LANES, di))],
        out_specs=pl.BlockSpec((1, c, di), lambda bi, i: (bi, i, 0)),
        out_shape=jax.ShapeDtypeStruct((b, s, di), BF16),
        scratch_shapes=[pltpu.VMEM((SSD_HEADS // 2, SSD_STATE, LANES), F32)],
        compiler_params=_params(("parallel", "arbitrary")),
        name="ssd",
    )(z, xbc, dt_raw, pad(dt_bias), pad(a_log),
      jnp.repeat(d_skip.astype(F32), SSD_HEAD_DIM).reshape(1, di), norm_w.reshape(1, di).astype(F32), expand)


def _pad_cols(w, n):
    return jnp.pad(w, ((0, 0), (0, n - w.shape[1])))


def _swa_mixer(x, b, s, gain, pos_b, freq, w_in, b_in, sinks, w_out, gain_post, ffn):
    nq, nkv = SWA_HEADS * HEAD_DIM, SWA_KV_HEADS * HEAD_DIM
    layout = (("q", 0, nq, 1), ("k", nq, nkv, 1), ("v", nq + nkv, nkv, 1))
    q, k, v = _proj_attn(x, b, s, gain, pos_b, freq, w_in, b_in, layout)
    o = _attention(q, k, v, n_heads=SWA_HEADS, kv_heads=SWA_KV_HEADS, n_back=SWA_WINDOW - 1, sinks=sinks)
    return _outproj_ffn(o.reshape(b * s, nq), x, gain_post, w_out, *ffn)


def _dilated_mixer(x, b, s, gain, pos_b, freq, w_in, w_out, gain_post, ffn):
    nq, nkv = DIL_HEADS * HEAD_DIM, DIL_KV_HEADS * HEAD_DIM
    per_group = nq + 2 * nkv
    layout = []
    for gi, (_, dilation) in enumerate(DIL_PATTERN):
        base = gi * per_group
        layout += [("q", base, nq, dilation), ("k", base + nq, nkv, dilation), ("v", base + nq + nkv, nkv, dilation)]
    qkv = _proj_attn(x, b, s, gain, pos_b, freq, w_in, None, tuple(layout))
    group_outs, group_lses = [], []
    for gi, (window, dilation) in enumerate(DIL_PATTERN):
        o, lse = _attention(*qkv[3 * gi:3 * gi + 3], n_heads=DIL_HEADS, kv_heads=DIL_KV_HEADS,
                            n_back=window // dilation)
        group_outs.append(o)
        group_lses.append(lse)
    return _outproj_merge_ffn(group_outs, group_lses, x, b, s, gain_post, w_out, DIL_HEADS, *ffn)


def _gdn_mixer(x, b, s, gain, w_in, conv_w, a_log, dt_bias, norm_w, w_out, gain_post, ffn):
    width = GDN_HEADS * GDN_HEAD_DIM
    w = jnp.concatenate([w_in[:, :4 * width], _pad_cols(w_in[:, 4 * width:], LANES)], axis=1)
    layout = ((0, 3 * width, 0), (3 * width, width, None), (4 * width, LANES, None))
    qkv, z, small = _proj_conv(x, b, s, gain, w, conv_w, None, layout)
    o = _gdn_core(qkv.reshape(b, s, 3 * width), z.reshape(b, s, width), small.reshape(b, s, LANES),
                  a_log, dt_bias, norm_w)
    return _outproj_ffn(o.reshape(b * s, width), x, gain_post, w_out, *ffn)


def _ssd_mixer(x, b, s, gain, w_in, conv_w, conv_b, dt_bias, a_log, d_skip, norm_w, w_out, gain_post, ffn):
    di = SSD_D_INNER
    conv_ch = di + 2 * SSD_GROUPS * SSD_STATE
    w = jnp.concatenate([w_in[:, :di + conv_ch], _pad_cols(w_in[:, di + conv_ch:], LANES)], axis=1)
    layout = ((0, di, None), (di, conv_ch, 0), (di + conv_ch, LANES, None))
    z, xbc, dt_raw = _proj_conv(x, b, s, gain, w, conv_w, conv_b, layout)
    o = _ssd_core(z.reshape(b, s, di), xbc.reshape(b, s, conv_ch), dt_raw.reshape(b, s, LANES),
                  dt_bias, a_log, d_skip, norm_w)
    return _outproj_ffn(o.reshape(b * s, di), x, gain_post, w_out, *ffn)


def kernel(x, positions, l0_norms, l0_ffn_w_in, l0_ffn_w_out, a_w_in, a_b_in, a_sinks, a_w_out, l1_norms, l1_ffn_w_in, l1_ffn_w_out, b_w_in, b_conv_w, b_A_log, b_dt_bias, b_norm, b_w_out, l2_norms, l2_ffn_w_in, l2_ffn_w_out, c_w_in, c_conv_w, c_conv_b, c_dt_bias, c_A_log, c_D, c_norm, c_w_out, l3_norms, l3_ffn_w_in, l3_ffn_w_out, d_w_in, d_w_out):
    b, s, d = x.shape
    t = b * s
    pos_b = jnp.broadcast_to(positions.astype(F32).reshape(t, 1), (t, LANES))
    inv_freq = ROPE_THETA ** (-jnp.arange(0, ROPE_DIMS, 2, dtype=F32) / ROPE_DIMS)
    freq = jnp.tile(inv_freq, LANES // inv_freq.shape[0]).reshape(1, LANES)

    mixers = (
        lambda h, n, f: _swa_mixer(h, b, s, n[2], pos_b, freq, a_w_in, a_b_in, a_sinks, a_w_out, n[3], f),
        lambda h, n, f: _gdn_mixer(h, b, s, n[2], b_w_in, b_conv_w, b_A_log, b_dt_bias, b_norm, b_w_out, n[3], f),
        lambda h, n, f: _ssd_mixer(h, b, s, n[2], c_w_in, c_conv_w, c_conv_b, c_dt_bias, c_A_log, c_D, c_norm,
                                   c_w_out, n[3], f),
        lambda h, n, f: _dilated_mixer(h, b, s, n[2], pos_b, freq, d_w_in, d_w_out, n[3], f),
    )
    layers = ((l0_norms, l0_ffn_w_in, l0_ffn_w_out), (l1_norms, l1_ffn_w_in, l1_ffn_w_out),
              (l2_norms, l2_ffn_w_in, l2_ffn_w_out), (l3_norms, l3_ffn_w_in, l3_ffn_w_out))
    h = x.reshape(t, d)
    for i, (norms, ffn_w_in, ffn_w_out) in enumerate(layers):
        norms = norms.astype(F32)
        h = _ffn(h, norms[0:2], ffn_w_in[0], ffn_w_out[0])
        h = mixers[i % len(mixers)](h, norms, (norms[4:6], ffn_w_in[1], ffn_w_out[1]))
    return h.reshape(b, s, d)
```

```python
import functools
import math

import jax
import jax.numpy as jnp
from jax import lax
from jax.experimental import pallas as pl
from jax.experimental.pallas import tpu as pltpu

F32 = jnp.float32
BF16 = jnp.bfloat16

D_MODEL = 1024
D_FF = 2816
NORM_EPS = 1e-6
ROPE_THETA = 500000.0
ROPE_DIMS = 16
HEAD_DIM = 64
BLOCK = 128
NEG_INF = -1e30

SWA_HEADS, SWA_KV_HEADS, SWA_WINDOW = 16, 2, 128
GDN_HEADS, GDN_HEAD_DIM, GDN_CONV, GDN_CHUNK = 8, 128, 4, 64
SSD_D_INNER, SSD_HEAD_DIM, SSD_GROUPS, SSD_STATE, SSD_CONV, SSD_CHUNK = 2048, 64, 4, 128, 4, 128
SSD_HEADS = SSD_D_INNER // SSD_HEAD_DIM
DIL_PATTERN = ((128, 1), (512, 4), (2048, 16))
DIL_HEADS, DIL_KV_HEADS = 16, 4

LANES = 128
CARRY_ROWS = 8
VMEM_LIMIT = 56 * 1024 * 1024
HI = lax.Precision.HIGHEST

TM_FFN = 512
TM_PROJ = 512
FF_CHUNK = 1536
ATTN_BLOCKS = 2


def _params(sem):
    return pltpu.CompilerParams(dimension_semantics=sem, vmem_limit_bytes=VMEM_LIMIT)


def _resident(shape):
    nd = len(shape)
    return pl.BlockSpec(shape, lambda *_: (0,) * nd, pipeline_mode=pl.Buffered(1))


def _rmsnorm(x, gain):
    return x * lax.rsqrt(jnp.mean(x * x, axis=-1, keepdims=True) + NORM_EPS) * gain


def _silu(x):
    half = 0.5 * x
    return half * (jnp.tanh(half) + 1.0)


def _softplus(x):
    return jnp.maximum(x, 0.0) + jnp.log(1.0 + jnp.exp(-jnp.abs(x)))


def _dot(a, b):
    return jnp.dot(a, b, preferred_element_type=F32)


def _dot_nt(a, b, precision=None):
    return lax.dot_general(a, b, (((1,), (1,)), ((), ())), preferred_element_type=F32, precision=precision)


def _dot_tn(a, b):
    return lax.dot_general(a, b, (((0,), (0,)), ((), ())), preferred_element_type=F32)


def _eye(n, dtype=F32):
    return (lax.broadcasted_iota(jnp.int32, (n, n), 0) == lax.broadcasted_iota(jnp.int32, (n, n), 1)).astype(dtype)


def _tril(n, dtype=F32):
    return (lax.broadcasted_iota(jnp.int32, (n, n), 0) >= lax.broadcasted_iota(jnp.int32, (n, n), 1)).astype(dtype)


def _ffn_value(x, g_ref, win_ref, wout_ref, act_ref):
    h = _rmsnorm(x, g_ref[0:1, :]).astype(BF16)
    for lo in range(0, D_FF, FF_CHUNK):
        width = min(FF_CHUNK, D_FF - lo)
        gate = _dot(h, win_ref[:, lo:lo + width])
        up = _dot(h, win_ref[:, D_FF + lo:D_FF + lo + width])
        act_ref[:, lo:lo + width] = (_silu(gate) * up).astype(BF16)
    y = _dot(act_ref[...], wout_ref[...])
    return x + 0.5 * _rmsnorm(y, g_ref[1:2, :])


def _ffn_specs():
    return [_resident((2, D_MODEL)), _resident((D_MODEL, 2 * D_FF)), _resident((D_FF, D_MODEL))]


def _ffn_body(x_ref, g_ref, win_ref, wout_ref, o_ref, act_ref):
    o_ref[...] = _ffn_value(x_ref[...], g_ref, win_ref, wout_ref, act_ref)


def _ffn(x, gains, w_in, w_out):
    t = x.shape[0]
    tm = min(TM_FFN, t)
    return pl.pallas_call(
        _ffn_body,
        grid=(t // tm,),
        in_specs=[pl.BlockSpec((tm, D_MODEL), lambda i: (i, 0))] + _ffn_specs(),
        out_specs=pl.BlockSpec((tm, D_MODEL), lambda i: (i, 0)),
        out_shape=jax.ShapeDtypeStruct((t, D_MODEL), F32),
        scratch_shapes=[pltpu.VMEM((tm, D_FF), BF16)],
        compiler_params=_params(("parallel",)),
        name="ffn",
    )(x, gains, w_in.astype(BF16), w_out.astype(BF16))


def _outproj_ffn_body(y_ref, x_ref, gp_ref, wo_ref, g_ref, win_ref, wout_ref, o_ref, act_ref):
    x = x_ref[...] + _rmsnorm(_dot(y_ref[...], wo_ref[...]), gp_ref[...])
    o_ref[...] = _ffn_value(x, g_ref, win_ref, wout_ref, act_ref)


def _outproj_ffn(y, x, gain, w_out, ffn_gains, ffn_w_in, ffn_w_out):
    t, k = y.shape
    tm = min(TM_FFN, t)
    return pl.pallas_call(
        _outproj_ffn_body,
        grid=(t // tm,),
        in_specs=[pl.BlockSpec((tm, k), lambda i: (i, 0)),
                  pl.BlockSpec((tm, D_MODEL), lambda i: (i, 0)),
                  _resident((1, D_MODEL)),
                  _resident((k, D_MODEL))] + _ffn_specs(),
        out_specs=pl.BlockSpec((tm, D_MODEL), lambda i: (i, 0)),
        out_shape=jax.ShapeDtypeStruct((t, D_MODEL), F32),
        scratch_shapes=[pltpu.VMEM((tm, D_FF), BF16)],
        compiler_params=_params(("parallel",)),
        name="outproj_ffn",
    )(y, x, gain.reshape(1, D_MODEL), w_out.astype(BF16), ffn_gains, ffn_w_in.astype(BF16), ffn_w_out.astype(BF16))


def _rope_tables(pos_ref, freq_ref):
    ang = pos_ref[...] * freq_ref[...]
    d = lax.broadcasted_iota(jnp.int32, ang.shape, 1) % HEAD_DIM
    half = ROPE_DIMS // 2
    cos, sin = jnp.cos(ang), jnp.sin(ang)
    c = jnp.where(d < ROPE_DIMS, cos, 1.0)
    s_lo = jnp.where(d < half, -sin, 0.0)
    s_hi = jnp.where((d >= half) & (d < ROPE_DIMS), sin, 0.0)
    return c, s_lo, s_hi


def _rope(y, tables):
    c, s_lo, s_hi = tables
    half = ROPE_DIMS // 2
    return y * c + pltpu.roll(y, LANES - half, 1) * s_lo + pltpu.roll(y, half, 1) * s_hi


PROJ_CHUNK = 512
CONV_CHUNK = 256


def _proj_attn_body(*refs, layout, q_scale, has_bias, with_ffn):
    x_ref, refs = refs[0], refs[1:]
    if with_ffn:
        (fg_ref, win_ref, wout_ref), refs = refs[:3], refs[3:]
    (g_ref, pos_ref, freq_ref, w_ref), refs = refs[:4], refs[4:]
    if has_bias:
        b_ref, refs = refs[0], refs[1:]
    if with_ffn:
        x_out_ref, refs = refs[0], refs[1:]
    out_refs, tmp_ref = refs[:len(layout)], refs[len(layout)]
    rows = x_ref.shape[0]
    x = x_ref[...]
    if with_ffn:
        x = _ffn_value(x, fg_ref, win_ref, wout_ref, refs[len(layout) + 1])
        x_out_ref[...] = x
    h = _rmsnorm(x, g_ref[...]).astype(BF16)
    k_tables = _rope_tables(pos_ref, freq_ref)
    q_tables = tuple(t * q_scale for t in k_tables)
    for (kind, start, width, dilation), o_ref in zip(layout, out_refs):
        chunk = min(width, PROJ_CHUNK)
        for lo in range(0, width, chunk):
            y = _dot(h, w_ref[:, start + lo:start + lo + chunk])
            if has_bias:
                y = y + b_ref[:, start + lo:start + lo + chunk]
            for sub in range(0, chunk, LANES):
                ys = y[:, sub:sub + LANES]
                if kind == "q":
                    ys = _rope(ys, q_tables)
                elif kind == "k":
                    ys = _rope(ys, k_tables)
                if dilation == 1:
                    o_ref[0, :, lo + sub:lo + sub + LANES] = ys.astype(o_ref.dtype)
                else:
                    tmp_ref[sub // LANES] = ys
            if dilation > 1:
                for sub in range(0, chunk, LANES):
                    for r in range(dilation):
                        o_ref[r, :, lo + sub:lo + sub + LANES] = tmp_ref[
                            sub // LANES, pl.ds(r, rows // dilation, stride=dilation), :].astype(o_ref.dtype)


def _proj_attn(x, b, s, gain, pos_b, freq, w_in, b_in, layout, ffn=None):
    n = w_in.shape[1]
    tm = min(TM_PROJ, s)
    steps = s // tm
    has_bias = b_in is not None
    with_ffn = ffn is not None
    body = functools.partial(_proj_attn_body, layout=layout, q_scale=HEAD_DIM ** -0.5, has_bias=has_bias,
                             with_ffn=with_ffn)
    row = lambda width: pl.BlockSpec((tm, width), lambda bi, i: (bi * steps + i, 0))
    in_specs, args = [row(D_MODEL)], [x]
    if with_ffn:
        in_specs += _ffn_specs()
        args += [ffn[0], ffn[1].astype(BF16), ffn[2].astype(BF16)]
    in_specs += [_resident((1, D_MODEL)), row(LANES), _resident((1, LANES)), _resident((D_MODEL, n))]
    args += [gain.reshape(1, D_MODEL), pos_b, freq, w_in.astype(BF16)]
    if has_bias:
        in_specs.append(_resident((1, n)))
        args.append(b_in.reshape(1, n).astype(F32))
    out_specs = [pl.BlockSpec((None, d, tm // d, w), lambda bi, i: (bi, 0, i, 0)) for _, _, w, d in layout]
    out_shape = [jax.ShapeDtypeStruct((b, d, s // d, w), BF16) for _, _, w, d in layout]
    scratch = [pltpu.VMEM((PROJ_CHUNK // LANES, tm, LANES), F32)]
    if with_ffn:
        out_specs = [row(D_MODEL)] + out_specs
        out_shape = [jax.ShapeDtypeStruct((b * s, D_MODEL), F32)] + out_shape
        scratch.append(pltpu.VMEM((tm, D_FF), BF16))
    return pl.pallas_call(
        body,
        grid=(b, steps),
        in_specs=in_specs,
        out_specs=out_specs,
        out_shape=out_shape,
        scratch_shapes=scratch,
        compiler_params=_params(("parallel", "parallel")),
        name="ffn_proj_attn" if with_ffn else "proj_attn",
    )(*args)


def _proj_conv_body(*refs, layout, conv_width, has_bias):
    x_ref, g_ref, w_ref, convw_ref = refs[:4]
    refs = refs[4:]
    if has_bias:
        convb_ref, refs = refs[0], refs[1:]
    out_refs, (carry_ref, buf_ref) = refs[:len(layout)], refs[len(layout):]
    rows = x_ref.shape[0]

    @pl.when(pl.program_id(1) == 0)
    def _():
        carry_ref[...] = jnp.zeros_like(carry_ref)

    h = _rmsnorm(x_ref[...], g_ref[...]).astype(BF16)
    slot = 0
    for (start, width, conv_col), o_ref in zip(layout, out_refs):
        chunk = PROJ_CHUNK if width % PROJ_CHUNK == 0 else LANES
        if conv_col is not None:
            chunk = CONV_CHUNK
        for lo in range(0, width, chunk):
            y = _dot(h, w_ref[:, start + lo:start + lo + chunk])
            if conv_col is None:
                o_ref[:, lo:lo + chunk] = y
                continue
            cols = slice(conv_col + lo, conv_col + lo + chunk)
            buf = buf_ref.at[slot % buf_ref.shape[0]]
            slot += 1
            buf[0:CARRY_ROWS, 0:chunk] = carry_ref[:, cols]
            buf[CARRY_ROWS:CARRY_ROWS + rows, 0:chunk] = y
            carry_ref[:, cols] = buf[rows:rows + CARRY_ROWS, 0:chunk]
            acc = None
            for i in range(conv_width):
                first = CARRY_ROWS - (conv_width - 1) + i
                term = buf[first:first + rows, 0:chunk] * convw_ref[i:i + 1, cols]
                acc = term if acc is None else acc + term
            if has_bias:
                acc = acc + convb_ref[:, cols]
            o_ref[:, lo:lo + chunk] = _silu(acc)


def _proj_conv(x, b, s, gain, w_in, conv_w, conv_b, layout):
    n = w_in.shape[1]
    conv_ch = conv_w.shape[1]
    tm = min(TM_PROJ, s)
    steps = s // tm
    has_bias = conv_b is not None
    row = lambda width: pl.BlockSpec((tm, width), lambda bi, i: (bi * steps + i, 0))
    in_specs = [row(D_MODEL), _resident((1, D_MODEL)), _resident((D_MODEL, n)), _resident(conv_w.shape)]
    args = [x, gain.reshape(1, D_MODEL), w_in.astype(BF16), conv_w.astype(F32)]
    if has_bias:
        in_specs.append(_resident((1, conv_ch)))
        args.append(conv_b.reshape(1, conv_ch).astype(F32))
    return pl.pallas_call(
        functools.partial(_proj_conv_body, layout=layout, conv_width=conv_w.shape[0], has_bias=has_bias),
        grid=(b, steps),
        in_specs=in_specs,
        out_specs=[row(w) for _, w, _ in layout],
        out_shape=[jax.ShapeDtypeStruct((b * s, w), F32) for _, w, _ in layout],
        scratch_shapes=[pltpu.VMEM((CARRY_ROWS, conv_ch), F32),
                        pltpu.VMEM((2, CARRY_ROWS + tm, PROJ_CHUNK), F32)],
        compiler_params=_params(("parallel", "arbitrary")),
        name="proj_conv",
    )(*args)


def _attn_body(*refs, n_heads, kv_heads, n_back, n_blocks, with_sink):
    if with_sink:
        sink_ref, q_ref, kc_ref, kp_ref, vc_ref, vp_ref, o_ref = refs
    else:
        q_ref, kc_ref, kp_ref, vc_ref, vp_ref, o_ref, lse_ref = refs
    gq = n_heads // kv_heads
    assert gq % 2 == 0 and kv_heads % 2 == 0
    kw = 2 * BLOCK
    step = pl.program_id(2)
    k_all = jnp.concatenate([kp_ref[...], kc_ref[...]], axis=0)
    v_all = jnp.concatenate([vp_ref[...], vc_ref[...]], axis=0)
    rows = lax.broadcasted_iota(jnp.int32, (BLOCK, 2 * kw), 0) + BLOCK
    cols = lax.broadcasted_iota(jnp.int32, (BLOCK, 2 * kw), 1) % kw
    band = (rows - cols >= 0) & (rows - cols <= n_back)
    lane = lax.broadcasted_iota(jnp.int32, (BLOCK, LANES), 1)
    low = lane < HEAD_DIM
    key_low = lax.broadcasted_iota(jnp.int32, (kw, LANES), 1) < HEAD_DIM
    zeros_b = jnp.zeros((kw, LANES), BF16)
    ones_low, ones_high = key_low.astype(BF16), jnp.logical_not(key_low).astype(BF16)
    for j in range(n_blocks):
        r0 = j * BLOCK
        has_prev = (step * n_blocks + j) > 0
        visible = band & ((cols >= BLOCK) | has_prev)
        keys, values = [], []
        for t in range(kv_heads // 2):
            k_tile = k_all[r0:r0 + kw, t * LANES:(t + 1) * LANES]
            v_tile = v_all[r0:r0 + kw, t * LANES:(t + 1) * LANES]
            k_swap, v_swap = pltpu.roll(k_tile, HEAD_DIM, 1), pltpu.roll(v_tile, HEAD_DIM, 1)
            for first in (True, False):
                k_dup = jnp.where(key_low == first, k_tile, k_swap)
                v_dup = jnp.where(key_low == first, v_tile, v_swap)
                keys.append(jnp.concatenate([jnp.where(key_low, k_dup, zeros_b),
                                             jnp.where(key_low, zeros_b, k_dup)], axis=0))
                values.append(jnp.concatenate(
                    [jnp.concatenate([jnp.where(key_low, v_dup, zeros_b), ones_low], axis=1),
                     jnp.concatenate([jnp.where(key_low, zeros_b, v_dup), ones_high], axis=1)], axis=0))
        pairs = range(n_heads // 2)
        kv_of = [2 * p // gq for p in pairs]
        scores = [jnp.where(visible, _dot_nt(q_ref[r0:r0 + BLOCK, p * LANES:(p + 1) * LANES], keys[kv_of[p]]), NEG_INF)
                  for p in pairs]
        max_a = [jnp.max(s[:, :kw], axis=-1, keepdims=True) for s in scores]
        max_b = [jnp.max(s[:, kw:], axis=-1, keepdims=True) for s in scores]
        probs = [jnp.concatenate([jnp.exp(s[:, :kw] - ma), jnp.exp(s[:, kw:] - mb)], axis=1).astype(BF16)
                 for s, ma, mb in zip(scores, max_a, max_b)]
        accs = [_dot(pr, values[kv_of[p]]) for p, pr in zip(pairs, probs)]
        lse_tile = jnp.zeros((BLOCK, LANES), F32)
        for p in pairs:
            acc, l = accs[p][:, :LANES], accs[p][:, LANES:]
            m = jnp.where(low, max_a[p], max_b[p])
            if with_sink:
                sink = jnp.where(low, sink_ref[2 * p], sink_ref[2 * p + 1])
                m_all = jnp.maximum(m, sink)
                rescale = jnp.exp(m - m_all)
                out = acc * (rescale / (l * rescale + jnp.exp(sink - m_all)))
            else:
                out = acc / l
                lse_tile = jnp.where((lane == 2 * p) | (lane == HEAD_DIM + 2 * p + 1), m + jnp.log(l), lse_tile)
            o_ref[r0:r0 + BLOCK, p * LANES:(p + 1) * LANES] = out.astype(o_ref.dtype)
        if not with_sink:
            lse_ref[r0:r0 + BLOCK, :] = lse_tile


def _lse_lane_of_head(h):
    return h if h % 2 == 0 else HEAD_DIM + h


def _attention(q, k, v, *, n_heads, kv_heads, n_back, sinks=None):
    b, d, length, qc = q.shape
    kc = k.shape[-1]
    assert length % BLOCK == 0
    n_blocks = ATTN_BLOCKS if length % (ATTN_BLOCKS * BLOCK) == 0 else 1
    tq = n_blocks * BLOCK
    with_sink = sinks is not None

    def cur(width):
        return pl.BlockSpec((None, None, tq, width), lambda bi, r, i: (bi, r, i, 0))

    def prev(width):
        return pl.BlockSpec((None, None, BLOCK, width), lambda bi, r, i: (bi, r, jnp.maximum(i * n_blocks - 1, 0), 0))

    in_specs = [cur(qc), cur(kc), prev(kc), cur(kc), prev(kc)]
    args = [q, k, k, v, v]
    if with_sink:
        in_specs = [pl.BlockSpec(memory_space=pltpu.SMEM)] + in_specs
        args = [sinks.astype(F32)] + args
        out_specs = cur(qc)
        out_shape = jax.ShapeDtypeStruct((b, d, length, qc), BF16)
    else:
        out_specs = [cur(qc), cur(LANES)]
        out_shape = [jax.ShapeDtypeStruct((b, d, length, qc), BF16),
                     jax.ShapeDtypeStruct((b, d, length, LANES), F32)]
    body = functools.partial(_attn_body, n_heads=n_heads, kv_heads=kv_heads, n_back=n_back,
                             n_blocks=n_blocks, with_sink=with_sink)
    return pl.pallas_call(
        body,
        grid=(b, d, length // tq),
        in_specs=in_specs,
        out_specs=out_specs,
        out_shape=out_shape,
        compiler_params=_params(("parallel", "parallel", "parallel")),
        name="attn_sink" if with_sink else "attn_lse",
    )(*args)


def _outproj_merge_body(*refs, dilations, n_heads):
    n_groups = len(dilations)
    o_refs, lse_refs = refs[:n_groups], refs[n_groups:2 * n_groups]
    x_ref, g_ref, w_ref, fg_ref, win_ref, wout_ref, out_ref, o_nat_ref, lse_nat_ref, act_ref = refs[2 * n_groups:]
    rows = x_ref.shape[0]

    def natural(ref, scratch, d):
        if d == 1:
            return ref[0].astype(F32)
        slabs = scratch.shape[0]
        for r in range(d):
            for c in range(slabs):
                scratch[c, pl.ds(r, rows // d, stride=d), :] = ref[r, :, c * LANES:(c + 1) * LANES].astype(F32)
        return jnp.concatenate([scratch[c] for c in range(slabs)], axis=1) if slabs > 1 else scratch[0]

    lses = [natural(lse_refs[g], lse_nat_ref.at[g:g + 1], d) for g, d in enumerate(dilations)]
    top = functools.reduce(jnp.maximum, lses)
    exps = [jnp.exp(s - top) for s in lses]
    den = functools.reduce(jnp.add, exps)
    n = o_refs[0].shape[-1]
    lane = lax.broadcasted_iota(jnp.int32, (LANES, n), 0)
    head_of_col = lax.broadcasted_iota(jnp.int32, (LANES, n), 1) // HEAD_DIM
    expand = functools.reduce(jnp.logical_or, [(lane == _lse_lane_of_head(h)) & (head_of_col == h)
                                               for h in range(n_heads)]).astype(BF16)
    merged = None
    for e, o_ref, d in zip(exps, o_refs, dilations):
        weights = jnp.concatenate(_split(e / den), axis=1)
        term = _dot(weights, jnp.concatenate([expand, expand], axis=0)) * natural(o_ref, o_nat_ref, d)
        merged = term if merged is None else merged + term
    x = x_ref[...] + _rmsnorm(_dot(merged.astype(BF16), w_ref[...]), g_ref[...])
    out_ref[...] = _ffn_value(x, fg_ref, win_ref, wout_ref, act_ref)


def _outproj_merge_ffn(outs, lses, x, b, s, gain, w_out, n_heads, ffn_gains, ffn_w_in, ffn_w_out):
    k = outs[0].shape[-1]
    dilations = tuple(o.shape[1] for o in outs)
    tm = min(TM_FFN, s)
    steps = s // tm
    grouped = lambda d, width: pl.BlockSpec((None, d, tm // d, width), lambda bi, i: (bi, 0, i, 0))
    row = pl.BlockSpec((tm, D_MODEL), lambda bi, i: (bi * steps + i, 0))
    return pl.pallas_call(
        functools.partial(_outproj_merge_body, dilations=dilations, n_heads=n_heads),
        grid=(b, steps),
        in_specs=[grouped(d, k) for d in dilations] + [grouped(d, LANES) for d in dilations]
                 + [row, _resident((1, D_MODEL)), _resident((k, D_MODEL))] + _ffn_specs(),
        out_specs=row,
        out_shape=jax.ShapeDtypeStruct((b * s, D_MODEL), F32),
        scratch_shapes=[pltpu.VMEM((k // LANES, tm, LANES), F32), pltpu.VMEM((len(dilations), tm, LANES), F32),
                        pltpu.VMEM((tm, D_FF), BF16)],
        compiler_params=_params(("parallel", "parallel")),
        name="outproj_merge_ffn",
    )(*outs, *lses, x, gain.reshape(1, D_MODEL), w_out.astype(BF16), ffn_gains, ffn_w_in.astype(BF16),
      ffn_w_out.astype(BF16))


def _l2norm(x):
    return x * lax.rsqrt(jnp.sum(x * x, axis=-1, keepdims=True) + NORM_EPS)


def _split(a):
    hi = a.astype(BF16)
    return hi, (a - hi.astype(F32)).astype(BF16)


def _dot_split(a, b):
    (ah, al), (bh, bl) = a, b
    return _dot(ah, bh) + (_dot(ah, bl) + _dot(al, bh))


def _unit_lower_inverses(strict_lowers):
    n = strict_lowers[0].shape[0]
    ri = lax.broadcasted_iota(jnp.int32, (n, n), 0)
    ci = lax.broadcasted_iota(jnp.int32, (n, n), 1)
    invs = None
    for level in range(int(math.log2(n))):
        joins = ((ri >> (level + 1)) == (ci >> (level + 1))) & ((ri >> level) != (ci >> level))
        couplings = [jnp.where(joins, l, 0.0) for l in strict_lowers]
        if invs is None:
            invs = [_eye(n) - c for c in couplings]
            continue
        inv_splits = [_split(inv) for inv in invs]
        right = [_dot_split(_split(c), d) for c, d in zip(couplings, inv_splits)]
        yield
        invs = [inv - _dot_split(d, _split(r)) for inv, d, r in zip(invs, inv_splits, right)]
        yield
    return invs


def _gdn_body(qkv_ref, z_ref, small_ref, alog_ref, dtb_ref, normw_ref, o_ref, state_ref):
    c = GDN_CHUNK
    nh, dk = GDN_HEADS, GDN_HEAD_DIM
    width = nh * dk
    rows = qkv_ref.shape[1]
    chunks = range(rows // c)

    @pl.when(pl.program_id(1) == 0)
    def _():
        state_ref[...] = jnp.zeros_like(state_ref)

    small = small_ref[0]
    beta_all = jax.nn.sigmoid(small)
    g_all = -jnp.exp(alog_ref[...]) * _softplus(small + dtb_ref[...])
    tril = _tril(c)
    gcum_alls = [jnp.dot(tril, g_all[j * c:(j + 1) * c], preferred_element_type=F32, precision=HI)
                 for j in chunks]
    gcum_rows = [_dot_nt(_eye(LANES), g, precision=HI) for g in gcum_alls]
    ri = lax.broadcasted_iota(jnp.int32, (c, c), 0)
    ci = lax.broadcasted_iota(jnp.int32, (c, c), 1)
    causal, strict = ri >= ci, ri > ci
    def act(c0, j):
        return qkv_ref[0, j * c:(j + 1) * c, c0:c0 + dk]

    def prepare(j):
        heads = range(nh)
        qs = [_l2norm(act(h * dk, j)) * (dk ** -0.5) for h in heads]
        yield
        ks = [_l2norm(act(width + h * dk, j)) for h in heads]
        yield
        vs = [act(2 * width + h * dk, j) for h in heads]
        betas = [beta_all[j * c:(j + 1) * c, h:h + 1] for h in heads]
        gcums = [gcum_alls[j][:, nh + h:nh + h + 1] for h in heads]
        decays = [jnp.exp(jnp.where(causal, gcums[h] - gcum_rows[j][nh + h:nh + h + 1, :], -jnp.inf)) for h in heads]
        yield
        k_bs = [k.astype(BF16) for k in ks]
        q_bs = [q.astype(BF16) for q in qs]
        lowers = [jnp.where(strict, _dot_nt(k_bs[h], k_bs[h]) * decays[h] * betas[h], 0.0) for h in heads]
        yield
        intras = [jnp.where(causal, _dot_nt(q_bs[h], k_bs[h]) * decays[h], 0.0).astype(BF16) for h in heads]
        yield
        invs = yield from _unit_lower_inverses(lowers)
        rhss = [jnp.concatenate([vs[h] * betas[h], ks[h] * (betas[h] * jnp.exp(gcums[h]))], axis=-1) for h in heads]
        yield
        sols = [_dot_split(_split(invs[h]), _split(rhss[h])) for h in heads]
        yield
        g_lasts = [g[c - 1:c, :] for g in gcums]
        return dict(
            us=[sol[:, :dk] for sol in sols],
            w_bs=[sol[:, dk:].astype(BF16) for sol in sols],
            q_ins=[(qs[h] * jnp.exp(gcums[h])).astype(BF16) for h in heads],
            k_ends=[(ks[h] * jnp.exp(g_lasts[h] - gcums[h])).astype(BF16) for h in heads],
            g_lasts=g_lasts, intras=intras)

    def finish(j, prep, states):
        heads = range(nh)
        state_bs = [s.astype(BF16) for s in states]
        v_new_bs = [(prep["us"][h] - _dot(prep["w_bs"][h], state_bs[h])).astype(BF16) for h in heads]
        outs = [_dot(prep["q_ins"][h], state_bs[h]) + _dot(prep["intras"][h], v_new_bs[h]) for h in heads]
        new_states = [states[h] * jnp.exp(prep["g_lasts"][h]) + _dot_tn(prep["k_ends"][h], v_new_bs[h]) for h in heads]
        for h in heads:
            gate = _silu(z_ref[0, j * c:(j + 1) * c, h * dk:(h + 1) * dk])
            o_ref[0, j * c:(j + 1) * c, h * dk:(h + 1) * dk] = (
                _rmsnorm(outs[h], normw_ref[...]) * gate).astype(o_ref.dtype)
        return new_states

    states = [state_ref[h] for h in range(nh)]
    pipelines = [prepare(j) for j in chunks]
    done = 0
    tick = 0
    while done < len(pipelines):
        for j in range(done, len(pipelines)):
            if tick < j * GDN_STAGE_SKEW:
                break
            try:
                next(pipelines[j])
            except StopIteration as stop:
                assert j == done
                states = finish(j, stop.value, states)
                done += 1
        tick += 1
    for h in range(nh):
        state_ref[h] = states[h]


GDN_STAGE_SKEW = 2
GDN_CHUNKS_PER_STEP = 4


def _gdn_core(qkv, z, small, a_log, dt_bias, norm_w):
    b, s, _ = qkv.shape
    nh, dk = GDN_HEADS, GDN_HEAD_DIM
    width = nh * dk
    rows = GDN_CHUNKS_PER_STEP * GDN_CHUNK
    assert s % rows == 0
    pad = lambda a: jnp.zeros((1, LANES), F32).at[0, nh:2 * nh].set(a.astype(F32))
    return pl.pallas_call(
        _gdn_body,
        grid=(b, s // rows),
        in_specs=[pl.BlockSpec((1, rows, 3 * width), lambda bi, i: (bi, i, 0)),
                  pl.BlockSpec((1, rows, width), lambda bi, i: (bi, i, 0)),
                  pl.BlockSpec((1, rows, LANES), lambda bi, i: (bi, i, 0)),
                  _resident((1, LANES)),
                  _resident((1, LANES)),
                  _resident((1, dk))],
        out_specs=pl.BlockSpec((1, rows, width), lambda bi, i: (bi, i, 0)),
        out_shape=jax.ShapeDtypeStruct((b, s, width), BF16),
        scratch_shapes=[pltpu.VMEM((nh, dk, dk), F32)],
        compiler_params=_params(("parallel", "arbitrary")),
        name="gdn",
    )(qkv, z, small, pad(a_log), pad(dt_bias), norm_w.reshape(1, dk).astype(F32))


def _ssd_body(z_ref, xbc_ref, dt_ref, dtb_ref, alog_ref, dskip_ref, normw_ref, expand_ref, o_ref, state_ref):
    c = SSD_CHUNK
    di, ng, p = SSD_D_INNER, SSD_GROUPS, SSD_HEAD_DIM
    heads_per_group = SSD_HEADS // ng
    pairs_per_group = heads_per_group // 2
    group_width = di // ng

    @pl.when(pl.program_id(1) == 0)
    def _():
        state_ref[...] = jnp.zeros_like(state_ref)

    def conv(c0):
        return xbc_ref[0, :, c0:c0 + LANES]

    dt_all = _softplus(dt_ref[0] + dtb_ref[...])
    adt = dt_all * -jnp.exp(alog_ref[...])
    acum_all = jnp.dot(_tril(c), adt, preferred_element_type=F32, precision=HI)
    acum_rows = _dot_nt(_eye(LANES), acum_all, precision=HI)
    from_start_all = jnp.exp(acum_all)
    to_end_all = jnp.exp(acum_all[c - 1:c, :] - acum_all)
    ri = lax.broadcasted_iota(jnp.int32, (c, c), 0)
    ci = lax.broadcasted_iota(jnp.int32, (c, c), 1)
    causal = ri >= ci
    low_half = lax.broadcasted_iota(jnp.int32, (c, LANES), 1) < p
    eye_b = _eye(LANES, BF16)
    groups, pairs = range(ng), range(SSD_HEADS // 2)
    group_of = [pr // pairs_per_group for pr in pairs]

    def to_channels(all_heads):
        return _dot(jnp.concatenate(_split(all_heads), axis=1), expand_ref[...])

    dt_ch, from_start_ch, to_end_ch = to_channels(dt_all), to_channels(from_start_all), to_channels(to_end_all)

    def per_head(channels, pr):
        return channels[:, pr * LANES:(pr + 1) * LANES]

    def seg(hd, cb):
        diff = acum_all[:, hd:hd + 1] - acum_rows[hd:hd + 1, :]
        return (cb * jnp.exp(jnp.where(causal, diff, -jnp.inf))).astype(BF16)

    bms = [conv(di + g * SSD_STATE).astype(BF16) for g in groups]
    cms = [conv(di + ng * SSD_STATE + g * SSD_STATE).astype(BF16) for g in groups]
    cbs = [_dot_nt(cms[g], bms[g]) for g in groups]
    bm_ts = [_dot_nt(eye_b, bms[g]).astype(BF16) for g in groups]
    xs = [conv(pr * LANES) for pr in pairs]
    x_dts = [xs[pr] * per_head(dt_ch, pr) for pr in pairs]
    x_dt_bs = [v.astype(BF16) for v in x_dts]
    y_diags = [jnp.where(low_half, _dot(seg(2 * pr, cbs[group_of[pr]]), x_dt_bs[pr]),
                         _dot(seg(2 * pr + 1, cbs[group_of[pr]]), x_dt_bs[pr])) for pr in pairs]
    states = [state_ref[pr] for pr in pairs]
    y_offs = [_dot(cms[group_of[pr]], states[pr].astype(BF16)) * per_head(from_start_ch, pr) for pr in pairs]
    for pr in pairs:
        decay_all = per_head(from_start_ch, pr)[c - 1:c, :]
        state_ref[pr] = states[pr] * decay_all + _dot(bm_ts[group_of[pr]],
                                                      (x_dts[pr] * per_head(to_end_ch, pr)).astype(BF16))
    ys = [(y_diags[pr] + y_offs[pr] + xs[pr] * dskip_ref[:, pr * LANES:(pr + 1) * LANES])
          * _silu(z_ref[0, :, pr * LANES:(pr + 1) * LANES]) for pr in pairs]
    sumsqs = [jnp.sum(y * y, axis=-1, keepdims=True) for y in ys]
    for g in groups:
        members = [pr for pr in pairs if group_of[pr] == g]
        scale = lax.rsqrt(functools.reduce(jnp.add, [sumsqs[pr] for pr in members]) * (1.0 / group_width) + NORM_EPS)
        for pr in members:
            c0 = pr * LANES
            o_ref[0, :, c0:c0 + LANES] = (ys[pr] * scale * normw_ref[:, c0:c0 + LANES]).astype(o_ref.dtype)


def _ssd_core(z, xbc, dt_raw, dt_bias, a_log, d_skip, norm_w):
    b, s, conv_ch = xbc.shape
    c = SSD_CHUNK
    di = SSD_D_INNER
    pad = lambda a: jnp.zeros((1, LANES), F32).at[0, :SSD_HEADS].set(a.astype(F32))
    expand = (jnp.arange(2 * LANES)[:, None] % LANES == jnp.arange(di)[None, :] // SSD_HEAD_DIM).astype(BF16)
    return pl.pallas_call(
        _ssd_body,
        grid=(b, s // c),
        in_specs=[pl.BlockSpec((1, c, di), lambda bi, i: (bi, i, 0)),
                  pl.BlockSpec((1, c, conv_ch), lambda bi, i: (bi, i, 0)),
                  pl.BlockSpec((1, c, LANES), lambda bi, i: (bi, i, 0)),
                  _resident((1, LANES)),
                  _resident((1, LANES)),
                  _resident((1, di)),
                  _resident((1, di)),
                  _resident((2 * LANES, di))],
        out_specs=pl.BlockSpec((1, c, di), lambda bi, i: (bi, i, 0)),
        out_shape=jax.ShapeDtypeStruct((b, s, di), BF16),
        scratch_shapes=[pltpu.VMEM((SSD_HEADS // 2, SSD_STATE, LANES), F32)],
        compiler_params=_params(("parallel", "arbitrary")),
        name="ssd",
    )(z, xbc, dt_raw, pad(dt_bias), pad(a_log),
      jnp.repeat(d_skip.astype(F32), SSD_HEAD_DIM).reshape(1, di), norm_w.reshape(1, di).astype(F32), expand)


def _pad_cols(w, n):
    return jnp.pad(w, ((0, 0), (0, n - w.shape[1])))


def _swa_mixer(x, b, s, gain, pos_b, freq, w_in, b_in, sinks, w_out, gain_post, ffn_a, ffn):
    nq, nkv = SWA_HEADS * HEAD_DIM, SWA_KV_HEADS * HEAD_DIM
    layout = (("q", 0, nq, 1), ("k", nq, nkv, 1), ("v", nq + nkv, nkv, 1))
    x, q, k, v = _proj_attn(x, b, s, gain, pos_b, freq, w_in, b_in, layout, ffn=ffn_a)
    o = _attention(q, k, v, n_heads=SWA_HEADS, kv_heads=SWA_KV_HEADS, n_back=SWA_WINDOW - 1, sinks=sinks)
    return _outproj_ffn(o.reshape(b * s, nq), x, gain_post, w_out, *ffn)


def _dilated_mixer(x, b, s, gain, pos_b, freq, w_in, w_out, gain_post, ffn_a, ffn):
    nq, nkv = DIL_HEADS * HEAD_DIM, DIL_KV_HEADS * HEAD_DIM
    per_group = nq + 2 * nkv
    layout = []
    for gi, (_, dilation) in enumerate(DIL_PATTERN):
        base = gi * per_group
        layout += [("q", base, nq, dilation), ("k", base + nq, nkv, dilation), ("v", base + nq + nkv, nkv, dilation)]
    x, *qkv = _proj_attn(x, b, s, gain, pos_b, freq, w_in, None, tuple(layout), ffn=ffn_a)
    group_outs, group_lses = [], []
    for gi, (window, dilation) in enumerate(DIL_PATTERN):
        o, lse = _attention(*qkv[3 * gi:3 * gi + 3], n_heads=DIL_HEADS, kv_heads=DIL_KV_HEADS,
                            n_back=window // dilation)
        group_outs.append(o)
        group_lses.append(lse)
    return _outproj_merge_ffn(group_outs, group_lses, x, b, s, gain_post, w_out, DIL_HEADS, *ffn)


def _gdn_mixer(x, b, s, gain, w_in, conv_w, a_log, dt_bias, norm_w, w_out, gain_post, ffn_a, ffn):
    width = GDN_HEADS * GDN_HEAD_DIM
    x = _ffn(x, *ffn_a)
    w = jnp.concatenate([w_in[:, :4 * width], _pad_cols(w_in[:, 4 * width:], LANES)], axis=1)
    layout = ((0, 3 * width, 0), (3 * width, width, None), (4 * width, LANES, None))
    qkv, z, small = _proj_conv(x, b, s, gain, w, conv_w, None, layout)
    o = _gdn_core(qkv.reshape(b, s, 3 * width), z.reshape(b, s, width), small.reshape(b, s, LANES),
                  a_log, dt_bias, norm_w)
    return _outproj_ffn(o.reshape(b * s, width), x, gain_post, w_out, *ffn)


def _ssd_mixer(x, b, s, gain, w_in, conv_w, conv_b, dt_bias, a_log, d_skip, norm_w, w_out, gain_post, ffn_a, ffn):
    di = SSD_D_INNER
    x = _ffn(x, *ffn_a)
    conv_ch = di + 2 * SSD_GROUPS * SSD_STATE
    w = jnp.concatenate([w_in[:, :di + conv_ch], _pad_cols(w_in[:, di + conv_ch:], LANES)], axis=1)
    layout = ((0, di, None), (di, conv_ch, 0), (di + conv_ch, LANES, None))
    z, xbc, dt_raw = _proj_conv(x, b, s, gain, w, conv_w, conv_b, layout)
    o = _ssd_core(z.reshape(b, s, di), xbc.reshape(b, s, conv_ch), dt_raw.reshape(b, s, LANES),
                  dt_bias, a_log, d_skip, norm_w)
    return _outproj_ffn(o.reshape(b * s, di), x, gain_post, w_out, *ffn)


def kernel(x, positions, l0_norms, l0_ffn_w_in, l0_ffn_w_out, a_w_in, a_b_in, a_sinks, a_w_out, l1_norms, l1_ffn_w_in, l1_ffn_w_out, b_w_in, b_conv_w, b_A_log, b_dt_bias, b_norm, b_w_out, l2_norms, l2_ffn_w_in, l2_ffn_w_out, c_w_in, c_conv_w, c_conv_b, c_dt_bias, c_A_log, c_D, c_norm, c_w_out, l3_norms, l3_ffn_w_in, l3_ffn_w_out, d_w_in, d_w_out):
    b, s, d = x.shape
    t = b * s
    pos_b = jnp.broadcast_to(positions.astype(F32).reshape(t, 1), (t, LANES))
    inv_freq = ROPE_THETA ** (-jnp.arange(0, ROPE_DIMS, 2, dtype=F32) / ROPE_DIMS)
    freq = jnp.tile(inv_freq, LANES // inv_freq.shape[0]).reshape(1, LANES)

    mixers = (
        lambda h, n, fa, fb: _swa_mixer(h, b, s, n[2], pos_b, freq, a_w_in, a_b_in, a_sinks, a_w_out, n[3], fa, fb),
        lambda h, n, fa, fb: _gdn_mixer(h, b, s, n[2], b_w_in, b_conv_w, b_A_log, b_dt_bias, b_norm, b_w_out, n[3],
                                        fa, fb),
        lambda h, n, fa, fb: _ssd_mixer(h, b, s, n[2], c_w_in, c_conv_w, c_conv_b, c_dt_bias, c_A_log, c_D, c_norm,
                                        c_w_out, n[3], fa, fb),
        lambda h, n, fa, fb: _dilated_mixer(h, b, s, n[2], pos_b, freq, d_w_in, d_w_out, n[3], fa, fb),
    )
    layers = ((l0_norms, l0_ffn_w_in, l0_ffn_w_out), (l1_norms, l1_ffn_w_in, l1_ffn_w_out),
              (l2_norms, l2_ffn_w_in, l2_ffn_w_out), (l3_norms, l3_ffn_w_in, l3_ffn_w_out))
    h = x.reshape(t, d)
    for i, (norms, ffn_w_in, ffn_w_out) in enumerate(layers):
        norms = norms.astype(F32)
        h = mixers[i % len(mixers)](h, norms, (norms[0:2], ffn_w_in[0], ffn_w_out[0]),
                                    (norms[4:6], ffn_w_in[1], ffn_w_out[1]))
    return h.reshape(b, s, d)
```

```python
import functools
import math

import jax
import jax.numpy as jnp
from jax import lax
from jax.experimental import pallas as pl
from jax.experimental.pallas import tpu as pltpu

F32 = jnp.float32
BF16 = jnp.bfloat16

D_MODEL = 1024
D_FF = 2816
NORM_EPS = 1e-6
ROPE_THETA = 500000.0
ROPE_DIMS = 16
HEAD_DIM = 64
BLOCK = 128
NEG_INF = -1e30

SWA_HEADS, SWA_KV_HEADS, SWA_WINDOW = 16, 2, 128
GDN_HEADS, GDN_HEAD_DIM, GDN_CONV, GDN_CHUNK = 8, 128, 4, 64
SSD_D_INNER, SSD_HEAD_DIM, SSD_GROUPS, SSD_STATE, SSD_CONV, SSD_CHUNK = 2048, 64, 4, 128, 4, 128
SSD_HEADS = SSD_D_INNER // SSD_HEAD_DIM
DIL_PATTERN = ((128, 1), (512, 4), (2048, 16))
DIL_HEADS, DIL_KV_HEADS = 16, 4

LANES = 128
CARRY_ROWS = 8
VMEM_LIMIT = 56 * 1024 * 1024
HI = lax.Precision.HIGHEST

TM_FFN = 512
TM_PROJ = 512
FF_CHUNK = 1536
ATTN_BLOCKS = 4
ATTN_PAIRS_PER_STAGE = 8


def _params(sem):
    return pltpu.CompilerParams(dimension_semantics=sem, vmem_limit_bytes=VMEM_LIMIT)


def _resident(shape):
    nd = len(shape)
    return pl.BlockSpec(shape, lambda *_: (0,) * nd, pipeline_mode=pl.Buffered(1))


def _rmsnorm(x, gain):
    return x * lax.rsqrt(jnp.mean(x * x, axis=-1, keepdims=True) + NORM_EPS) * gain


def _silu(x):
    half = 0.5 * x
    return half * (jnp.tanh(half) + 1.0)


def _softplus(x):
    return jnp.maximum(x, 0.0) + jnp.log(1.0 + jnp.exp(-jnp.abs(x)))


def _dot(a, b):
    return jnp.dot(a, b, preferred_element_type=F32)


def _dot_nt(a, b, precision=None):
    return lax.dot_general(a, b, (((1,), (1,)), ((), ())), preferred_element_type=F32, precision=precision)


def _dot_tn(a, b):
    return lax.dot_general(a, b, (((0,), (0,)), ((), ())), preferred_element_type=F32)


def _eye(n, dtype=F32):
    return (lax.broadcasted_iota(jnp.int32, (n, n), 0) == lax.broadcasted_iota(jnp.int32, (n, n), 1)).astype(dtype)


def _tril(n, dtype=F32):
    return (lax.broadcasted_iota(jnp.int32, (n, n), 0) >= lax.broadcasted_iota(jnp.int32, (n, n), 1)).astype(dtype)


def _ffn_value(x, g_ref, win_ref, wout_ref, act_ref):
    h = _rmsnorm(x, g_ref[0:1, :]).astype(BF16)
    for lo in range(0, D_FF, FF_CHUNK):
        width = min(FF_CHUNK, D_FF - lo)
        gate = _dot(h, win_ref[:, lo:lo + width])
        up = _dot(h, win_ref[:, D_FF + lo:D_FF + lo + width])
        act_ref[:, lo:lo + width] = (_silu(gate) * up).astype(BF16)
    y = _dot(act_ref[...], wout_ref[...])
    return x + 0.5 * _rmsnorm(y, g_ref[1:2, :])


def _ffn_specs():
    return [_resident((2, D_MODEL)), _resident((D_MODEL, 2 * D_FF)), _resident((D_FF, D_MODEL))]


def _ffn_body(x_ref, g_ref, win_ref, wout_ref, o_ref, act_ref):
    o_ref[...] = _ffn_value(x_ref[...], g_ref, win_ref, wout_ref, act_ref)


def _ffn(x, gains, w_in, w_out):
    t = x.shape[0]
    tm = min(TM_FFN, t)
    return pl.pallas_call(
        _ffn_body,
        grid=(t // tm,),
        in_specs=[pl.BlockSpec((tm, D_MODEL), lambda i: (i, 0))] + _ffn_specs(),
        out_specs=pl.BlockSpec((tm, D_MODEL), lambda i: (i, 0)),
        out_shape=jax.ShapeDtypeStruct((t, D_MODEL), F32),
        scratch_shapes=[pltpu.VMEM((tm, D_FF), BF16)],
        compiler_params=_params(("parallel",)),
        name="ffn",
    )(x, gains, w_in.astype(BF16), w_out.astype(BF16))


def _outproj_ffn_body(y_ref, x_ref, gp_ref, wo_ref, g_ref, win_ref, wout_ref, o_ref, act_ref):
    x = x_ref[...] + _rmsnorm(_dot(y_ref[...], wo_ref[...]), gp_ref[...])
    o_ref[...] = _ffn_value(x, g_ref, win_ref, wout_ref, act_ref)


def _outproj_ffn(y, x, gain, w_out, ffn_gains, ffn_w_in, ffn_w_out):
    t, k = y.shape
    tm = min(TM_FFN, t)
    return pl.pallas_call(
        _outproj_ffn_body,
        grid=(t // tm,),
        in_specs=[pl.BlockSpec((tm, k), lambda i: (i, 0)),
                  pl.BlockSpec((tm, D_MODEL), lambda i: (i, 0)),
                  _resident((1, D_MODEL)),
                  _resident((k, D_MODEL))] + _ffn_specs(),
        out_specs=pl.BlockSpec((tm, D_MODEL), lambda i: (i, 0)),
        out_shape=jax.ShapeDtypeStruct((t, D_MODEL), F32),
        scratch_shapes=[pltpu.VMEM((tm, D_FF), BF16)],
        compiler_params=_params(("parallel",)),
        name="outproj_ffn",
    )(y, x, gain.reshape(1, D_MODEL), w_out.astype(BF16), ffn_gains, ffn_w_in.astype(BF16), ffn_w_out.astype(BF16))


def _rope_tables(pos_ref, freq_ref):
    ang = pos_ref[...] * freq_ref[...]
    d = lax.broadcasted_iota(jnp.int32, ang.shape, 1) % HEAD_DIM
    half = ROPE_DIMS // 2
    cos, sin = jnp.cos(ang), jnp.sin(ang)
    c = jnp.where(d < ROPE_DIMS, cos, 1.0)
    s_lo = jnp.where(d < half, -sin, 0.0)
    s_hi = jnp.where((d >= half) & (d < ROPE_DIMS), sin, 0.0)
    return c, s_lo, s_hi


def _rope(y, tables):
    c, s_lo, s_hi = tables
    half = ROPE_DIMS // 2
    return y * c + pltpu.roll(y, LANES - half, 1) * s_lo + pltpu.roll(y, half, 1) * s_hi


PROJ_CHUNK = 512
CONV_CHUNK = 256


def _proj_attn_body(*refs, layout, q_scale, has_bias, with_ffn):
    x_ref, refs = refs[0], refs[1:]
    if with_ffn:
        (fg_ref, win_ref, wout_ref), refs = refs[:3], refs[3:]
    (g_ref, pos_ref, freq_ref, w_ref), refs = refs[:4], refs[4:]
    if has_bias:
        b_ref, refs = refs[0], refs[1:]
    if with_ffn:
        x_out_ref, refs = refs[0], refs[1:]
    out_refs, tmp_ref = refs[:len(layout)], refs[len(layout)]
    rows = x_ref.shape[0]
    x = x_ref[...]
    if with_ffn:
        x = _ffn_value(x, fg_ref, win_ref, wout_ref, refs[len(layout) + 1])
        x_out_ref[...] = x
    h = _rmsnorm(x, g_ref[...]).astype(BF16)
    k_tables = _rope_tables(pos_ref, freq_ref)
    q_tables = tuple(t * q_scale for t in k_tables)
    for (kind, start, width, dilation), o_ref in zip(layout, out_refs):
        chunk = min(width, PROJ_CHUNK)
        for lo in range(0, width, chunk):
            y = _dot(h, w_ref[:, start + lo:start + lo + chunk])
            if has_bias:
                y = y + b_ref[:, start + lo:start + lo + chunk]
            for sub in range(0, chunk, LANES):
                ys = y[:, sub:sub + LANES]
                if kind == "q":
                    ys = _rope(ys, q_tables)
                elif kind == "k":
                    ys = _rope(ys, k_tables)
                if dilation == 1:
                    o_ref[0, :, lo + sub:lo + sub + LANES] = ys.astype(o_ref.dtype)
                else:
                    tmp_ref[sub // LANES] = ys
            if dilation > 1:
                for sub in range(0, chunk, LANES):
                    for r in range(dilation):
                        o_ref[r, :, lo + sub:lo + sub + LANES] = tmp_ref[
                            sub // LANES, pl.ds(r, rows // dilation, stride=dilation), :].astype(o_ref.dtype)


def _proj_attn(x, b, s, gain, pos_b, freq, w_in, b_in, layout, ffn=None):
    n = w_in.shape[1]
    tm = min(TM_PROJ, s)
    steps = s // tm
    has_bias = b_in is not None
    with_ffn = ffn is not None
    body = functools.partial(_proj_attn_body, layout=layout, q_scale=HEAD_DIM ** -0.5, has_bias=has_bias,
                             with_ffn=with_ffn)
    row = lambda width: pl.BlockSpec((tm, width), lambda bi, i: (bi * steps + i, 0))
    in_specs, args = [row(D_MODEL)], [x]
    if with_ffn:
        in_specs += _ffn_specs()
        args += [ffn[0], ffn[1].astype(BF16), ffn[2].astype(BF16)]
    in_specs += [_resident((1, D_MODEL)), row(LANES), _resident((1, LANES)), _resident((D_MODEL, n))]
    args += [gain.reshape(1, D_MODEL), pos_b, freq, w_in.astype(BF16)]
    if has_bias:
        in_specs.append(_resident((1, n)))
        args.append(b_in.reshape(1, n).astype(F32))
    out_specs = [pl.BlockSpec((None, d, tm // d, w), lambda bi, i: (bi, 0, i, 0)) for _, _, w, d in layout]
    out_shape = [jax.ShapeDtypeStruct((b, d, s // d, w), BF16) for _, _, w, d in layout]
    scratch = [pltpu.VMEM((PROJ_CHUNK // LANES, tm, LANES), F32)]
    if with_ffn:
        out_specs = [row(D_MODEL)] + out_specs
        out_shape = [jax.ShapeDtypeStruct((b * s, D_MODEL), F32)] + out_shape
        scratch.append(pltpu.VMEM((tm, D_FF), BF16))
    return pl.pallas_call(
        body,
        grid=(b, steps),
        in_specs=in_specs,
        out_specs=out_specs,
        out_shape=out_shape,
        scratch_shapes=scratch,
        compiler_params=_params(("parallel", "parallel")),
        name="ffn_proj_attn" if with_ffn else "proj_attn",
    )(*args)


def _proj_conv_body(*refs, layout, conv_width, has_bias):
    x_ref, g_ref, w_ref, convw_ref = refs[:4]
    refs = refs[4:]
    if has_bias:
        convb_ref, refs = refs[0], refs[1:]
    out_refs, (carry_ref, buf_ref) = refs[:len(layout)], refs[len(layout):]
    rows = x_ref.shape[0]

    @pl.when(pl.program_id(1) == 0)
    def _():
        carry_ref[...] = jnp.zeros_like(carry_ref)

    h = _rmsnorm(x_ref[...], g_ref[...]).astype(BF16)
    slot = 0
    for (start, width, conv_col), o_ref in zip(layout, out_refs):
        chunk = PROJ_CHUNK if width % PROJ_CHUNK == 0 else LANES
        if conv_col is not None:
            chunk = CONV_CHUNK
        for lo in range(0, width, chunk):
            y = _dot(h, w_ref[:, start + lo:start + lo + chunk])
            if conv_col is None:
                o_ref[:, lo:lo + chunk] = y
                continue
            cols = slice(conv_col + lo, conv_col + lo + chunk)
            buf = buf_ref.at[slot % buf_ref.shape[0]]
            slot += 1
            buf[0:CARRY_ROWS, 0:chunk] = carry_ref[:, cols]
            buf[CARRY_ROWS:CARRY_ROWS + rows, 0:chunk] = y
            carry_ref[:, cols] = buf[rows:rows + CARRY_ROWS, 0:chunk]
            acc = None
            for i in range(conv_width):
                first = CARRY_ROWS - (conv_width - 1) + i
                term = buf[first:first + rows, 0:chunk] * convw_ref[i:i + 1, cols]
                acc = term if acc is None else acc + term
            if has_bias:
                acc = acc + convb_ref[:, cols]
            o_ref[:, lo:lo + chunk] = _silu(acc)


def _proj_conv(x, b, s, gain, w_in, conv_w, conv_b, layout):
    n = w_in.shape[1]
    conv_ch = conv_w.shape[1]
    tm = min(TM_PROJ, s)
    steps = s // tm
    has_bias = conv_b is not None
    row = lambda width: pl.BlockSpec((tm, width), lambda bi, i: (bi * steps + i, 0))
    in_specs = [row(D_MODEL), _resident((1, D_MODEL)), _resident((D_MODEL, n)), _resident(conv_w.shape)]
    args = [x, gain.reshape(1, D_MODEL), w_in.astype(BF16), conv_w.astype(F32)]
    if has_bias:
        in_specs.append(_resident((1, conv_ch)))
        args.append(conv_b.reshape(1, conv_ch).astype(F32))
    return pl.pallas_call(
        functools.partial(_proj_conv_body, layout=layout, conv_width=conv_w.shape[0], has_bias=has_bias),
        grid=(b, steps),
        in_specs=in_specs,
        out_specs=[row(w) for _, w, _ in layout],
        out_shape=[jax.ShapeDtypeStruct((b * s, w), F32) for _, w, _ in layout],
        scratch_shapes=[pltpu.VMEM((CARRY_ROWS, conv_ch), F32),
                        pltpu.VMEM((2, CARRY_ROWS + tm, PROJ_CHUNK), F32)],
        compiler_params=_params(("parallel", "arbitrary")),
        name="proj_conv",
    )(*args)


def _attn_body(*refs, n_heads, kv_heads, n_back, n_blocks, with_sink):
    if with_sink:
        sink_ref, q_ref, kc_ref, kp_ref, vc_ref, vp_ref, o_ref = refs
    else:
        q_ref, kc_ref, kp_ref, vc_ref, vp_ref, o_ref, lse_ref = refs
    gq = n_heads // kv_heads
    assert gq % 2 == 0 and kv_heads % 2 == 0
    kw = 2 * BLOCK
    step = pl.program_id(2)
    k_all = jnp.concatenate([kp_ref[...], kc_ref[...]], axis=0)
    v_all = jnp.concatenate([vp_ref[...], vc_ref[...]], axis=0)
    rows = lax.broadcasted_iota(jnp.int32, (BLOCK, 2 * kw), 0) + BLOCK
    cols = lax.broadcasted_iota(jnp.int32, (BLOCK, 2 * kw), 1) % kw
    band = (rows - cols >= 0) & (rows - cols <= n_back)
    lane = lax.broadcasted_iota(jnp.int32, (BLOCK, LANES), 1)
    low = lane < HEAD_DIM
    key_low = lax.broadcasted_iota(jnp.int32, (kw, LANES), 1) < HEAD_DIM
    zeros_b = jnp.zeros((kw, LANES), BF16)
    ones_low, ones_high = key_low.astype(BF16), jnp.logical_not(key_low).astype(BF16)
    for j in range(n_blocks):
        r0 = j * BLOCK
        has_prev = (step * n_blocks + j) > 0
        visible = band & ((cols >= BLOCK) | has_prev)
        keys, values = [], []
        for t in range(kv_heads // 2):
            k_tile = k_all[r0:r0 + kw, t * LANES:(t + 1) * LANES]
            v_tile = v_all[r0:r0 + kw, t * LANES:(t + 1) * LANES]
            k_swap, v_swap = pltpu.roll(k_tile, HEAD_DIM, 1), pltpu.roll(v_tile, HEAD_DIM, 1)
            for first in (True, False):
                k_dup = jnp.where(key_low == first, k_tile, k_swap)
                v_dup = jnp.where(key_low == first, v_tile, v_swap)
                keys.append(jnp.concatenate([jnp.where(key_low, k_dup, zeros_b),
                                             jnp.where(key_low, zeros_b, k_dup)], axis=0))
                values.append(jnp.concatenate(
                    [jnp.concatenate([jnp.where(key_low, v_dup, zeros_b), ones_low], axis=1),
                     jnp.concatenate([jnp.where(key_low, zeros_b, v_dup), ones_high], axis=1)], axis=0))
        kv_of = [2 * p // gq for p in range(n_heads // 2)]
        lse_tile = jnp.zeros((BLOCK, LANES), F32)
        for first_pair in range(0, n_heads // 2, ATTN_PAIRS_PER_STAGE):
            pairs = range(first_pair, first_pair + ATTN_PAIRS_PER_STAGE)
            scores = {p: jnp.where(visible, _dot_nt(q_ref[r0:r0 + BLOCK, p * LANES:(p + 1) * LANES], keys[kv_of[p]]),
                                   NEG_INF) for p in pairs}
            max_a = {p: jnp.max(scores[p][:, :kw], axis=-1, keepdims=True) for p in pairs}
            max_b = {p: jnp.max(scores[p][:, kw:], axis=-1, keepdims=True) for p in pairs}
            probs = {p: jnp.concatenate([jnp.exp(scores[p][:, :kw] - max_a[p]), jnp.exp(scores[p][:, kw:] - max_b[p])],
                                        axis=1).astype(BF16) for p in pairs}
            accs = {p: _dot(probs[p], values[kv_of[p]]) for p in pairs}
            for p in pairs:
                acc, l = accs[p][:, :LANES], accs[p][:, LANES:]
                m = jnp.where(low, max_a[p], max_b[p])
                if with_sink:
                    sink = jnp.where(low, sink_ref[2 * p], sink_ref[2 * p + 1])
                    m_all = jnp.maximum(m, sink)
                    rescale = jnp.exp(m - m_all)
                    out = acc * (rescale / (l * rescale + jnp.exp(sink - m_all)))
                else:
                    out = acc / l
                    lse_tile = jnp.where((lane == 2 * p) | (lane == HEAD_DIM + 2 * p + 1), m + jnp.log(l), lse_tile)
                o_ref[r0:r0 + BLOCK, p * LANES:(p + 1) * LANES] = out.astype(o_ref.dtype)
        if not with_sink:
            lse_ref[r0:r0 + BLOCK, :] = lse_tile


def _lse_lane_of_head(h):
    return h if h % 2 == 0 else HEAD_DIM + h


def _attention(q, k, v, *, n_heads, kv_heads, n_back, sinks=None):
    b, d, length, qc = q.shape
    kc = k.shape[-1]
    assert length % BLOCK == 0
    n_blocks = ATTN_BLOCKS if length % (ATTN_BLOCKS * BLOCK) == 0 else 1
    tq = n_blocks * BLOCK
    with_sink = sinks is not None

    def cur(width):
        return pl.BlockSpec((None, None, tq, width), lambda bi, r, i: (bi, r, i, 0))

    def prev(width):
        return pl.BlockSpec((None, None, BLOCK, width), lambda bi, r, i: (bi, r, jnp.maximum(i * n_blocks - 1, 0), 0))

    in_specs = [cur(qc), cur(kc), prev(kc), cur(kc), prev(kc)]
    args = [q, k, k, v, v]
    if with_sink:
        in_specs = [pl.BlockSpec(memory_space=pltpu.SMEM)] + in_specs
        args = [sinks.astype(F32)] + args
        out_specs = cur(qc)
        out_shape = jax.ShapeDtypeStruct((b, d, length, qc), BF16)
    else:
        out_specs = [cur(qc), cur(LANES)]
        out_shape = [jax.ShapeDtypeStruct((b, d, length, qc), BF16),
                     jax.ShapeDtypeStruct((b, d, length, LANES), F32)]
    body = functools.partial(_attn_body, n_heads=n_heads, kv_heads=kv_heads, n_back=n_back,
                             n_blocks=n_blocks, with_sink=with_sink)
    return pl.pallas_call(
        body,
        grid=(b, d, length // tq),
        in_specs=in_specs,
        out_specs=out_specs,
        out_shape=out_shape,
        compiler_params=_params(("parallel", "parallel", "parallel")),
        name="attn_sink" if with_sink else "attn_lse",
    )(*args)


def _outproj_merge_body(*refs, dilations, n_heads):
    n_groups = len(dilations)
    o_refs, lse_refs = refs[:n_groups], refs[n_groups:2 * n_groups]
    x_ref, g_ref, w_ref, fg_ref, win_ref, wout_ref, out_ref, o_nat_ref, lse_nat_ref, act_ref = refs[2 * n_groups:]
    rows = x_ref.shape[0]

    def natural(ref, scratch, d):
        if d == 1:
            return ref[0].astype(F32)
        slabs = scratch.shape[0]
        for r in range(d):
            for c in range(slabs):
                scratch[c, pl.ds(r, rows // d, stride=d), :] = ref[r, :, c * LANES:(c + 1) * LANES].astype(F32)
        return jnp.concatenate([scratch[c] for c in range(slabs)], axis=1) if slabs > 1 else scratch[0]

    lses = [natural(lse_refs[g], lse_nat_ref.at[g:g + 1], d) for g, d in enumerate(dilations)]
    top = functools.reduce(jnp.maximum, lses)
    exps = [jnp.exp(s - top) for s in lses]
    den = functools.reduce(jnp.add, exps)
    n = o_refs[0].shape[-1]
    lane = lax.broadcasted_iota(jnp.int32, (LANES, n), 0)
    head_of_col = lax.broadcasted_iota(jnp.int32, (LANES, n), 1) // HEAD_DIM
    expand = functools.reduce(jnp.logical_or, [(lane == _lse_lane_of_head(h)) & (head_of_col == h)
                                               for h in range(n_heads)]).astype(BF16)
    merged = None
    for e, o_ref, d in zip(exps, o_refs, dilations):
        weights = jnp.concatenate(_split(e / den), axis=1)
        term = _dot(weights, jnp.concatenate([expand, expand], axis=0)) * natural(o_ref, o_nat_ref, d)
        merged = term if merged is None else merged + term
    x = x_ref[...] + _rmsnorm(_dot(merged.astype(BF16), w_ref[...]), g_ref[...])
    out_ref[...] = _ffn_value(x, fg_ref, win_ref, wout_ref, act_ref)


def _outproj_merge_ffn(outs, lses, x, b, s, gain, w_out, n_heads, ffn_gains, ffn_w_in, ffn_w_out):
    k = outs[0].shape[-1]
    dilations = tuple(o.shape[1] for o in outs)
    tm = min(TM_FFN, s)
    steps = s // tm
    grouped = lambda d, width: pl.BlockSpec((None, d, tm // d, width), lambda bi, i: (bi, 0, i, 0))
    row = pl.BlockSpec((tm, D_MODEL), lambda bi, i: (bi * steps + i, 0))
    return pl.pallas_call(
        functools.partial(_outproj_merge_body, dilations=dilations, n_heads=n_heads),
        grid=(b, steps),
        in_specs=[grouped(d, k) for d in dilations] + [grouped(d, LANES) for d in dilations]
                 + [row, _resident((1, D_MODEL)), _resident((k, D_MODEL))] + _ffn_specs(),
        out_specs=row,
        out_shape=jax.ShapeDtypeStruct((b * s, D_MODEL), F32),
        scratch_shapes=[pltpu.VMEM((k // LANES, tm, LANES), F32), pltpu.VMEM((len(dilations), tm, LANES), F32),
                        pltpu.VMEM((tm, D_FF), BF16)],
        compiler_params=_params(("parallel", "parallel")),
        name="outproj_merge_ffn",
    )(*outs, *lses, x, gain.reshape(1, D_MODEL), w_out.astype(BF16), ffn_gains, ffn_w_in.astype(BF16),
      ffn_w_out.astype(BF16))


def _l2norm(x):
    return x * lax.rsqrt(jnp.sum(x * x, axis=-1, keepdims=True) + NORM_EPS)


def _split(a):
    hi = a.astype(BF16)
    return hi, (a - hi.astype(F32)).astype(BF16)


def _dot_split(a, b):
    (ah, al), (bh, bl) = a, b
    return _dot(ah, bh) + (_dot(ah, bl) + _dot(al, bh))


def _unit_lower_inverses(strict_lowers):
    n = strict_lowers[0].shape[0]
    ri = lax.broadcasted_iota(jnp.int32, (n, n), 0)
    ci = lax.broadcasted_iota(jnp.int32, (n, n), 1)
    invs = None
    for level in range(int(math.log2(n))):
        joins = ((ri >> (level + 1)) == (ci >> (level + 1))) & ((ri >> level) != (ci >> level))
        couplings = [jnp.where(joins, l, 0.0) for l in strict_lowers]
        if invs is None:
            invs = [_eye(n) - c for c in couplings]
            continue
        inv_splits = [_split(inv) for inv in invs]
        right = [_dot_split(_split(c), d) for c, d in zip(couplings, inv_splits)]
        yield
        invs = [inv - _dot_split(d, _split(r)) for inv, d, r in zip(invs, inv_splits, right)]
        yield
    return invs


def _gdn_body(qkv_ref, z_ref, small_ref, alog_ref, dtb_ref, normw_ref, o_ref, state_ref):
    c = GDN_CHUNK
    nh, dk = GDN_HEADS, GDN_HEAD_DIM
    width = nh * dk
    rows = qkv_ref.shape[1]
    chunks = range(rows // c)

    @pl.when(pl.program_id(1) == 0)
    def _():
        state_ref[...] = jnp.zeros_like(state_ref)

    small = small_ref[0]
    beta_all = jax.nn.sigmoid(small)
    g_all = -jnp.exp(alog_ref[...]) * _softplus(small + dtb_ref[...])
    tril = _tril(c)
    gcum_alls = [jnp.dot(tril, g_all[j * c:(j + 1) * c], preferred_element_type=F32, precision=HI)
                 for j in chunks]
    gcum_rows = [_dot_nt(_eye(LANES), g, precision=HI) for g in gcum_alls]
    ri = lax.broadcasted_iota(jnp.int32, (c, c), 0)
    ci = lax.broadcasted_iota(jnp.int32, (c, c), 1)
    causal, strict = ri >= ci, ri > ci
    def act(c0, j):
        return qkv_ref[0, j * c:(j + 1) * c, c0:c0 + dk]

    def prepare(j):
        heads = range(nh)
        qs = [_l2norm(act(h * dk, j)) * (dk ** -0.5) for h in heads]
        yield
        ks = [_l2norm(act(width + h * dk, j)) for h in heads]
        yield
        vs = [act(2 * width + h * dk, j) for h in heads]
        betas = [beta_all[j * c:(j + 1) * c, h:h + 1] for h in heads]
        gcums = [gcum_alls[j][:, nh + h:nh + h + 1] for h in heads]
        decays = [jnp.exp(jnp.where(causal, gcums[h] - gcum_rows[j][nh + h:nh + h + 1, :], -jnp.inf)) for h in heads]
        yield
        k_bs = [k.astype(BF16) for k in ks]
        q_bs = [q.astype(BF16) for q in qs]
        lowers = [jnp.where(strict, _dot_nt(k_bs[h], k_bs[h]) * decays[h] * betas[h], 0.0) for h in heads]
        yield
        intras = [jnp.where(causal, _dot_nt(q_bs[h], k_bs[h]) * decays[h], 0.0).astype(BF16) for h in heads]
        yield
        invs = yield from _unit_lower_inverses(lowers)
        rhss = [jnp.concatenate([vs[h] * betas[h], ks[h] * (betas[h] * jnp.exp(gcums[h]))], axis=-1) for h in heads]
        yield
        sols = [_dot_split(_split(invs[h]), _split(rhss[h])) for h in heads]
        yield
        g_lasts = [g[c - 1:c, :] for g in gcums]
        return dict(
            us=[sol[:, :dk] for sol in sols],
            w_bs=[sol[:, dk:].astype(BF16) for sol in sols],
            q_ins=[(qs[h] * jnp.exp(gcums[h])).astype(BF16) for h in heads],
            k_ends=[(ks[h] * jnp.exp(g_lasts[h] - gcums[h])).astype(BF16) for h in heads],
            g_lasts=g_lasts, intras=intras)

    def finish(j, prep, states):
        heads = range(nh)
        state_bs = [s.astype(BF16) for s in states]
        v_new_bs = [(prep["us"][h] - _dot(prep["w_bs"][h], state_bs[h])).astype(BF16) for h in heads]
        outs = [_dot(prep["q_ins"][h], state_bs[h]) + _dot(prep["intras"][h], v_new_bs[h]) for h in heads]
        new_states = [states[h] * jnp.exp(prep["g_lasts"][h]) + _dot_tn(prep["k_ends"][h], v_new_bs[h]) for h in heads]
        for h in heads:
            gate = _silu(z_ref[0, j * c:(j + 1) * c, h * dk:(h + 1) * dk])
            o_ref[0, j * c:(j + 1) * c, h * dk:(h + 1) * dk] = (
                _rmsnorm(outs[h], normw_ref[...]) * gate).astype(o_ref.dtype)
        return new_states

    states = [state_ref[h] for h in range(nh)]
    pipelines = [prepare(j) for j in chunks]
    done = 0
    tick = 0
    while done < len(pipelines):
        for j in range(done, len(pipelines)):
            if tick < j * GDN_STAGE_SKEW:
                break
            try:
                next(pipelines[j])
            except StopIteration as stop:
                assert j == done
                states = finish(j, stop.value, states)
                done += 1
        tick += 1
    for h in range(nh):
        state_ref[h] = states[h]


GDN_STAGE_SKEW = 2
GDN_CHUNKS_PER_STEP = 4


def _gdn_core(qkv, z, small, a_log, dt_bias, norm_w):
    b, s, _ = qkv.shape
    nh, dk = GDN_HEADS, GDN_HEAD_DIM
    width = nh * dk
    rows = GDN_CHUNKS_PER_STEP * GDN_CHUNK
    assert s % rows == 0
    pad = lambda a: jnp.zeros((1, LANES), F32).at[0, nh:2 * nh].set(a.astype(F32))
    return pl.pallas_call(
        _gdn_body,
        grid=(b, s // rows),
        in_specs=[pl.BlockSpec((1, rows, 3 * width), lambda bi, i: (bi, i, 0)),
                  pl.BlockSpec((1, rows, width), lambda bi, i: (bi, i, 0)),
                  pl.BlockSpec((1, rows, LANES), lambda bi, i: (bi, i, 0)),
                  _resident((1, LANES)),
                  _resident((1, LANES)),
                  _resident((1, dk))],
        out_specs=pl.BlockSpec((1, rows, width), lambda bi, i: (bi, i, 0)),
        out_shape=jax.ShapeDtypeStruct((b, s, width), BF16),
        scratch_shapes=[pltpu.VMEM((nh, dk, dk), F32)],
        compiler_params=_params(("parallel", "arbitrary")),
        name="gdn",
    )(qkv, z, small, pad(a_log), pad(dt_bias), norm_w.reshape(1, dk).astype(F32))


def _ssd_body(z_ref, xbc_ref, dt_ref, dtb_ref, alog_ref, dskip_ref, normw_ref, expand_ref, o_ref, state_ref):
    c = SSD_CHUNK
    di, ng, p = SSD_D_INNER, SSD_GROUPS, SSD_HEAD_DIM
    heads_per_group = SSD_HEADS // ng
    pairs_per_group = heads_per_group // 2
    group_width = di // ng

    @pl.when(pl.program_id(1) == 0)
    def _():
        state_ref[...] = jnp.zeros_like(state_ref)

    def conv(c0):
        return xbc_ref[0, :, c0:c0 + LANES]

    dt_all = _softplus(dt_ref[0] + dtb_ref[...])
    adt = dt_all * -jnp.exp(alog_ref[...])
    acum_all = jnp.dot(_tril(c), adt, preferred_element_type=F32, precision=HI)
    acum_rows = _dot_nt(_eye(LANES), acum_all, precision=HI)
    from_start_all = jnp.exp(acum_all)
    to_end_all = jnp.exp(acum_all[c - 1:c, :] - acum_all)
    ri = lax.broadcasted_iota(jnp.int32, (c, c), 0)
    ci = lax.broadcasted_iota(jnp.int32, (c, c), 1)
    causal = ri >= ci
    low_half = lax.broadcasted_iota(jnp.int32, (c, LANES), 1) < p
    eye_b = _eye(LANES, BF16)
    groups, pairs = range(ng), range(SSD_HEADS // 2)
    group_of = [pr // pairs_per_group for pr in pairs]

    def to_channels(all_heads):
        return _dot(jnp.concatenate(_split(all_heads), axis=1), expand_ref[...])

    dt_ch, from_start_ch, to_end_ch = to_channels(dt_all), to_channels(from_start_all), to_channels(to_end_all)

    def per_head(channels, pr):
        return channels[:, pr * LANES:(pr + 1) * LANES]

    def seg(hd, cb):
        diff = acum_all[:, hd:hd + 1] - acum_rows[hd:hd + 1, :]
        return (cb * jnp.exp(jnp.where(causal, diff, -jnp.inf))).astype(BF16)

    bms = [conv(di + g * SSD_STATE).astype(BF16) for g in groups]
    cms = [conv(di + ng * SSD_STATE + g * SSD_STATE).astype(BF16) for g in groups]
    cbs = [_dot_nt(cms[g], bms[g]) for g in groups]
    bm_ts = [_dot_nt(eye_b, bms[g]).astype(BF16) for g in groups]
    xs = [conv(pr * LANES) for pr in pairs]
    x_dts = [xs[pr] * per_head(dt_ch, pr) for pr in pairs]
    x_dt_bs = [v.astype(BF16) for v in x_dts]
    y_diags = [jnp.where(low_half, _dot(seg(2 * pr, cbs[group_of[pr]]), x_dt_bs[pr]),
                         _dot(seg(2 * pr + 1, cbs[group_of[pr]]), x_dt_bs[pr])) for pr in pairs]
    states = [state_ref[pr] for pr in pairs]
    y_offs = [_dot(cms[group_of[pr]], states[pr].astype(BF16)) * per_head(from_start_ch, pr) for pr in pairs]
    for pr in pairs:
        decay_all = per_head(from_start_ch, pr)[c - 1:c, :]
        state_ref[pr] = states[pr] * decay_all + _dot(bm_ts[group_of[pr]],
                                                      (x_dts[pr] * per_head(to_end_ch, pr)).astype(BF16))
    ys = [(y_diags[pr] + y_offs[pr] + xs[pr] * dskip_ref[:, pr * LANES:(pr + 1) * LANES])
          * _silu(z_ref[0, :, pr * LANES:(pr + 1) * LANES]) for pr in pairs]
    sumsqs = [jnp.sum(y * y, axis=-1, keepdims=True) for y in ys]
    for g in groups:
        members = [pr for pr in pairs if group_of[pr] == g]
        scale = lax.rsqrt(functools.reduce(jnp.add, [sumsqs[pr] for pr in members]) * (1.0 / group_width) + NORM_EPS)
        for pr in members:
            c0 = pr * LANES
            o_ref[0, :, c0:c0 + LANES] = (ys[pr] * scale * normw_ref[:, c0:c0 + LANES]).astype(o_ref.dtype)


def _ssd_core(z, xbc, dt_raw, dt_bias, a_log, d_skip, norm_w):
    b, s, conv_ch = xbc.shape
    c = SSD_CHUNK
    di = SSD_D_INNER
    pad = lambda a: jnp.zeros((1, LANES), F32).at[0, :SSD_HEADS].set(a.astype(F32))
    expand = (jnp.arange(2 * LANES)[:, None] % LANES == jnp.arange(di)[None, :] // SSD_HEAD_DIM).astype(BF16)
    return pl.pallas_call(
        _ssd_body,
        grid=(b, s // c),
        in_specs=[pl.BlockSpec((1, c, di), lambda bi, i: (bi, i, 0)),
                  pl.BlockSpec((1, c, conv_ch), lambda bi, i: (bi, i, 0)),
                  pl.BlockSpec((1, c, LANES), lambda bi, i: (bi, i, 0)),
                  _resident((1, LANES)),
                  _resident((1, LANES)),
                  _resident((1, di)),
                  _resident((1, di)),
                  _resident((2 * LANES, di))],
        out_specs=pl.BlockSpec((1, c, di), lambda bi, i: (bi, i, 0)),
        out_shape=jax.ShapeDtypeStruct((b, s, di), BF16),
        scratch_shapes=[pltpu.VMEM((SSD_HEADS // 2, SSD_STATE, LANES), F32)],
        compiler_params=_params(("parallel", "arbitrary")),
        name="ssd",
    )(z, xbc, dt_raw, pad(dt_bias), pad(a_log),
      jnp.repeat(d_skip.astype(F32), SSD_HEAD_DIM).reshape(1, di), norm_w.reshape(1, di).astype(F32), expand)


def _pad_cols(w, n):
    return jnp.pad(w, ((0, 0), (0, n - w.shape[1])))


def _swa_mixer(x, b, s, gain, pos_b, freq, w_in, b_in, sinks, w_out, gain_post, ffn_a, ffn):
    nq, nkv = SWA_HEADS * HEAD_DIM, SWA_KV_HEADS * HEAD_DIM
    layout = (("q", 0, nq, 1), ("k", nq, nkv, 1), ("v", nq + nkv, nkv, 1))
    x, q, k, v = _proj_attn(x, b, s, gain, pos_b, freq, w_in, b_in, layout, ffn=ffn_a)
    o = _attention(q, k, v, n_heads=SWA_HEADS, kv_heads=SWA_KV_HEADS, n_back=SWA_WINDOW - 1, sinks=sinks)
    return _outproj_ffn(o.reshape(b * s, nq), x, gain_post, w_out, *ffn)


def _dilated_mixer(x, b, s, gain, pos_b, freq, w_in, w_out, gain_post, ffn_a, ffn):
    nq, nkv = DIL_HEADS * HEAD_DIM, DIL_KV_HEADS * HEAD_DIM
    per_group = nq + 2 * nkv
    layout = []
    for gi, (_, dilation) in enumerate(DIL_PATTERN):
        base = gi * per_group
        layout += [("q", base, nq, dilation), ("k", base + nq, nkv, dilation), ("v", base + nq + nkv, nkv, dilation)]
    x, *qkv = _proj_attn(x, b, s, gain, pos_b, freq, w_in, None, tuple(layout), ffn=ffn_a)
    group_outs, group_lses = [], []
    for gi, (window, dilation) in enumerate(DIL_PATTERN):
        o, lse = _attention(*qkv[3 * gi:3 * gi + 3], n_heads=DIL_HEADS, kv_heads=DIL_KV_HEADS,
                            n_back=window // dilation)
        group_outs.append(o)
        group_lses.append(lse)
    return _outproj_merge_ffn(group_outs, group_lses, x, b, s, gain_post, w_out, DIL_HEADS, *ffn)


def _gdn_mixer(x, b, s, gain, w_in, conv_w, a_log, dt_bias, norm_w, w_out, gain_post, ffn_a, ffn):
    width = GDN_HEADS * GDN_HEAD_DIM
    x = _ffn(x, *ffn_a)
    w = jnp.concatenate([w_in[:, :4 * width], _pad_cols(w_in[:, 4 * width:], LANES)], axis=1)
    layout = ((0, 3 * width, 0), (3 * width, width, None), (4 * width, LANES, None))
    qkv, z, small = _proj_conv(x, b, s, gain, w, conv_w, None, layout)
    o = _gdn_core(qkv.reshape(b, s, 3 * width), z.reshape(b, s, width), small.reshape(b, s, LANES),
                  a_log, dt_bias, norm_w)
    return _outproj_ffn(o.reshape(b * s, width), x, gain_post, w_out, *ffn)


def _ssd_mixer(x, b, s, gain, w_in, conv_w, conv_b, dt_bias, a_log, d_skip, norm_w, w_out, gain_post, ffn_a, ffn):
    di = SSD_D_INNER
    x = _ffn(x, *ffn_a)
    conv_ch = di + 2 * SSD_GROUPS * SSD_STATE
    w = jnp.concatenate([w_in[:, :di + conv_ch], _pad_cols(w_in[:, di + conv_ch:], LANES)], axis=1)
    layout = ((0, di, None), (di, conv_ch, 0), (di + conv_ch, LANES, None))
    z, xbc, dt_raw = _proj_conv(x, b, s, gain, w, conv_w, conv_b, layout)
    o = _ssd_core(z.reshape(b, s, di), xbc.reshape(b, s, conv_ch), dt_raw.reshape(b, s, LANES),
                  dt_bias, a_log, d_skip, norm_w)
    return _outproj_ffn(o.reshape(b * s, di), x, gain_post, w_out, *ffn)


def kernel(x, positions, l0_norms, l0_ffn_w_in, l0_ffn_w_out, a_w_in, a_b_in, a_sinks, a_w_out, l1_norms, l1_ffn_w_in, l1_ffn_w_out, b_w_in, b_conv_w, b_A_log, b_dt_bias, b_norm, b_w_out, l2_norms, l2_ffn_w_in, l2_ffn_w_out, c_w_in, c_conv_w, c_conv_b, c_dt_bias, c_A_log, c_D, c_norm, c_w_out, l3_norms, l3_ffn_w_in, l3_ffn_w_out, d_w_in, d_w_out):
    b, s, d = x.shape
    t = b * s
    pos_b = jnp.broadcast_to(positions.astype(F32).reshape(t, 1), (t, LANES))
    inv_freq = ROPE_THETA ** (-jnp.arange(0, ROPE_DIMS, 2, dtype=F32) / ROPE_DIMS)
    freq = jnp.tile(inv_freq, LANES // inv_freq.shape[0]).reshape(1, LANES)

    mixers = (
        lambda h, n, fa, fb: _swa_mixer(h, b, s, n[2], pos_b, freq, a_w_in, a_b_in, a_sinks, a_w_out, n[3], fa, fb),
        lambda h, n, fa, fb: _gdn_mixer(h, b, s, n[2], b_w_in, b_conv_w, b_A_log, b_dt_bias, b_norm, b_w_out, n[3],
                                        fa, fb),
        lambda h, n, fa, fb: _ssd_mixer(h, b, s, n[2], c_w_in, c_conv_w, c_conv_b, c_dt_bias, c_A_log, c_D, c_norm,
                                        c_w_out, n[3], fa, fb),
        lambda h, n, fa, fb: _dilated_mixer(h, b, s, n[2], pos_b, freq, d_w_in, d_w_out, n[3], fa, fb),
    )
    layers = ((l0_norms, l0_ffn_w_in, l0_ffn_w_out), (l1_norms, l1_ffn_w_in, l1_ffn_w_out),
              (l2_norms, l2_ffn_w_in, l2_ffn_w_out), (l3_norms, l3_ffn_w_in, l3_ffn_w_out))
    h = x.reshape(t, d)
    for i, (norms, ffn_w_in, ffn_w_out) in enumerate(layers):
        norms = norms.astype(F32)
        h = mixers[i % len(mixers)](h, norms, (norms[0:2], ffn_w_in[0], ffn_w_out[0]),
                                    (norms[4:6], ffn_w_in[1], ffn_w_out[1]))
    return h.reshape(b, s, d)
```

```python
import functools
import math

import jax
import jax.numpy as jnp
from jax import lax
from jax.experimental import pallas as pl
from jax.experimental.pallas import tpu as pltpu

F32 = jnp.float32
BF16 = jnp.bfloat16

D_MODEL = 1024
D_FF = 2816
NORM_EPS = 1e-6
ROPE_THETA = 500000.0
ROPE_DIMS = 16
HEAD_DIM = 64
BLOCK = 128
NEG_INF = -1e30

SWA_HEADS, SWA_KV_HEADS, SWA_WINDOW = 16, 2, 128
GDN_HEADS, GDN_HEAD_DIM, GDN_CONV, GDN_CHUNK = 8, 128, 4, 64
SSD_D_INNER, SSD_HEAD_DIM, SSD_GROUPS, SSD_STATE, SSD_CONV, SSD_CHUNK = 2048, 64, 4, 128, 4, 128
SSD_HEADS = SSD_D_INNER // SSD_HEAD_DIM
DIL_PATTERN = ((128, 1), (512, 4), (2048, 16))
DIL_HEADS, DIL_KV_HEADS = 16, 4

LANES = 128
CARRY_ROWS = 8
VMEM_LIMIT = 56 * 1024 * 1024
HI = lax.Precision.HIGHEST

TM_FFN = 512
TM_PROJ = 512
FF_CHUNK = 1536
ATTN_BLOCKS = 4
ATTN_PAIRS_PER_STAGE = 8


def _params(sem):
    return pltpu.CompilerParams(dimension_semantics=sem, vmem_limit_bytes=VMEM_LIMIT)


def _resident(shape):
    nd = len(shape)
    return pl.BlockSpec(shape, lambda *_: (0,) * nd, pipeline_mode=pl.Buffered(1))


def _rmsnorm(x, gain):
    return x * lax.rsqrt(jnp.mean(x * x, axis=-1, keepdims=True) + NORM_EPS) * gain


def _silu(x):
    half = 0.5 * x
    return half * (jnp.tanh(half) + 1.0)


def _softplus(x):
    return jnp.maximum(x, 0.0) + jnp.log(1.0 + jnp.exp(-jnp.abs(x)))


def _dot(a, b):
    return jnp.dot(a, b, preferred_element_type=F32)


def _dot_nt(a, b, precision=None):
    return lax.dot_general(a, b, (((1,), (1,)), ((), ())), preferred_element_type=F32, precision=precision)


def _dot_tn(a, b):
    return lax.dot_general(a, b, (((0,), (0,)), ((), ())), preferred_element_type=F32)


def _eye(n, dtype=F32):
    return (lax.broadcasted_iota(jnp.int32, (n, n), 0) == lax.broadcasted_iota(jnp.int32, (n, n), 1)).astype(dtype)


def _tril(n, dtype=F32):
    return (lax.broadcasted_iota(jnp.int32, (n, n), 0) >= lax.broadcasted_iota(jnp.int32, (n, n), 1)).astype(dtype)


def _ffn_value(x, g_ref, win_ref, wout_ref, act_ref):
    h = _rmsnorm(x, g_ref[0:1, :]).astype(BF16)
    for lo in range(0, D_FF, FF_CHUNK):
        width = min(FF_CHUNK, D_FF - lo)
        gate = _dot(h, win_ref[:, lo:lo + width])
        up = _dot(h, win_ref[:, D_FF + lo:D_FF + lo + width])
        act_ref[:, lo:lo + width] = (_silu(gate) * up).astype(BF16)
    y = _dot(act_ref[...], wout_ref[...])
    return x + 0.5 * _rmsnorm(y, g_ref[1:2, :])


def _ffn_specs():
    return [_resident((2, D_MODEL)), _resident((D_MODEL, 2 * D_FF)), _resident((D_FF, D_MODEL))]


def _ffn_body(x_ref, g_ref, win_ref, wout_ref, o_ref, act_ref):
    o_ref[...] = _ffn_value(x_ref[...], g_ref, win_ref, wout_ref, act_ref)


def _ffn(x, gains, w_in, w_out):
    t = x.shape[0]
    tm = min(TM_FFN, t)
    return pl.pallas_call(
        _ffn_body,
        grid=(t // tm,),
        in_specs=[pl.BlockSpec((tm, D_MODEL), lambda i: (i, 0))] + _ffn_specs(),
        out_specs=pl.BlockSpec((tm, D_MODEL), lambda i: (i, 0)),
        out_shape=jax.ShapeDtypeStruct((t, D_MODEL), F32),
        scratch_shapes=[pltpu.VMEM((tm, D_FF), BF16)],
        compiler_params=_params(("parallel",)),
        name="ffn",
    )(x, gains, w_in.astype(BF16), w_out.astype(BF16))


def _outproj_ffn_body(y_ref, x_ref, gp_ref, wo_ref, g_ref, win_ref, wout_ref, o_ref, act_ref):
    x = x_ref[...] + _rmsnorm(_dot(y_ref[...], wo_ref[...]), gp_ref[...])
    o_ref[...] = _ffn_value(x, g_ref, win_ref, wout_ref, act_ref)


def _outproj_ffn(y, x, gain, w_out, ffn_gains, ffn_w_in, ffn_w_out):
    t, k = y.shape
    tm = min(TM_FFN, t)
    return pl.pallas_call(
        _outproj_ffn_body,
        grid=(t // tm,),
        in_specs=[pl.BlockSpec((tm, k), lambda i: (i, 0)),
                  pl.BlockSpec((tm, D_MODEL), lambda i: (i, 0)),
                  _resident((1, D_MODEL)),
                  _resident((k, D_MODEL))] + _ffn_specs(),
        out_specs=pl.BlockSpec((tm, D_MODEL), lambda i: (i, 0)),
        out_shape=jax.ShapeDtypeStruct((t, D_MODEL), F32),
        scratch_shapes=[pltpu.VMEM((tm, D_FF), BF16)],
        compiler_params=_params(("parallel",)),
        name="outproj_ffn",
    )(y, x, gain.reshape(1, D_MODEL), w_out.astype(BF16), ffn_gains, ffn_w_in.astype(BF16), ffn_w_out.astype(BF16))


def _rope_tables(pos_ref, freq_ref):
    ang = pos_ref[...] * freq_ref[...]
    d = lax.broadcasted_iota(jnp.int32, ang.shape, 1) % HEAD_DIM
    half = ROPE_DIMS // 2
    cos, sin = jnp.cos(ang), jnp.sin(ang)
    c = jnp.where(d < ROPE_DIMS, cos, 1.0)
    s_lo = jnp.where(d < half, -sin, 0.0)
    s_hi = jnp.where((d >= half) & (d < ROPE_DIMS), sin, 0.0)
    return c, s_lo, s_hi


def _rope(y, tables):
    c, s_lo, s_hi = tables
    half = ROPE_DIMS // 2
    return y * c + pltpu.roll(y, LANES - half, 1) * s_lo + pltpu.roll(y, half, 1) * s_hi


PROJ_CHUNK = 512
CONV_CHUNK = 256


def _proj_attn_body(*refs, layout, q_scale, has_bias, with_ffn):
    x_ref, refs = refs[0], refs[1:]
    if with_ffn:
        (fg_ref, win_ref, wout_ref), refs = refs[:3], refs[3:]
    (g_ref, pos_ref, freq_ref, w_ref), refs = refs[:4], refs[4:]
    if has_bias:
        b_ref, refs = refs[0], refs[1:]
    if with_ffn:
        x_out_ref, refs = refs[0], refs[1:]
    out_refs, tmp_ref = refs[:len(layout)], refs[len(layout)]
    rows = x_ref.shape[0]
    x = x_ref[...]
    if with_ffn:
        x = _ffn_value(x, fg_ref, win_ref, wout_ref, refs[len(layout) + 1])
        x_out_ref[...] = x
    h = _rmsnorm(x, g_ref[...]).astype(BF16)
    k_tables = _rope_tables(pos_ref, freq_ref)
    q_tables = tuple(t * q_scale for t in k_tables)
    for (kind, start, width, dilation), o_ref in zip(layout, out_refs):
        chunk = min(width, PROJ_CHUNK)
        for lo in range(0, width, chunk):
            y = _dot(h, w_ref[:, start + lo:start + lo + chunk])
            if has_bias:
                y = y + b_ref[:, start + lo:start + lo + chunk]
            for sub in range(0, chunk, LANES):
                ys = y[:, sub:sub + LANES]
                if kind == "q":
                    ys = _rope(ys, q_tables)
                elif kind == "k":
                    ys = _rope(ys, k_tables)
                if dilation == 1:
                    o_ref[0, :, lo + sub:lo + sub + LANES] = ys.astype(o_ref.dtype)
                else:
                    tmp_ref[sub // LANES] = ys
            if dilation > 1:
                for sub in range(0, chunk, LANES):
                    for r in range(dilation):
                        o_ref[r, :, lo + sub:lo + sub + LANES] = tmp_ref[
                            sub // LANES, pl.ds(r, rows // dilation, stride=dilation), :].astype(o_ref.dtype)


def _proj_attn(x, b, s, gain, pos_b, freq, w_in, b_in, layout, ffn=None):
    n = w_in.shape[1]
    tm = min(TM_PROJ, s)
    steps = s // tm
    has_bias = b_in is not None
    with_ffn = ffn is not None
    body = functools.partial(_proj_attn_body, layout=layout, q_scale=HEAD_DIM ** -0.5, has_bias=has_bias,
                             with_ffn=with_ffn)
    row = lambda width: pl.BlockSpec((tm, width), lambda bi, i: (bi * steps + i, 0))
    in_specs, args = [row(D_MODEL)], [x]
    if with_ffn:
        in_specs += _ffn_specs()
        args += [ffn[0], ffn[1].astype(BF16), ffn[2].astype(BF16)]
    in_specs += [_resident((1, D_MODEL)), row(LANES), _resident((1, LANES)), _resident((D_MODEL, n))]
    args += [gain.reshape(1, D_MODEL), pos_b, freq, w_in.astype(BF16)]
    if has_bias:
        in_specs.append(_resident((1, n)))
        args.append(b_in.reshape(1, n).astype(F32))
    out_specs = [pl.BlockSpec((None, d, tm // d, w), lambda bi, i: (bi, 0, i, 0)) for _, _, w, d in layout]
    out_shape = [jax.ShapeDtypeStruct((b, d, s // d, w), BF16) for _, _, w, d in layout]
    scratch = [pltpu.VMEM((PROJ_CHUNK // LANES, tm, LANES), F32)]
    if with_ffn:
        out_specs = [row(D_MODEL)] + out_specs
        out_shape = [jax.ShapeDtypeStruct((b * s, D_MODEL), F32)] + out_shape
        scratch.append(pltpu.VMEM((tm, D_FF), BF16))
    return pl.pallas_call(
        body,
        grid=(b, steps),
        in_specs=in_specs,
        out_specs=out_specs,
        out_shape=out_shape,
        scratch_shapes=scratch,
        compiler_params=_params(("parallel", "parallel")),
        name="ffn_proj_attn" if with_ffn else "proj_attn",
    )(*args)


def _proj_conv_body(*refs, layout, conv_width, has_bias):
    x_ref, g_ref, w_ref, convw_ref = refs[:4]
    refs = refs[4:]
    if has_bias:
        convb_ref, refs = refs[0], refs[1:]
    out_refs, (carry_ref, buf_ref) = refs[:len(layout)], refs[len(layout):]
    rows = x_ref.shape[0]

    @pl.when(pl.program_id(1) == 0)
    def _():
        carry_ref[...] = jnp.zeros_like(carry_ref)

    h = _rmsnorm(x_ref[...], g_ref[...]).astype(BF16)
    slot = 0
    for (start, width, conv_col), o_ref in zip(layout, out_refs):
        chunk = PROJ_CHUNK if width % PROJ_CHUNK == 0 else LANES
        if conv_col is not None:
            chunk = CONV_CHUNK
        for lo in range(0, width, chunk):
            y = _dot(h, w_ref[:, start + lo:start + lo + chunk])
            if conv_col is None:
                o_ref[:, lo:lo + chunk] = y
                continue
            cols = slice(conv_col + lo, conv_col + lo + chunk)
            buf = buf_ref.at[slot % buf_ref.shape[0]]
            slot += 1
            buf[0:CARRY_ROWS, 0:chunk] = carry_ref[:, cols]
            buf[CARRY_ROWS:CARRY_ROWS + rows, 0:chunk] = y
            carry_ref[:, cols] = buf[rows:rows + CARRY_ROWS, 0:chunk]
            acc = None
            for i in range(conv_width):
                first = CARRY_ROWS - (conv_width - 1) + i
                term = buf[first:first + rows, 0:chunk] * convw_ref[i:i + 1, cols]
                acc = term if acc is None else acc + term
            if has_bias:
                acc = acc + convb_ref[:, cols]
            o_ref[:, lo:lo + chunk] = _silu(acc)


def _proj_conv(x, b, s, gain, w_in, conv_w, conv_b, layout):
    n = w_in.shape[1]
    conv_ch = conv_w.shape[1]
    tm = min(TM_PROJ, s)
    steps = s // tm
    has_bias = conv_b is not None
    row = lambda width: pl.BlockSpec((tm, width), lambda bi, i: (bi * steps + i, 0))
    in_specs = [row(D_MODEL), _resident((1, D_MODEL)), _resident((D_MODEL, n)), _resident(conv_w.shape)]
    args = [x, gain.reshape(1, D_MODEL), w_in.astype(BF16), conv_w.astype(F32)]
    if has_bias:
        in_specs.append(_resident((1, conv_ch)))
        args.append(conv_b.reshape(1, conv_ch).astype(F32))
    return pl.pallas_call(
        functools.partial(_proj_conv_body, layout=layout, conv_width=conv_w.shape[0], has_bias=has_bias),
        grid=(b, steps),
        in_specs=in_specs,
        out_specs=[row(w) for _, w, _ in layout],
        out_shape=[jax.ShapeDtypeStruct((b * s, w), F32) for _, w, _ in layout],
        scratch_shapes=[pltpu.VMEM((CARRY_ROWS, conv_ch), F32),
                        pltpu.VMEM((2, CARRY_ROWS + tm, PROJ_CHUNK), F32)],
        compiler_params=_params(("parallel", "arbitrary")),
        name="proj_conv",
    )(*args)


def _attn_body(*refs, n_heads, kv_heads, n_back, n_blocks, with_sink):
    if with_sink:
        sink_ref, q_ref, kc_ref, kp_ref, vc_ref, vp_ref, o_ref = refs
    else:
        q_ref, kc_ref, kp_ref, vc_ref, vp_ref, o_ref, lse_ref = refs
    gq = n_heads // kv_heads
    assert gq % 2 == 0 and kv_heads % 2 == 0
    kw = 2 * BLOCK
    step = pl.program_id(2)
    k_all = jnp.concatenate([kp_ref[...], kc_ref[...]], axis=0)
    v_all = jnp.concatenate([vp_ref[...], vc_ref[...]], axis=0)
    rows = lax.broadcasted_iota(jnp.int32, (BLOCK, 2 * kw), 0) + BLOCK
    cols = lax.broadcasted_iota(jnp.int32, (BLOCK, 2 * kw), 1) % kw
    band = (rows - cols >= 0) & (rows - cols <= n_back)
    lane = lax.broadcasted_iota(jnp.int32, (BLOCK, LANES), 1)
    low = lane < HEAD_DIM
    key_low = lax.broadcasted_iota(jnp.int32, (kw, LANES), 1) < HEAD_DIM
    zeros_b = jnp.zeros((kw, LANES), BF16)
    ones_low, ones_high = key_low.astype(BF16), jnp.logical_not(key_low).astype(BF16)
    for j in range(n_blocks):
        r0 = j * BLOCK
        has_prev = (step * n_blocks + j) > 0
        visible = band & ((cols >= BLOCK) | has_prev)
        keys, values = [], []
        for t in range(kv_heads // 2):
            k_tile = k_all[r0:r0 + kw, t * LANES:(t + 1) * LANES]
            v_tile = v_all[r0:r0 + kw, t * LANES:(t + 1) * LANES]
            k_swap, v_swap = pltpu.roll(k_tile, HEAD_DIM, 1), pltpu.roll(v_tile, HEAD_DIM, 1)
            for first in (True, False):
                k_dup = jnp.where(key_low == first, k_tile, k_swap)
                v_dup = jnp.where(key_low == first, v_tile, v_swap)
                keys.append(jnp.concatenate([jnp.where(key_low, k_dup, zeros_b),
                                             jnp.where(key_low, zeros_b, k_dup)], axis=0))
                values.append(jnp.concatenate(
                    [jnp.concatenate([jnp.where(key_low, v_dup, zeros_b), ones_low], axis=1),
                     jnp.concatenate([jnp.where(key_low, zeros_b, v_dup), ones_high], axis=1)], axis=0))
        kv_of = [2 * p // gq for p in range(n_heads // 2)]
        lse_tile = jnp.zeros((BLOCK, LANES), F32)
        for first_pair in range(0, n_heads // 2, ATTN_PAIRS_PER_STAGE):
            pairs = range(first_pair, first_pair + ATTN_PAIRS_PER_STAGE)
            scores = {p: jnp.where(visible, _dot_nt(q_ref[r0:r0 + BLOCK, p * LANES:(p + 1) * LANES], keys[kv_of[p]]),
                                   NEG_INF) for p in pairs}
            max_a = {p: jnp.max(scores[p][:, :kw], axis=-1, keepdims=True) for p in pairs}
            max_b = {p: jnp.max(scores[p][:, kw:], axis=-1, keepdims=True) for p in pairs}
            probs = {p: jnp.concatenate([jnp.exp(scores[p][:, :kw] - max_a[p]), jnp.exp(scores[p][:, kw:] - max_b[p])],
                                        axis=1).astype(BF16) for p in pairs}
            accs = {p: _dot(probs[p], values[kv_of[p]]) for p in pairs}
            for p in pairs:
                acc, l = accs[p][:, :LANES], accs[p][:, LANES:]
                m = jnp.where(low, max_a[p], max_b[p])
                if with_sink:
                    sink = jnp.where(low, sink_ref[2 * p], sink_ref[2 * p + 1])
                    m_all = jnp.maximum(m, sink)
                    rescale = jnp.exp(m - m_all)
                    out = acc * (rescale / (l * rescale + jnp.exp(sink - m_all)))
                else:
                    out = acc / l
                    lse_tile = jnp.where((lane == 2 * p) | (lane == HEAD_DIM + 2 * p + 1), m + jnp.log(l), lse_tile)
                o_ref[r0:r0 + BLOCK, p * LANES:(p + 1) * LANES] = out.astype(o_ref.dtype)
        if not with_sink:
            lse_ref[r0:r0 + BLOCK, :] = lse_tile


def _lse_lane_of_head(h):
    return h if h % 2 == 0 else HEAD_DIM + h


def _attention(q, k, v, *, n_heads, kv_heads, n_back, sinks=None):
    b, d, length, qc = q.shape
    kc = k.shape[-1]
    assert length % BLOCK == 0
    n_blocks = ATTN_BLOCKS if length % (ATTN_BLOCKS * BLOCK) == 0 else 1
    tq = n_blocks * BLOCK
    with_sink = sinks is not None

    def cur(width):
        return pl.BlockSpec((None, None, tq, width), lambda bi, r, i: (bi, r, i, 0))

    def prev(width):
        return pl.BlockSpec((None, None, BLOCK, width), lambda bi, r, i: (bi, r, jnp.maximum(i * n_blocks - 1, 0), 0))

    in_specs = [cur(qc), cur(kc), prev(kc), cur(kc), prev(kc)]
    args = [q, k, k, v, v]
    if with_sink:
        in_specs = [pl.BlockSpec(memory_space=pltpu.SMEM)] + in_specs
        args = [sinks.astype(F32)] + args
        out_specs = cur(qc)
        out_shape = jax.ShapeDtypeStruct((b, d, length, qc), BF16)
    else:
        out_specs = [cur(qc), cur(LANES)]
        out_shape = [jax.ShapeDtypeStruct((b, d, length, qc), BF16),
                     jax.ShapeDtypeStruct((b, d, length, LANES), F32)]
    body = functools.partial(_attn_body, n_heads=n_heads, kv_heads=kv_heads, n_back=n_back,
                             n_blocks=n_blocks, with_sink=with_sink)
    return pl.pallas_call(
        body,
        grid=(b, d, length // tq),
        in_specs=in_specs,
        out_specs=out_specs,
        out_shape=out_shape,
        compiler_params=_params(("parallel", "parallel", "parallel")),
        name="attn_sink" if with_sink else "attn_lse",
    )(*args)


def _outproj_merge_body(*refs, dilations, n_heads):
    n_groups = len(dilations)
    o_refs, lse_refs = refs[:n_groups], refs[n_groups:2 * n_groups]
    x_ref, g_ref, w_ref, fg_ref, win_ref, wout_ref, out_ref, o_nat_ref, lse_nat_ref, act_ref = refs[2 * n_groups:]
    rows = x_ref.shape[0]

    def natural(ref, scratch, d):
        if d == 1:
            return ref[0].astype(F32)
        slabs = scratch.shape[0]
        for r in range(d):
            for c in range(slabs):
                scratch[c, pl.ds(r, rows // d, stride=d), :] = ref[r, :, c * LANES:(c + 1) * LANES].astype(F32)
        return jnp.concatenate([scratch[c] for c in range(slabs)], axis=1) if slabs > 1 else scratch[0]

    lses = [natural(lse_refs[g], lse_nat_ref.at[g:g + 1], d) for g, d in enumerate(dilations)]
    top = functools.reduce(jnp.maximum, lses)
    exps = [jnp.exp(s - top) for s in lses]
    den = functools.reduce(jnp.add, exps)
    n = o_refs[0].shape[-1]
    lane = lax.broadcasted_iota(jnp.int32, (LANES, n), 0)
    head_of_col = lax.broadcasted_iota(jnp.int32, (LANES, n), 1) // HEAD_DIM
    expand = functools.reduce(jnp.logical_or, [(lane == _lse_lane_of_head(h)) & (head_of_col == h)
                                               for h in range(n_heads)]).astype(BF16)
    merged = None
    for e, o_ref, d in zip(exps, o_refs, dilations):
        weights = jnp.concatenate(_split(e / den), axis=1)
        term = _dot(weights, jnp.concatenate([expand, expand], axis=0)) * natural(o_ref, o_nat_ref, d)
        merged = term if merged is None else merged + term
    x = x_ref[...] + _rmsnorm(_dot(merged.astype(BF16), w_ref[...]), g_ref[...])
    out_ref[...] = _ffn_value(x, fg_ref, win_ref, wout_ref, act_ref)


def _outproj_merge_ffn(outs, lses, x, b, s, gain, w_out, n_heads, ffn_gains, ffn_w_in, ffn_w_out):
    k = outs[0].shape[-1]
    dilations = tuple(o.shape[1] for o in outs)
    tm = min(TM_FFN, s)
    steps = s // tm
    grouped = lambda d, width: pl.BlockSpec((None, d, tm // d, width), lambda bi, i: (bi, 0, i, 0))
    row = pl.BlockSpec((tm, D_MODEL), lambda bi, i: (bi * steps + i, 0))
    return pl.pallas_call(
        functools.partial(_outproj_merge_body, dilations=dilations, n_heads=n_heads),
        grid=(b, steps),
        in_specs=[grouped(d, k) for d in dilations] + [grouped(d, LANES) for d in dilations]
                 + [row, _resident((1, D_MODEL)), _resident((k, D_MODEL))] + _ffn_specs(),
        out_specs=row,
        out_shape=jax.ShapeDtypeStruct((b * s, D_MODEL), F32),
        scratch_shapes=[pltpu.VMEM((k // LANES, tm, LANES), F32), pltpu.VMEM((len(dilations), tm, LANES), F32),
                        pltpu.VMEM((tm, D_FF), BF16)],
        compiler_params=_params(("parallel", "parallel")),
        name="outproj_merge_ffn",
    )(*outs, *lses, x, gain.reshape(1, D_MODEL), w_out.astype(BF16), ffn_gains, ffn_w_in.astype(BF16),
      ffn_w_out.astype(BF16))


def _l2norm(x):
    return x * lax.rsqrt(jnp.sum(x * x, axis=-1, keepdims=True) + NORM_EPS)


def _split(a):
    hi = a.astype(BF16)
    return hi, (a - hi.astype(F32)).astype(BF16)


def _dot_split(a, b):
    (ah, al), (bh, bl) = a, b
    return _dot(ah, bh) + (_dot(ah, bl) + _dot(al, bh))


def _unit_lower_inverses(strict_lowers):
    n = strict_lowers[0].shape[0]
    ri = lax.broadcasted_iota(jnp.int32, (n, n), 0)
    ci = lax.broadcasted_iota(jnp.int32, (n, n), 1)
    invs = None
    for level in range(int(math.log2(n))):
        joins = ((ri >> (level + 1)) == (ci >> (level + 1))) & ((ri >> level) != (ci >> level))
        couplings = [jnp.where(joins, l, 0.0) for l in strict_lowers]
        if invs is None:
            invs = [_eye(n) - c for c in couplings]
            continue
        inv_splits = [_split(inv) for inv in invs]
        right = [_dot_split(_split(c), d) for c, d in zip(couplings, inv_splits)]
        yield
        invs = [inv - _dot_split(d, _split(r)) for inv, d, r in zip(invs, inv_splits, right)]
        yield
    return invs


def _gdn_body(qkv_ref, z_ref, small_ref, alog_ref, dtb_ref, normw_ref, o_ref, state_ref):
    c = GDN_CHUNK
    nh, dk = GDN_HEADS, GDN_HEAD_DIM
    width = nh * dk
    rows = qkv_ref.shape[1]
    chunks = range(rows // c)

    @pl.when(pl.program_id(1) == 0)
    def _():
        state_ref[...] = jnp.zeros_like(state_ref)

    small = small_ref[0]
    beta_all = jax.nn.sigmoid(small)
    g_all = -jnp.exp(alog_ref[...]) * _softplus(small + dtb_ref[...])
    tril = _tril(c)
    gcum_alls = [jnp.dot(tril, g_all[j * c:(j + 1) * c], preferred_element_type=F32, precision=HI)
                 for j in chunks]
    gcum_rows = [_dot_nt(_eye(LANES), g, precision=HI) for g in gcum_alls]
    ri = lax.broadcasted_iota(jnp.int32, (c, c), 0)
    ci = lax.broadcasted_iota(jnp.int32, (c, c), 1)
    causal, strict = ri >= ci, ri > ci
    def act(c0, j):
        return qkv_ref[0, j * c:(j + 1) * c, c0:c0 + dk]

    def prepare(j):
        heads = range(nh)
        qs = [_l2norm(act(h * dk, j)) * (dk ** -0.5) for h in heads]
        yield
        ks = [_l2norm(act(width + h * dk, j)) for h in heads]
        yield
        vs = [act(2 * width + h * dk, j) for h in heads]
        betas = [beta_all[j * c:(j + 1) * c, h:h + 1] for h in heads]
        gcums = [gcum_alls[j][:, nh + h:nh + h + 1] for h in heads]
        decays = [jnp.exp(jnp.where(causal, gcums[h] - gcum_rows[j][nh + h:nh + h + 1, :], -jnp.inf)) for h in heads]
        yield
        k_bs = [k.astype(BF16) for k in ks]
        q_bs = [q.astype(BF16) for q in qs]
        lowers = [jnp.where(strict, _dot_nt(k_bs[h], k_bs[h]) * decays[h] * betas[h], 0.0) for h in heads]
        yield
        intras = [jnp.where(causal, _dot_nt(q_bs[h], k_bs[h]) * decays[h], 0.0).astype(BF16) for h in heads]
        yield
        invs = yield from _unit_lower_inverses(lowers)
        rhss = [jnp.concatenate([vs[h] * betas[h], ks[h] * (betas[h] * jnp.exp(gcums[h]))], axis=-1) for h in heads]
        yield
        sols = [_dot_split(_split(invs[h]), _split(rhss[h])) for h in heads]
        yield
        g_lasts = [g[c - 1:c, :] for g in gcums]
        return dict(
            us=[sol[:, :dk] for sol in sols],
            w_bs=[sol[:, dk:].astype(BF16) for sol in sols],
            q_ins=[(qs[h] * jnp.exp(gcums[h])).astype(BF16) for h in heads],
            k_ends=[(ks[h] * jnp.exp(g_lasts[h] - gcums[h])).astype(BF16) for h in heads],
            g_lasts=g_lasts, intras=intras)

    def finish(j, prep, states):
        heads = range(nh)
        state_bs = [s.astype(BF16) for s in states]
        v_new_bs = [(prep["us"][h] - _dot(prep["w_bs"][h], state_bs[h])).astype(BF16) for h in heads]
        outs = [_dot(prep["q_ins"][h], state_bs[h]) + _dot(prep["intras"][h], v_new_bs[h]) for h in heads]
        new_states = [states[h] * jnp.exp(prep["g_lasts"][h]) + _dot_tn(prep["k_ends"][h], v_new_bs[h]) for h in heads]
        for h in heads:
            gate = _silu(z_ref[0, j * c:(j + 1) * c, h * dk:(h + 1) * dk])
            o_ref[0, j * c:(j + 1) * c, h * dk:(h + 1) * dk] = (
                _rmsnorm(outs[h], normw_ref[...]) * gate).astype(o_ref.dtype)
        return new_states

    states = [state_ref[h] for h in range(nh)]
    pipelines = [prepare(j) for j in chunks]
    done = 0
    tick = 0
    while done < len(pipelines):
        for j in range(done, len(pipelines)):
            if tick < j * GDN_STAGE_SKEW:
                break
            try:
                next(pipelines[j])
            except StopIteration as stop:
                assert j == done
                states = finish(j, stop.value, states)
                done += 1
        tick += 1
    for h in range(nh):
        state_ref[h] = states[h]


GDN_STAGE_SKEW = 2
GDN_CHUNKS_PER_STEP = 4


def _gdn_core(qkv, z, small, a_log, dt_bias, norm_w):
    b, s, _ = qkv.shape
    nh, dk = GDN_HEADS, GDN_HEAD_DIM
    width = nh * dk
    rows = GDN_CHUNKS_PER_STEP * GDN_CHUNK
    assert s % rows == 0
    pad = lambda a: jnp.zeros((1, LANES), F32).at[0, nh:2 * nh].set(a.astype(F32))
    return pl.pallas_call(
        _gdn_body,
        grid=(b, s // rows),
        in_specs=[pl.BlockSpec((1, rows, 3 * width), lambda bi, i: (bi, i, 0)),
                  pl.BlockSpec((1, rows, width), lambda bi, i: (bi, i, 0)),
                  pl.BlockSpec((1, rows, LANES), lambda bi, i: (bi, i, 0)),
                  _resident((1, LANES)),
                  _resident((1, LANES)),
                  _resident((1, dk))],
        out_specs=pl.BlockSpec((1, rows, width), lambda bi, i: (bi, i, 0)),
        out_shape=jax.ShapeDtypeStruct((b, s, width), BF16),
        scratch_shapes=[pltpu.VMEM((nh, dk, dk), F32)],
        compiler_params=_params(("parallel", "arbitrary")),
        name="gdn",
    )(qkv, z, small, pad(a_log), pad(dt_bias), norm_w.reshape(1, dk).astype(F32))


def _ssd_chunk(r0, z_ref, xbc_ref, dt_ref, dtb_ref, alog_ref, dskip_ref, normw_ref, expand_ref, o_ref, state_ref):
    c = SSD_CHUNK
    di, ng, p = SSD_D_INNER, SSD_GROUPS, SSD_HEAD_DIM
    heads_per_group = SSD_HEADS // ng
    pairs_per_group = heads_per_group // 2
    group_width = di // ng

    def conv(c0):
        return xbc_ref[0, r0:r0 + c, c0:c0 + LANES]

    dt_all = _softplus(dt_ref[0, r0:r0 + c, :] + dtb_ref[...])
    adt = dt_all * -jnp.exp(alog_ref[...])
    acum_all = jnp.dot(_tril(c), adt, preferred_element_type=F32, precision=HI)
    acum_rows = _dot_nt(_eye(LANES), acum_all, precision=HI)
    from_start_all = jnp.exp(acum_all)
    to_end_all = jnp.exp(acum_all[c - 1:c, :] - acum_all)
    ri = lax.broadcasted_iota(jnp.int32, (c, c), 0)
    ci = lax.broadcasted_iota(jnp.int32, (c, c), 1)
    causal = ri >= ci
    low_half = lax.broadcasted_iota(jnp.int32, (c, LANES), 1) < p
    eye_b = _eye(LANES, BF16)
    groups, pairs = range(ng), range(SSD_HEADS // 2)
    group_of = [pr // pairs_per_group for pr in pairs]

    def to_channels(all_heads):
        return _dot(jnp.concatenate(_split(all_heads), axis=1), expand_ref[...])

    dt_ch, from_start_ch, to_end_ch = to_channels(dt_all), to_channels(from_start_all), to_channels(to_end_all)

    def per_head(channels, pr):
        return channels[:, pr * LANES:(pr + 1) * LANES]

    def seg(hd, cb):
        diff = acum_all[:, hd:hd + 1] - acum_rows[hd:hd + 1, :]
        return (cb * jnp.exp(jnp.where(causal, diff, -jnp.inf))).astype(BF16)

    bms = [conv(di + g * SSD_STATE).astype(BF16) for g in groups]
    cms = [conv(di + ng * SSD_STATE + g * SSD_STATE).astype(BF16) for g in groups]
    cbs = [_dot_nt(cms[g], bms[g]) for g in groups]
    bm_ts = [_dot_nt(eye_b, bms[g]).astype(BF16) for g in groups]
    xs = [conv(pr * LANES) for pr in pairs]
    x_dts = [xs[pr] * per_head(dt_ch, pr) for pr in pairs]
    x_dt_bs = [v.astype(BF16) for v in x_dts]
    y_diags = [jnp.where(low_half, _dot(seg(2 * pr, cbs[group_of[pr]]), x_dt_bs[pr]),
                         _dot(seg(2 * pr + 1, cbs[group_of[pr]]), x_dt_bs[pr])) for pr in pairs]
    states = [state_ref[pr] for pr in pairs]
    y_offs = [_dot(cms[group_of[pr]], states[pr].astype(BF16)) * per_head(from_start_ch, pr) for pr in pairs]
    for pr in pairs:
        decay_all = per_head(from_start_ch, pr)[c - 1:c, :]
        state_ref[pr] = states[pr] * decay_all + _dot(bm_ts[group_of[pr]],
                                                      (x_dts[pr] * per_head(to_end_ch, pr)).astype(BF16))
    ys = [(y_diags[pr] + y_offs[pr] + xs[pr] * dskip_ref[:, pr * LANES:(pr + 1) * LANES])
          * _silu(z_ref[0, r0:r0 + c, pr * LANES:(pr + 1) * LANES]) for pr in pairs]
    sumsqs = [jnp.sum(y * y, axis=-1, keepdims=True) for y in ys]
    for g in groups:
        members = [pr for pr in pairs if group_of[pr] == g]
        scale = lax.rsqrt(functools.reduce(jnp.add, [sumsqs[pr] for pr in members]) * (1.0 / group_width) + NORM_EPS)
        for pr in members:
            c0 = pr * LANES
            o_ref[0, r0:r0 + c, c0:c0 + LANES] = (ys[pr] * scale * normw_ref[:, c0:c0 + LANES]).astype(o_ref.dtype)


def _ssd_body(z_ref, *refs):
    state_ref = refs[-1]

    @pl.when(pl.program_id(1) == 0)
    def _():
        state_ref[...] = jnp.zeros_like(state_ref)

    for j in range(z_ref.shape[1] // SSD_CHUNK):
        _ssd_chunk(j * SSD_CHUNK, z_ref, *refs)


SSD_CHUNKS_PER_STEP = 4


def _ssd_core(z, xbc, dt_raw, dt_bias, a_log, d_skip, norm_w):
    b, s, conv_ch = xbc.shape
    c = SSD_CHUNKS_PER_STEP * SSD_CHUNK
    assert s % c == 0
    di = SSD_D_INNER
    pad = lambda a: jnp.zeros((1, LANES), F32).at[0, :SSD_HEADS].set(a.astype(F32))
    expand = (jnp.arange(2 * LANES)[:, None] % LANES == jnp.arange(di)[None, :] // SSD_HEAD_DIM).astype(BF16)
    return pl.pallas_call(
        _ssd_body,
        grid=(b, s // c),
        in_specs=[pl.BlockSpec((1, c, di), lambda bi, i: (bi, i, 0)),
                  pl.BlockSpec((1, c, conv_ch), lambda bi, i: (bi, i, 0)),
                  pl.BlockSpec((1, c, LANES), lambda bi, i: (bi, i, 0)),
                  _resident((1, LANES)),
                  _resident((1, LANES)),
                  _resident((1, di)),
                  _resident((1, di)),
                  _resident((2 * LANES, di))],
        out_specs=pl.BlockSpec((1, c, di), lambda bi, i: (bi, i, 0)),
        out_shape=jax.ShapeDtypeStruct((b, s, di), BF16),
        scratch_shapes=[pltpu.VMEM((SSD_HEADS // 2, SSD_STATE, LANES), F32)],
        compiler_params=_params(("parallel", "arbitrary")),
        name="ssd",
    )(z, xbc, dt_raw, pad(dt_bias), pad(a_log),
      jnp.repeat(d_skip.astype(F32), SSD_HEAD_DIM).reshape(1, di), norm_w.reshape(1, di).astype(F32), expand)


def _pad_cols(w, n):
    return jnp.pad(w, ((0, 0), (0, n - w.shape[1])))


def _swa_mixer(x, b, s, gain, pos_b, freq, w_in, b_in, sinks, w_out, gain_post, ffn_a, ffn):
    nq, nkv = SWA_HEADS * HEAD_DIM, SWA_KV_HEADS * HEAD_DIM
    layout = (("q", 0, nq, 1), ("k", nq, nkv, 1), ("v", nq + nkv, nkv, 1))
    x, q, k, v = _proj_attn(x, b, s, gain, pos_b, freq, w_in, b_in, layout, ffn=ffn_a)
    o = _attention(q, k, v, n_heads=SWA_HEADS, kv_heads=SWA_KV_HEADS, n_back=SWA_WINDOW - 1, sinks=sinks)
    return _outproj_ffn(o.reshape(b * s, nq), x, gain_post, w_out, *ffn)


def _dilated_mixer(x, b, s, gain, pos_b, freq, w_in, w_out, gain_post, ffn_a, ffn):
    nq, nkv = DIL_HEADS * HEAD_DIM, DIL_KV_HEADS * HEAD_DIM
    per_group = nq + 2 * nkv
    layout = []
    for gi, (_, dilation) in enumerate(DIL_PATTERN):
        base = gi * per_group
        layout += [("q", base, nq, dilation), ("k", base + nq, nkv, dilation), ("v", base + nq + nkv, nkv, dilation)]
    x, *qkv = _proj_attn(x, b, s, gain, pos_b, freq, w_in, None, tuple(layout), ffn=ffn_a)
    group_outs, group_lses = [], []
    for gi, (window, dilation) in enumerate(DIL_PATTERN):
        o, lse = _attention(*qkv[3 * gi:3 * gi + 3], n_heads=DIL_HEADS, kv_heads=DIL_KV_HEADS,
                            n_back=window // dilation)
        group_outs.append(o)
        group_lses.append(lse)
    return _outproj_merge_ffn(group_outs, group_lses, x, b, s, gain_post, w_out, DIL_HEADS, *ffn)


def _gdn_mixer(x, b, s, gain, w_in, conv_w, a_log, dt_bias, norm_w, w_out, gain_post, ffn_a, ffn):
    width = GDN_HEADS * GDN_HEAD_DIM
    x = _ffn(x, *ffn_a)
    w = jnp.concatenate([w_in[:, :4 * width], _pad_cols(w_in[:, 4 * width:], LANES)], axis=1)
    layout = ((0, 3 * width, 0), (3 * width, width, None), (4 * width, LANES, None))
    qkv, z, small = _proj_conv(x, b, s, gain, w, conv_w, None, layout)
    o = _gdn_core(qkv.reshape(b, s, 3 * width), z.reshape(b, s, width), small.reshape(b, s, LANES),
                  a_log, dt_bias, norm_w)
    return _outproj_ffn(o.reshape(b * s, width), x, gain_post, w_out, *ffn)


def _ssd_mixer(x, b, s, gain, w_in, conv_w, conv_b, dt_bias, a_log, d_skip, norm_w, w_out, gain_post, ffn_a, ffn):
    di = SSD_D_INNER
    x = _ffn(x, *ffn_a)
    conv_ch = di + 2 * SSD_GROUPS * SSD_STATE
    w = jnp.concatenate([w_in[:, :di + conv_ch], _pad_cols(w_in[:, di + conv_ch:], LANES)], axis=1)
    layout = ((0, di, None), (di, conv_ch, 0), (di + conv_ch, LANES, None))
    z, xbc, dt_raw = _proj_conv(x, b, s, gain, w, conv_w, conv_b, layout)
    o = _ssd_core(z.reshape(b, s, di), xbc.reshape(b, s, conv_ch), dt_raw.reshape(b, s, LANES),
                  dt_bias, a_log, d_skip, norm_w)
    return _outproj_ffn(o.reshape(b * s, di), x, gain_post, w_out, *ffn)


def kernel(x, positions, l0_norms, l0_ffn_w_in, l0_ffn_w_out, a_w_in, a_b_in, a_sinks, a_w_out, l1_norms, l1_ffn_w_in, l1_ffn_w_out, b_w_in, b_conv_w, b_A_log, b_dt_bias, b_norm, b_w_out, l2_norms, l2_ffn_w_in, l2_ffn_w_out, c_w_in, c_conv_w, c_conv_b, c_dt_bias, c_A_log, c_D, c_norm, c_w_out, l3_norms, l3_ffn_w_in, l3_ffn_w_out, d_w_in, d_w_out):
    b, s, d = x.shape
    t = b * s
    pos_b = jnp.broadcast_to(positions.astype(F32).reshape(t, 1), (t, LANES))
    inv_freq = ROPE_THETA ** (-jnp.arange(0, ROPE_DIMS, 2, dtype=F32) / ROPE_DIMS)
    freq = jnp.tile(inv_freq, LANES // inv_freq.shape[0]).reshape(1, LANES)

    mixers = (
        lambda h, n, fa, fb: _swa_mixer(h, b, s, n[2], pos_b, freq, a_w_in, a_b_in, a_sinks, a_w_out, n[3], fa, fb),
        lambda h, n, fa, fb: _gdn_mixer(h, b, s, n[2], b_w_in, b_conv_w, b_A_log, b_dt_bias, b_norm, b_w_out, n[3],
                                        fa, fb),
        lambda h, n, fa, fb: _ssd_mixer(h, b, s, n[2], c_w_in, c_conv_w, c_conv_b, c_dt_bias, c_A_log, c_D, c_norm,
                                        c_w_out, n[3], fa, fb),
        lambda h, n, fa, fb: _dilated_mixer(h, b, s, n[2], pos_b, freq, d_w_in, d_w_out, n[3], fa, fb),
    )
    layers = ((l0_norms, l0_ffn_w_in, l0_ffn_w_out), (l1_norms, l1_ffn_w_in, l1_ffn_w_out),
              (l2_norms, l2_ffn_w_in, l2_ffn_w_out), (l3_norms, l3_ffn_w_in, l3_ffn_w_out))
    h = x.reshape(t, d)
    for i, (norms, ffn_w_in, ffn_w_out) in enumerate(layers):
        norms = norms.astype(F32)
        h = mixers[i % len(mixers)](h, norms, (norms[0:2], ffn_w_in[0], ffn_w_out[0]),
                                    (norms[4:6], ffn_w_in[1], ffn_w_out[1]))
    return h.reshape(b, s, d)
```

```python
import functools
import math

import jax
import jax.numpy as jnp
from jax import lax
from jax.experimental import pallas as pl
from jax.experimental.pallas import tpu as pltpu

F32 = jnp.float32
BF16 = jnp.bfloat16

D_MODEL = 1024
D_FF = 2816
NORM_EPS = 1e-6
ROPE_THETA = 500000.0
ROPE_DIMS = 16
HEAD_DIM = 64
BLOCK = 128
NEG_INF = -1e30

SWA_HEADS, SWA_KV_HEADS, SWA_WINDOW = 16, 2, 128
GDN_HEADS, GDN_HEAD_DIM, GDN_CONV, GDN_CHUNK = 8, 128, 4, 64
SSD_D_INNER, SSD_HEAD_DIM, SSD_GROUPS, SSD_STATE, SSD_CONV, SSD_CHUNK = 2048, 64, 4, 128, 4, 128
SSD_HEADS = SSD_D_INNER // SSD_HEAD_DIM
DIL_PATTERN = ((128, 1), (512, 4), (2048, 16))
DIL_HEADS, DIL_KV_HEADS = 16, 4

LANES = 128
CARRY_ROWS = 8
VMEM_LIMIT = 56 * 1024 * 1024
HI = lax.Precision.HIGHEST

TM_FFN = 512
TM_PROJ = 512
FF_CHUNK = 1536
ATTN_BLOCKS = 4
ATTN_PAIRS_PER_STAGE = 8


def _params(sem):
    return pltpu.CompilerParams(dimension_semantics=sem, vmem_limit_bytes=VMEM_LIMIT)


def _resident(shape):
    nd = len(shape)
    return pl.BlockSpec(shape, lambda *_: (0,) * nd, pipeline_mode=pl.Buffered(1))


def _rmsnorm(x, gain):
    return x * lax.rsqrt(jnp.mean(x * x, axis=-1, keepdims=True) + NORM_EPS) * gain


def _silu(x):
    half = 0.5 * x
    return half * (jnp.tanh(half) + 1.0)


def _softplus(x):
    return jnp.maximum(x, 0.0) + jnp.log(1.0 + jnp.exp(-jnp.abs(x)))


def _dot(a, b):
    return jnp.dot(a, b, preferred_element_type=F32)


def _dot_nt(a, b, precision=None):
    return lax.dot_general(a, b, (((1,), (1,)), ((), ())), preferred_element_type=F32, precision=precision)


def _dot_tn(a, b):
    return lax.dot_general(a, b, (((0,), (0,)), ((), ())), preferred_element_type=F32)


def _eye(n, dtype=F32):
    return (lax.broadcasted_iota(jnp.int32, (n, n), 0) == lax.broadcasted_iota(jnp.int32, (n, n), 1)).astype(dtype)


def _tril(n, dtype=F32):
    return (lax.broadcasted_iota(jnp.int32, (n, n), 0) >= lax.broadcasted_iota(jnp.int32, (n, n), 1)).astype(dtype)


def _ffn_value(x, g_ref, win_ref, wout_ref, act_ref):
    h = _rmsnorm(x, g_ref[0:1, :]).astype(BF16)
    for lo in range(0, D_FF, FF_CHUNK):
        width = min(FF_CHUNK, D_FF - lo)
        gate = _dot(h, win_ref[:, lo:lo + width])
        up = _dot(h, win_ref[:, D_FF + lo:D_FF + lo + width])
        act_ref[:, lo:lo + width] = (_silu(gate) * up).astype(BF16)
    y = _dot(act_ref[...], wout_ref[...])
    return x + 0.5 * _rmsnorm(y, g_ref[1:2, :])


def _ffn_specs(which):
    def stacked(shape):
        return pl.BlockSpec((None,) + shape, lambda *_: (which, 0, 0), pipeline_mode=pl.Buffered(1))
    return [_resident((2, D_MODEL)), stacked((D_MODEL, 2 * D_FF)), stacked((D_FF, D_MODEL))]


def _ffn_body(x_ref, g_ref, win_ref, wout_ref, o_ref, act_ref):
    o_ref[...] = _ffn_value(x_ref[...], g_ref, win_ref, wout_ref, act_ref)


def _ffn(x, gains, w_in, w_out):
    t = x.shape[0]
    tm = min(TM_FFN, t)
    return pl.pallas_call(
        _ffn_body,
        grid=(t // tm,),
        in_specs=[pl.BlockSpec((tm, D_MODEL), lambda i: (i, 0))] + _ffn_specs(w_in[1]),
        out_specs=pl.BlockSpec((tm, D_MODEL), lambda i: (i, 0)),
        out_shape=jax.ShapeDtypeStruct((t, D_MODEL), F32),
        scratch_shapes=[pltpu.VMEM((tm, D_FF), BF16)],
        compiler_params=_params(("parallel",)),
        name="ffn",
    )(x, gains, w_in[0], w_out[0])


def _outproj_ffn_body(y_ref, x_ref, gp_ref, wo_ref, g_ref, win_ref, wout_ref, o_ref, act_ref):
    x = x_ref[...] + _rmsnorm(_dot(y_ref[...], wo_ref[...]), gp_ref[...])
    o_ref[...] = _ffn_value(x, g_ref, win_ref, wout_ref, act_ref)


def _outproj_ffn(y, x, gain, w_out, ffn_gains, ffn_w_in, ffn_w_out):
    t, k = y.shape
    tm = min(TM_FFN, t)
    return pl.pallas_call(
        _outproj_ffn_body,
        grid=(t // tm,),
        in_specs=[pl.BlockSpec((tm, k), lambda i: (i, 0)),
                  pl.BlockSpec((tm, D_MODEL), lambda i: (i, 0)),
                  _resident((1, D_MODEL)),
                  _resident((k, D_MODEL))] + _ffn_specs(ffn_w_in[1]),
        out_specs=pl.BlockSpec((tm, D_MODEL), lambda i: (i, 0)),
        out_shape=jax.ShapeDtypeStruct((t, D_MODEL), F32),
        scratch_shapes=[pltpu.VMEM((tm, D_FF), BF16)],
        compiler_params=_params(("parallel",)),
        name="outproj_ffn",
    )(y, x, gain.reshape(1, D_MODEL), w_out.astype(BF16), ffn_gains, ffn_w_in[0], ffn_w_out[0])


def _rope_tables(pos_ref, freq_ref):
    ang = pos_ref[...] * freq_ref[...]
    d = lax.broadcasted_iota(jnp.int32, ang.shape, 1) % HEAD_DIM
    half = ROPE_DIMS // 2
    cos, sin = jnp.cos(ang), jnp.sin(ang)
    c = jnp.where(d < ROPE_DIMS, cos, 1.0)
    s_lo = jnp.where(d < half, -sin, 0.0)
    s_hi = jnp.where((d >= half) & (d < ROPE_DIMS), sin, 0.0)
    return c, s_lo, s_hi


def _rope(y, tables):
    c, s_lo, s_hi = tables
    half = ROPE_DIMS // 2
    return y * c + pltpu.roll(y, LANES - half, 1) * s_lo + pltpu.roll(y, half, 1) * s_hi


PROJ_CHUNK = 512
CONV_CHUNK = 256


def _proj_attn_body(*refs, layout, q_scale, has_bias, with_ffn):
    x_ref, refs = refs[0], refs[1:]
    if with_ffn:
        (fg_ref, win_ref, wout_ref), refs = refs[:3], refs[3:]
    (g_ref, pos_ref, freq_ref, w_ref), refs = refs[:4], refs[4:]
    if has_bias:
        b_ref, refs = refs[0], refs[1:]
    if with_ffn:
        x_out_ref, refs = refs[0], refs[1:]
    out_refs, tmp_ref = refs[:len(layout)], refs[len(layout)]
    rows = x_ref.shape[0]
    x = x_ref[...]
    if with_ffn:
        x = _ffn_value(x, fg_ref, win_ref, wout_ref, refs[len(layout) + 1])
        x_out_ref[...] = x
    h = _rmsnorm(x, g_ref[...]).astype(BF16)
    k_tables = _rope_tables(pos_ref, freq_ref)
    q_tables = tuple(t * q_scale for t in k_tables)
    for (kind, start, width, dilation), o_ref in zip(layout, out_refs):
        chunk = min(width, PROJ_CHUNK)
        for lo in range(0, width, chunk):
            y = _dot(h, w_ref[:, start + lo:start + lo + chunk])
            if has_bias:
                y = y + b_ref[:, start + lo:start + lo + chunk]
            for sub in range(0, chunk, LANES):
                ys = y[:, sub:sub + LANES]
                if kind == "q":
                    ys = _rope(ys, q_tables)
                elif kind == "k":
                    ys = _rope(ys, k_tables)
                if dilation == 1:
                    o_ref[0, :, lo + sub:lo + sub + LANES] = ys.astype(o_ref.dtype)
                else:
                    tmp_ref[sub // LANES] = ys
            if dilation > 1:
                for sub in range(0, chunk, LANES):
                    for r in range(dilation):
                        o_ref[r, :, lo + sub:lo + sub + LANES] = tmp_ref[
                            sub // LANES, pl.ds(r, rows // dilation, stride=dilation), :].astype(o_ref.dtype)


def _proj_attn(x, b, s, gain, pos_b, freq, w_in, b_in, layout, ffn=None):
    n = w_in.shape[1]
    tm = min(TM_PROJ, s)
    steps = s // tm
    has_bias = b_in is not None
    with_ffn = ffn is not None
    body = functools.partial(_proj_attn_body, layout=layout, q_scale=HEAD_DIM ** -0.5, has_bias=has_bias,
                             with_ffn=with_ffn)
    row = lambda width: pl.BlockSpec((tm, width), lambda bi, i: (bi * steps + i, 0))
    in_specs, args = [row(D_MODEL)], [x]
    if with_ffn:
        in_specs += _ffn_specs(ffn[1][1])
        args += [ffn[0], ffn[1][0], ffn[2][0]]
    in_specs += [_resident((1, D_MODEL)), row(LANES), _resident((1, LANES)), _resident((D_MODEL, n))]
    args += [gain.reshape(1, D_MODEL), pos_b, freq, w_in.astype(BF16)]
    if has_bias:
        in_specs.append(_resident((1, n)))
        args.append(b_in.reshape(1, n).astype(F32))
    out_specs = [pl.BlockSpec((None, d, tm // d, w), lambda bi, i: (bi, 0, i, 0)) for _, _, w, d in layout]
    out_shape = [jax.ShapeDtypeStruct((b, d, s // d, w), BF16) for _, _, w, d in layout]
    scratch = [pltpu.VMEM((PROJ_CHUNK // LANES, tm, LANES), F32)]
    if with_ffn:
        out_specs = [row(D_MODEL)] + out_specs
        out_shape = [jax.ShapeDtypeStruct((b * s, D_MODEL), F32)] + out_shape
        scratch.append(pltpu.VMEM((tm, D_FF), BF16))
    return pl.pallas_call(
        body,
        grid=(b, steps),
        in_specs=in_specs,
        out_specs=out_specs,
        out_shape=out_shape,
        scratch_shapes=scratch,
        compiler_params=_params(("parallel", "parallel")),
        name="ffn_proj_attn" if with_ffn else "proj_attn",
    )(*args)


def _proj_conv_body(*refs, layout, conv_width, has_bias):
    x_ref, g_ref, w_ref, convw_ref = refs[:4]
    refs = refs[4:]
    if has_bias:
        convb_ref, refs = refs[0], refs[1:]
    out_refs, (carry_ref, buf_ref) = refs[:len(layout)], refs[len(layout):]
    rows = x_ref.shape[0]

    @pl.when(pl.program_id(1) == 0)
    def _():
        carry_ref[...] = jnp.zeros_like(carry_ref)

    h = _rmsnorm(x_ref[...], g_ref[...]).astype(BF16)
    slot = 0
    for (start, width, conv_col), o_ref in zip(layout, out_refs):
        chunk = PROJ_CHUNK if width % PROJ_CHUNK == 0 else LANES
        if conv_col is not None:
            chunk = CONV_CHUNK
        for lo in range(0, width, chunk):
            y = _dot(h, w_ref[:, start + lo:start + lo + chunk])
            if conv_col is None:
                o_ref[:, lo:lo + chunk] = y
                continue
            cols = slice(conv_col + lo, conv_col + lo + chunk)
            buf = buf_ref.at[slot % buf_ref.shape[0]]
            slot += 1
            buf[0:CARRY_ROWS, 0:chunk] = carry_ref[:, cols]
            buf[CARRY_ROWS:CARRY_ROWS + rows, 0:chunk] = y
            carry_ref[:, cols] = buf[rows:rows + CARRY_ROWS, 0:chunk]
            acc = None
            for i in range(conv_width):
                first = CARRY_ROWS - (conv_width - 1) + i
                term = buf[first:first + rows, 0:chunk] * convw_ref[i:i + 1, cols]
                acc = term if acc is None else acc + term
            if has_bias:
                acc = acc + convb_ref[:, cols]
            o_ref[:, lo:lo + chunk] = _silu(acc)


def _proj_conv(x, b, s, gain, w_in, conv_w, conv_b, layout):
    n = w_in.shape[1]
    conv_ch = conv_w.shape[1]
    tm = min(TM_PROJ, s)
    steps = s // tm
    has_bias = conv_b is not None
    row = lambda width: pl.BlockSpec((tm, width), lambda bi, i: (bi * steps + i, 0))
    in_specs = [row(D_MODEL), _resident((1, D_MODEL)), _resident((D_MODEL, n)), _resident(conv_w.shape)]
    args = [x, gain.reshape(1, D_MODEL), w_in.astype(BF16), conv_w.astype(F32)]
    if has_bias:
        in_specs.append(_resident((1, conv_ch)))
        args.append(conv_b.reshape(1, conv_ch).astype(F32))
    return pl.pallas_call(
        functools.partial(_proj_conv_body, layout=layout, conv_width=conv_w.shape[0], has_bias=has_bias),
        grid=(b, steps),
        in_specs=in_specs,
        out_specs=[row(w) for _, w, _ in layout],
        out_shape=[jax.ShapeDtypeStruct((b * s, w), F32) for _, w, _ in layout],
        scratch_shapes=[pltpu.VMEM((CARRY_ROWS, conv_ch), F32),
                        pltpu.VMEM((2, CARRY_ROWS + tm, PROJ_CHUNK), F32)],
        compiler_params=_params(("parallel", "arbitrary")),
        name="proj_conv",
    )(*args)


def _attn_body(*refs, n_heads, kv_heads, n_back, n_blocks, with_sink):
    if with_sink:
        sink_ref, q_ref, kc_ref, kp_ref, vc_ref, vp_ref, o_ref = refs
    else:
        q_ref, kc_ref, kp_ref, vc_ref, vp_ref, o_ref, lse_ref = refs
    gq = n_heads // kv_heads
    assert gq % 2 == 0 and kv_heads % 2 == 0
    kw = 2 * BLOCK
    step = pl.program_id(2)
    k_all = jnp.concatenate([kp_ref[...], kc_ref[...]], axis=0)
    v_all = jnp.concatenate([vp_ref[...], vc_ref[...]], axis=0)
    rows = lax.broadcasted_iota(jnp.int32, (BLOCK, 2 * kw), 0) + BLOCK
    cols = lax.broadcasted_iota(jnp.int32, (BLOCK, 2 * kw), 1) % kw
    band = (rows - cols >= 0) & (rows - cols <= n_back)
    lane = lax.broadcasted_iota(jnp.int32, (BLOCK, LANES), 1)
    low = lane < HEAD_DIM
    key_low = lax.broadcasted_iota(jnp.int32, (kw, LANES), 1) < HEAD_DIM
    zeros_b = jnp.zeros((kw, LANES), BF16)
    ones_low, ones_high = key_low.astype(BF16), jnp.logical_not(key_low).astype(BF16)
    for j in range(n_blocks):
        r0 = j * BLOCK
        has_prev = (step * n_blocks + j) > 0
        visible = band & ((cols >= BLOCK) | has_prev)
        keys, values = [], []
        for t in range(kv_heads // 2):
            k_tile = k_all[r0:r0 + kw, t * LANES:(t + 1) * LANES]
            v_tile = v_all[r0:r0 + kw, t * LANES:(t + 1) * LANES]
            k_swap, v_swap = pltpu.roll(k_tile, HEAD_DIM, 1), pltpu.roll(v_tile, HEAD_DIM, 1)
            for first in (True, False):
                k_dup = jnp.where(key_low == first, k_tile, k_swap)
                v_dup = jnp.where(key_low == first, v_tile, v_swap)
                keys.append(jnp.concatenate([jnp.where(key_low, k_dup, zeros_b),
                                             jnp.where(key_low, zeros_b, k_dup)], axis=0))
                values.append(jnp.concatenate(
                    [jnp.concatenate([jnp.where(key_low, v_dup, zeros_b), ones_low], axis=1),
                     jnp.concatenate([jnp.where(key_low, zeros_b, v_dup), ones_high], axis=1)], axis=0))
        kv_of = [2 * p // gq for p in range(n_heads // 2)]
        lse_tile = jnp.zeros((BLOCK, LANES), F32)
        for first_pair in range(0, n_heads // 2, ATTN_PAIRS_PER_STAGE):
            pairs = range(first_pair, first_pair + ATTN_PAIRS_PER_STAGE)
            scores = {p: jnp.where(visible, _dot_nt(q_ref[r0:r0 + BLOCK, p * LANES:(p + 1) * LANES], keys[kv_of[p]]),
                                   NEG_INF) for p in pairs}
            max_a = {p: jnp.max(scores[p][:, :kw], axis=-1, keepdims=True) for p in pairs}
            max_b = {p: jnp.max(scores[p][:, kw:], axis=-1, keepdims=True) for p in pairs}
            probs = {p: jnp.concatenate([jnp.exp(scores[p][:, :kw] - max_a[p]), jnp.exp(scores[p][:, kw:] - max_b[p])],
                                        axis=1).astype(BF16) for p in pairs}
            accs = {p: _dot(probs[p], values[kv_of[p]]) for p in pairs}
            for p in pairs:
                acc, l = accs[p][:, :LANES], accs[p][:, LANES:]
                m = jnp.where(low, max_a[p], max_b[p])
                if with_sink:
                    sink = jnp.where(low, sink_ref[2 * p], sink_ref[2 * p + 1])
                    m_all = jnp.maximum(m, sink)
                    rescale = jnp.exp(m - m_all)
                    out = acc * (rescale / (l * rescale + jnp.exp(sink - m_all)))
                else:
                    out = acc / l
                    lse_tile = jnp.where((lane == 2 * p) | (lane == HEAD_DIM + 2 * p + 1), m + jnp.log(l), lse_tile)
                o_ref[r0:r0 + BLOCK, p * LANES:(p + 1) * LANES] = out.astype(o_ref.dtype)
        if not with_sink:
            lse_ref[r0:r0 + BLOCK, :] = lse_tile


def _lse_lane_of_head(h):
    return h if h % 2 == 0 else HEAD_DIM + h


def _attention(q, k, v, *, n_heads, kv_heads, n_back, sinks=None):
    b, d, length, qc = q.shape
    kc = k.shape[-1]
    assert length % BLOCK == 0
    n_blocks = ATTN_BLOCKS if length % (ATTN_BLOCKS * BLOCK) == 0 else 1
    tq = n_blocks * BLOCK
    with_sink = sinks is not None

    def cur(width):
        return pl.BlockSpec((None, None, tq, width), lambda bi, r, i: (bi, r, i, 0))

    def prev(width):
        return pl.BlockSpec((None, None, BLOCK, width), lambda bi, r, i: (bi, r, jnp.maximum(i * n_blocks - 1, 0), 0))

    in_specs = [cur(qc), cur(kc), prev(kc), cur(kc), prev(kc)]
    args = [q, k, k, v, v]
    if with_sink:
        in_specs = [pl.BlockSpec(memory_space=pltpu.SMEM)] + in_specs
        args = [sinks.astype(F32)] + args
        out_specs = cur(qc)
        out_shape = jax.ShapeDtypeStruct((b, d, length, qc), BF16)
    else:
        out_specs = [cur(qc), cur(LANES)]
        out_shape = [jax.ShapeDtypeStruct((b, d, length, qc), BF16),
                     jax.ShapeDtypeStruct((b, d, length, LANES), F32)]
    body = functools.partial(_attn_body, n_heads=n_heads, kv_heads=kv_heads, n_back=n_back,
                             n_blocks=n_blocks, with_sink=with_sink)
    return pl.pallas_call(
        body,
        grid=(b, d, length // tq),
        in_specs=in_specs,
        out_specs=out_specs,
        out_shape=out_shape,
        compiler_params=_params(("parallel", "parallel", "parallel")),
        name="attn_sink" if with_sink else "attn_lse",
    )(*args)


def _outproj_merge_body(*refs, dilations, n_heads):
    n_groups = len(dilations)
    o_refs, lse_refs = refs[:n_groups], refs[n_groups:2 * n_groups]
    x_ref, g_ref, w_ref, fg_ref, win_ref, wout_ref, out_ref, o_nat_ref, lse_nat_ref, act_ref = refs[2 * n_groups:]
    rows = x_ref.shape[0]

    def natural(ref, scratch, d):
        if d == 1:
            return ref[0].astype(F32)
        slabs = scratch.shape[0]
        for r in range(d):
            for c in range(slabs):
                scratch[c, pl.ds(r, rows // d, stride=d), :] = ref[r, :, c * LANES:(c + 1) * LANES].astype(F32)
        return jnp.concatenate([scratch[c] for c in range(slabs)], axis=1) if slabs > 1 else scratch[0]

    lses = [natural(lse_refs[g], lse_nat_ref.at[g:g + 1], d) for g, d in enumerate(dilations)]
    top = functools.reduce(jnp.maximum, lses)
    exps = [jnp.exp(s - top) for s in lses]
    den = functools.reduce(jnp.add, exps)
    n = o_refs[0].shape[-1]
    lane = lax.broadcasted_iota(jnp.int32, (LANES, n), 0)
    head_of_col = lax.broadcasted_iota(jnp.int32, (LANES, n), 1) // HEAD_DIM
    expand = functools.reduce(jnp.logical_or, [(lane == _lse_lane_of_head(h)) & (head_of_col == h)
                                               for h in range(n_heads)]).astype(BF16)
    merged = None
    for e, o_ref, d in zip(exps, o_refs, dilations):
        weights = jnp.concatenate(_split(e / den), axis=1)
        term = _dot(weights, jnp.concatenate([expand, expand], axis=0)) * natural(o_ref, o_nat_ref, d)
        merged = term if merged is None else merged + term
    x = x_ref[...] + _rmsnorm(_dot(merged.astype(BF16), w_ref[...]), g_ref[...])
    out_ref[...] = _ffn_value(x, fg_ref, win_ref, wout_ref, act_ref)


def _outproj_merge_ffn(outs, lses, x, b, s, gain, w_out, n_heads, ffn_gains, ffn_w_in, ffn_w_out):
    k = outs[0].shape[-1]
    dilations = tuple(o.shape[1] for o in outs)
    tm = min(TM_FFN, s)
    steps = s // tm
    grouped = lambda d, width: pl.BlockSpec((None, d, tm // d, width), lambda bi, i: (bi, 0, i, 0))
    row = pl.BlockSpec((tm, D_MODEL), lambda bi, i: (bi * steps + i, 0))
    return pl.pallas_call(
        functools.partial(_outproj_merge_body, dilations=dilations, n_heads=n_heads),
        grid=(b, steps),
        in_specs=[grouped(d, k) for d in dilations] + [grouped(d, LANES) for d in dilations]
                 + [row, _resident((1, D_MODEL)), _resident((k, D_MODEL))] + _ffn_specs(ffn_w_in[1]),
        out_specs=row,
        out_shape=jax.ShapeDtypeStruct((b * s, D_MODEL), F32),
        scratch_shapes=[pltpu.VMEM((k // LANES, tm, LANES), F32), pltpu.VMEM((len(dilations), tm, LANES), F32),
                        pltpu.VMEM((tm, D_FF), BF16)],
        compiler_params=_params(("parallel", "parallel")),
        name="outproj_merge_ffn",
    )(*outs, *lses, x, gain.reshape(1, D_MODEL), w_out.astype(BF16), ffn_gains, ffn_w_in[0], ffn_w_out[0])


def _l2norm(x):
    return x * lax.rsqrt(jnp.sum(x * x, axis=-1, keepdims=True) + NORM_EPS)


def _split(a):
    hi = a.astype(BF16)
    return hi, (a - hi.astype(F32)).astype(BF16)


def _dot_split(a, b):
    (ah, al), (bh, bl) = a, b
    return _dot(ah, bh) + (_dot(ah, bl) + _dot(al, bh))


def _unit_lower_inverses(strict_lowers):
    n = strict_lowers[0].shape[0]
    ri = lax.broadcasted_iota(jnp.int32, (n, n), 0)
    ci = lax.broadcasted_iota(jnp.int32, (n, n), 1)
    invs = None
    for level in range(int(math.log2(n))):
        joins = ((ri >> (level + 1)) == (ci >> (level + 1))) & ((ri >> level) != (ci >> level))
        couplings = [jnp.where(joins, l, 0.0) for l in strict_lowers]
        if invs is None:
            invs = [_eye(n) - c for c in couplings]
            continue
        inv_splits = [_split(inv) for inv in invs]
        right = [_dot_split(_split(c), d) for c, d in zip(couplings, inv_splits)]
        yield
        invs = [inv - _dot_split(d, _split(r)) for inv, d, r in zip(invs, inv_splits, right)]
        yield
    return invs


def _gdn_body(qkv_ref, z_ref, small_ref, alog_ref, dtb_ref, normw_ref, o_ref, state_ref):
    c = GDN_CHUNK
    nh, dk = GDN_HEADS, GDN_HEAD_DIM
    width = nh * dk
    rows = qkv_ref.shape[1]
    chunks = range(rows // c)

    @pl.when(pl.program_id(1) == 0)
    def _():
        state_ref[...] = jnp.zeros_like(state_ref)

    small = small_ref[0]
    beta_all = jax.nn.sigmoid(small)
    g_all = -jnp.exp(alog_ref[...]) * _softplus(small + dtb_ref[...])
    tril = _tril(c)
    gcum_alls = [jnp.dot(tril, g_all[j * c:(j + 1) * c], preferred_element_type=F32, precision=HI)
                 for j in chunks]
    gcum_rows = [_dot_nt(_eye(LANES), g, precision=HI) for g in gcum_alls]
    ri = lax.broadcasted_iota(jnp.int32, (c, c), 0)
    ci = lax.broadcasted_iota(jnp.int32, (c, c), 1)
    causal, strict = ri >= ci, ri > ci
    def act(c0, j):
        return qkv_ref[0, j * c:(j + 1) * c, c0:c0 + dk]

    def prepare(j):
        heads = range(nh)
        qs = [_l2norm(act(h * dk, j)) * (dk ** -0.5) for h in heads]
        yield
        ks = [_l2norm(act(width + h * dk, j)) for h in heads]
        yield
        vs = [act(2 * width + h * dk, j) for h in heads]
        betas = [beta_all[j * c:(j + 1) * c, h:h + 1] for h in heads]
        gcums = [gcum_alls[j][:, nh + h:nh + h + 1] for h in heads]
        decays = [jnp.exp(jnp.where(causal, gcums[h] - gcum_rows[j][nh + h:nh + h + 1, :], -jnp.inf)) for h in heads]
        yield
        k_bs = [k.astype(BF16) for k in ks]
        q_bs = [q.astype(BF16) for q in qs]
        lowers = [jnp.where(strict, _dot_nt(k_bs[h], k_bs[h]) * decays[h] * betas[h], 0.0) for h in heads]
        yield
        intras = [jnp.where(causal, _dot_nt(q_bs[h], k_bs[h]) * decays[h], 0.0).astype(BF16) for h in heads]
        yield
        invs = yield from _unit_lower_inverses(lowers)
        rhss = [jnp.concatenate([vs[h] * betas[h], ks[h] * (betas[h] * jnp.exp(gcums[h]))], axis=-1) for h in heads]
        yield
        sols = [_dot_split(_split(invs[h]), _split(rhss[h])) for h in heads]
        yield
        g_lasts = [g[c - 1:c, :] for g in gcums]
        return dict(
            us=[sol[:, :dk] for sol in sols],
            w_bs=[sol[:, dk:].astype(BF16) for sol in sols],
            q_ins=[(qs[h] * jnp.exp(gcums[h])).astype(BF16) for h in heads],
            k_ends=[(ks[h] * jnp.exp(g_lasts[h] - gcums[h])).astype(BF16) for h in heads],
            g_lasts=g_lasts, intras=intras)

    def finish(j, prep, states):
        heads = range(nh)
        state_bs = [s.astype(BF16) for s in states]
        v_new_bs = [(prep["us"][h] - _dot(prep["w_bs"][h], state_bs[h])).astype(BF16) for h in heads]
        outs = [_dot(prep["q_ins"][h], state_bs[h]) + _dot(prep["intras"][h], v_new_bs[h]) for h in heads]
        new_states = [states[h] * jnp.exp(prep["g_lasts"][h]) + _dot_tn(prep["k_ends"][h], v_new_bs[h]) for h in heads]
        for h in heads:
            gate = _silu(z_ref[0, j * c:(j + 1) * c, h * dk:(h + 1) * dk])
            o_ref[0, j * c:(j + 1) * c, h * dk:(h + 1) * dk] = (
                _rmsnorm(outs[h], normw_ref[...]) * gate).astype(o_ref.dtype)
        return new_states

    states = [state_ref[h] for h in range(nh)]
    pipelines = [prepare(j) for j in chunks]
    done = 0
    tick = 0
    while done < len(pipelines):
        for j in range(done, len(pipelines)):
            if tick < j * GDN_STAGE_SKEW:
                break
            try:
                next(pipelines[j])
            except StopIteration as stop:
                assert j == done
                states = finish(j, stop.value, states)
                done += 1
        tick += 1
    for h in range(nh):
        state_ref[h] = states[h]


GDN_STAGE_SKEW = 2
GDN_CHUNKS_PER_STEP = 4


def _gdn_core(qkv, z, small, a_log, dt_bias, norm_w):
    b, s, _ = qkv.shape
    nh, dk = GDN_HEADS, GDN_HEAD_DIM
    width = nh * dk
    rows = GDN_CHUNKS_PER_STEP * GDN_CHUNK
    assert s % rows == 0
    pad = lambda a: jnp.zeros((1, LANES), F32).at[0, nh:2 * nh].set(a.astype(F32))
    return pl.pallas_call(
        _gdn_body,
        grid=(b, s // rows),
        in_specs=[pl.BlockSpec((1, rows, 3 * width), lambda bi, i: (bi, i, 0)),
                  pl.BlockSpec((1, rows, width), lambda bi, i: (bi, i, 0)),
                  pl.BlockSpec((1, rows, LANES), lambda bi, i: (bi, i, 0)),
                  _resident((1, LANES)),
                  _resident((1, LANES)),
                  _resident((1, dk))],
        out_specs=pl.BlockSpec((1, rows, width), lambda bi, i: (bi, i, 0)),
        out_shape=jax.ShapeDtypeStruct((b, s, width), BF16),
        scratch_shapes=[pltpu.VMEM((nh, dk, dk), F32)],
        compiler_params=_params(("parallel", "arbitrary")),
        name="gdn",
    )(qkv, z, small, pad(a_log), pad(dt_bias), norm_w.reshape(1, dk).astype(F32))


def _ssd_chunk(r0, z_ref, xbc_ref, dt_ref, dtb_ref, alog_ref, dskip_ref, normw_ref, expand_ref, o_ref, state_ref):
    c = SSD_CHUNK
    di, ng, p = SSD_D_INNER, SSD_GROUPS, SSD_HEAD_DIM
    heads_per_group = SSD_HEADS // ng
    pairs_per_group = heads_per_group // 2
    group_width = di // ng

    def conv(c0):
        return xbc_ref[0, r0:r0 + c, c0:c0 + LANES]

    dt_all = _softplus(dt_ref[0, r0:r0 + c, :] + dtb_ref[...])
    adt = dt_all * -jnp.exp(alog_ref[...])
    acum_all = jnp.dot(_tril(c), adt, preferred_element_type=F32, precision=HI)
    acum_rows = _dot_nt(_eye(LANES), acum_all, precision=HI)
    from_start_all = jnp.exp(acum_all)
    to_end_all = jnp.exp(acum_all[c - 1:c, :] - acum_all)
    ri = lax.broadcasted_iota(jnp.int32, (c, c), 0)
    ci = lax.broadcasted_iota(jnp.int32, (c, c), 1)
    causal = ri >= ci
    low_half = lax.broadcasted_iota(jnp.int32, (c, LANES), 1) < p
    eye_b = _eye(LANES, BF16)
    groups, pairs = range(ng), range(SSD_HEADS // 2)
    group_of = [pr // pairs_per_group for pr in pairs]

    def to_channels(all_heads):
        return _dot(jnp.concatenate(_split(all_heads), axis=1), expand_ref[...])

    dt_ch, from_start_ch, to_end_ch = to_channels(dt_all), to_channels(from_start_all), to_channels(to_end_all)

    def per_head(channels, pr):
        return channels[:, pr * LANES:(pr + 1) * LANES]

    def seg(hd, cb):
        diff = acum_all[:, hd:hd + 1] - acum_rows[hd:hd + 1, :]
        return (cb * jnp.exp(jnp.where(causal, diff, -jnp.inf))).astype(BF16)

    bms = [conv(di + g * SSD_STATE).astype(BF16) for g in groups]
    cms = [conv(di + ng * SSD_STATE + g * SSD_STATE).astype(BF16) for g in groups]
    cbs = [_dot_nt(cms[g], bms[g]) for g in groups]
    bm_ts = [_dot_nt(eye_b, bms[g]).astype(BF16) for g in groups]
    xs = [conv(pr * LANES) for pr in pairs]
    x_dts = [xs[pr] * per_head(dt_ch, pr) for pr in pairs]
    x_dt_bs = [v.astype(BF16) for v in x_dts]
    y_diags = [jnp.where(low_half, _dot(seg(2 * pr, cbs[group_of[pr]]), x_dt_bs[pr]),
                         _dot(seg(2 * pr + 1, cbs[group_of[pr]]), x_dt_bs[pr])) for pr in pairs]
    states = [state_ref[pr] for pr in pairs]
    y_offs = [_dot(cms[group_of[pr]], states[pr].astype(BF16)) * per_head(from_start_ch, pr) for pr in pairs]
    for pr in pairs:
        decay_all = per_head(from_start_ch, pr)[c - 1:c, :]
        state_ref[pr] = states[pr] * decay_all + _dot(bm_ts[group_of[pr]],
                                                      (x_dts[pr] * per_head(to_end_ch, pr)).astype(BF16))
    ys = [(y_diags[pr] + y_offs[pr] + xs[pr] * dskip_ref[:, pr * LANES:(pr + 1) * LANES])
          * _silu(z_ref[0, r0:r0 + c, pr * LANES:(pr + 1) * LANES]) for pr in pairs]
    sumsqs = [jnp.sum(y * y, axis=-1, keepdims=True) for y in ys]
    for g in groups:
        members = [pr for pr in pairs if group_of[pr] == g]
        scale = lax.rsqrt(functools.reduce(jnp.add, [sumsqs[pr] for pr in members]) * (1.0 / group_width) + NORM_EPS)
        for pr in members:
            c0 = pr * LANES
            o_ref[0, r0:r0 + c, c0:c0 + LANES] = (ys[pr] * scale * normw_ref[:, c0:c0 + LANES]).astype(o_ref.dtype)


def _ssd_body(z_ref, *refs):
    state_ref = refs[-1]

    @pl.when(pl.program_id(1) == 0)
    def _():
        state_ref[...] = jnp.zeros_like(state_ref)

    for j in range(z_ref.shape[1] // SSD_CHUNK):
        _ssd_chunk(j * SSD_CHUNK, z_ref, *refs)


SSD_CHUNKS_PER_STEP = 4


def _ssd_core(z, xbc, dt_raw, dt_bias, a_log, d_skip, norm_w):
    b, s, conv_ch = xbc.shape
    c = SSD_CHUNKS_PER_STEP * SSD_CHUNK
    assert s % c == 0
    di = SSD_D_INNER
    pad = lambda a: jnp.zeros((1, LANES), F32).at[0, :SSD_HEADS].set(a.astype(F32))
    expand = (jnp.arange(2 * LANES)[:, None] % LANES == jnp.arange(di)[None, :] // SSD_HEAD_DIM).astype(BF16)
    return pl.pallas_call(
        _ssd_body,
        grid=(b, s // c),
        in_specs=[pl.BlockSpec((1, c, di), lambda bi, i: (bi, i, 0)),
                  pl.BlockSpec((1, c, conv_ch), lambda bi, i: (bi, i, 0)),
                  pl.BlockSpec((1, c, LANES), lambda bi, i: (bi, i, 0)),
                  _resident((1, LANES)),
                  _resident((1, LANES)),
                  _resident((1, di)),
                  _resident((1, di)),
                  _resident((2 * LANES, di))],
        out_specs=pl.BlockSpec((1, c, di), lambda bi, i: (bi, i, 0)),
        out_shape=jax.ShapeDtypeStruct((b, s, di), BF16),
        scratch_shapes=[pltpu.VMEM((SSD_HEADS // 2, SSD_STATE, LANES), F32)],
        compiler_params=_params(("parallel", "arbitrary")),
        name="ssd",
    )(z, xbc, dt_raw, pad(dt_bias), pad(a_log),
      jnp.repeat(d_skip.astype(F32), SSD_HEAD_DIM).reshape(1, di), norm_w.reshape(1, di).astype(F32), expand)


def _pad_cols(w, n):
    return jnp.pad(w, ((0, 0), (0, n - w.shape[1])))


def _swa_mixer(x, b, s, gain, pos_b, freq, w_in, b_in, sinks, w_out, gain_post, ffn_a, ffn):
    nq, nkv = SWA_HEADS * HEAD_DIM, SWA_KV_HEADS * HEAD_DIM
    layout = (("q", 0, nq, 1), ("k", nq, nkv, 1), ("v", nq + nkv, nkv, 1))
    x, q, k, v = _proj_attn(x, b, s, gain, pos_b, freq, w_in, b_in, layout, ffn=ffn_a)
    o = _attention(q, k, v, n_heads=SWA_HEADS, kv_heads=SWA_KV_HEADS, n_back=SWA_WINDOW - 1, sinks=sinks)
    return _outproj_ffn(o.reshape(b * s, nq), x, gain_post, w_out, *ffn)


def _dilated_mixer(x, b, s, gain, pos_b, freq, w_in, w_out, gain_post, ffn_a, ffn):
    nq, nkv = DIL_HEADS * HEAD_DIM, DIL_KV_HEADS * HEAD_DIM
    per_group = nq + 2 * nkv
    layout = []
    for gi, (_, dilation) in enumerate(DIL_PATTERN):
        base = gi * per_group
        layout += [("q", base, nq, dilation), ("k", base + nq, nkv, dilation), ("v", base + nq + nkv, nkv, dilation)]
    x, *qkv = _proj_attn(x, b, s, gain, pos_b, freq, w_in, None, tuple(layout), ffn=ffn_a)
    group_outs, group_lses = [], []
    for gi, (window, dilation) in enumerate(DIL_PATTERN):
        o, lse = _attention(*qkv[3 * gi:3 * gi + 3], n_heads=DIL_HEADS, kv_heads=DIL_KV_HEADS,
                            n_back=window // dilation)
        group_outs.append(o)
        group_lses.append(lse)
    return _outproj_merge_ffn(group_outs, group_lses, x, b, s, gain_post, w_out, DIL_HEADS, *ffn)


def _gdn_mixer(x, b, s, gain, w_in, conv_w, a_log, dt_bias, norm_w, w_out, gain_post, ffn_a, ffn):
    width = GDN_HEADS * GDN_HEAD_DIM
    x = _ffn(x, *ffn_a)
    w = jnp.concatenate([w_in[:, :4 * width], _pad_cols(w_in[:, 4 * width:], LANES)], axis=1)
    layout = ((0, 3 * width, 0), (3 * width, width, None), (4 * width, LANES, None))
    qkv, z, small = _proj_conv(x, b, s, gain, w, conv_w, None, layout)
    o = _gdn_core(qkv.reshape(b, s, 3 * width), z.reshape(b, s, width), small.reshape(b, s, LANES),
                  a_log, dt_bias, norm_w)
    return _outproj_ffn(o.reshape(b * s, width), x, gain_post, w_out, *ffn)


def _ssd_mixer(x, b, s, gain, w_in, conv_w, conv_b, dt_bias, a_log, d_skip, norm_w, w_out, gain_post, ffn_a, ffn):
    di = SSD_D_INNER
    x = _ffn(x, *ffn_a)
    conv_ch = di + 2 * SSD_GROUPS * SSD_STATE
    w = jnp.concatenate([w_in[:, :di + conv_ch], _pad_cols(w_in[:, di + conv_ch:], LANES)], axis=1)
    layout = ((0, di, None), (di, conv_ch, 0), (di + conv_ch, LANES, None))
    z, xbc, dt_raw = _proj_conv(x, b, s, gain, w, conv_w, conv_b, layout)
    o = _ssd_core(z.reshape(b, s, di), xbc.reshape(b, s, conv_ch), dt_raw.reshape(b, s, LANES),
                  dt_bias, a_log, d_skip, norm_w)
    return _outproj_ffn(o.reshape(b * s, di), x, gain_post, w_out, *ffn)


def kernel(x, positions, l0_norms, l0_ffn_w_in, l0_ffn_w_out, a_w_in, a_b_in, a_sinks, a_w_out, l1_norms, l1_ffn_w_in, l1_ffn_w_out, b_w_in, b_conv_w, b_A_log, b_dt_bias, b_norm, b_w_out, l2_norms, l2_ffn_w_in, l2_ffn_w_out, c_w_in, c_conv_w, c_conv_b, c_dt_bias, c_A_log, c_D, c_norm, c_w_out, l3_norms, l3_ffn_w_in, l3_ffn_w_out, d_w_in, d_w_out):
    b, s, d = x.shape
    t = b * s
    pos_b = jnp.broadcast_to(positions.astype(F32).reshape(t, 1), (t, LANES))
    inv_freq = ROPE_THETA ** (-jnp.arange(0, ROPE_DIMS, 2, dtype=F32) / ROPE_DIMS)
    freq = jnp.tile(inv_freq, LANES // inv_freq.shape[0]).reshape(1, LANES)

    mixers = (
        lambda h, n, fa, fb: _swa_mixer(h, b, s, n[2], pos_b, freq, a_w_in, a_b_in, a_sinks, a_w_out, n[3], fa, fb),
        lambda h, n, fa, fb: _gdn_mixer(h, b, s, n[2], b_w_in, b_conv_w, b_A_log, b_dt_bias, b_norm, b_w_out, n[3],
                                        fa, fb),
        lambda h, n, fa, fb: _ssd_mixer(h, b, s, n[2], c_w_in, c_conv_w, c_conv_b, c_dt_bias, c_A_log, c_D, c_norm,
                                        c_w_out, n[3], fa, fb),
        lambda h, n, fa, fb: _dilated_mixer(h, b, s, n[2], pos_b, freq, d_w_in, d_w_out, n[3], fa, fb),
    )
    layers = ((l0_norms, l0_ffn_w_in, l0_ffn_w_out), (l1_norms, l1_ffn_w_in, l1_ffn_w_out),
              (l2_norms, l2_ffn_w_in, l2_ffn_w_out), (l3_norms, l3_ffn_w_in, l3_ffn_w_out))
    h = x.reshape(t, d)
    for i, (norms, ffn_w_in, ffn_w_out) in enumerate(layers):
        norms = norms.astype(F32)
        w_in_b, w_out_b = ffn_w_in.astype(BF16), ffn_w_out.astype(BF16)
        h = mixers[i % len(mixers)](h, norms, (norms[0:2], (w_in_b, 0), (w_out_b, 0)),
                                    (norms[4:6], (w_in_b, 1), (w_out_b, 1)))
    return h.reshape(b, s, d)
```
